```python
import jax, jax.numpy as jnp
from jax import lax
import numpy as np

D_MODEL = 2048
BATCH = 4
SEQ = 8192
DEPTH = 1
DEC_BATCH = 8
DEC_SEQ = 64
PAST_LEN = 1024

CHUNK = 64
WINDOW = 128
WIN_CHUNKS = WINDOW // CHUNK
SWA_HEADS = 16
SWA_KV_HEADS = 4
SWA_GROUP = SWA_HEADS // SWA_KV_HEADS
SWA_HEAD_DIM = 64
RET_HEADS = 8
RET_QK_DIM = 128
RET_V_DIM = 128
RET_ROPE_BASE = 10000.0
N_MEM = 256
MEM_HEADS = 4
MEM_HEAD_DIM = 256
D_FF = 5632
CONV_W = 3
N_BRANCH = 3
EPS = 1e-6
NEG = -1e30

SWA_Q = SWA_HEADS * SWA_HEAD_DIM
SWA_KV = SWA_KV_HEADS * SWA_HEAD_DIM
RET_QK = RET_HEADS * RET_QK_DIM
RET_V = RET_HEADS * RET_V_DIM
MEM_W = MEM_HEADS * MEM_HEAD_DIM
IN_SPLITS = (SWA_Q, SWA_KV, SWA_KV, RET_QK, RET_QK, RET_V, RET_V, MEM_W, N_BRANCH * D_MODEL)
IN_COLS = sum(IN_SPLITS)
MIX_W = SWA_Q + RET_V + MEM_W

kernel_name = "hybrid_streaming_swa_retention_step"


def rmsnorm(x, g):
    xf = x.astype(jnp.float32)
    y = xf * lax.rsqrt(jnp.mean(xf * xf, axis=-1, keepdims=True) + EPS)
    return (y * g.astype(jnp.float32)).astype(x.dtype)


def rotary(x, pos):
    half = x.shape[-1] // 2
    inv_freq = 1.0 / (RET_ROPE_BASE ** jnp.linspace(0.0, 1.0, half, dtype=jnp.float32))
    ang = pos.astype(jnp.float32)[:, None] * inv_freq[None, :]
    cos = jnp.cos(ang)[None, :, None, :]
    sin = jnp.sin(ang)[None, :, None, :]
    xf = x.astype(jnp.float32)
    x1, x2 = xf[..., :half], xf[..., half:]
    return jnp.concatenate([x1 * cos - x2 * sin, x1 * sin + x2 * cos], axis=-1).astype(x.dtype)


def sink_softmax(s, sink):
    sk = sink.astype(jnp.float32)[:, :, None, None]
    m = jnp.maximum(jnp.max(s, axis=-1, keepdims=True), sk)
    p = jnp.exp(s - m)
    return p / (jnp.sum(p, axis=-1, keepdims=True) + jnp.exp(sk - m))


def swa_banded(q, k, v, sink):
    B, S = q.shape[:2]
    nc = S // CHUNK
    qb = q.reshape(B, nc, CHUNK, SWA_KV_HEADS, SWA_GROUP, SWA_HEAD_DIM)

    def band(x):
        xb = jnp.pad(x.reshape(B, nc, CHUNK, SWA_KV_HEADS, SWA_HEAD_DIM),
                     ((0, 0), (WIN_CHUNKS, 0), (0, 0), (0, 0), (0, 0)))
        return jnp.concatenate([xb[:, j:j + nc] for j in range(WIN_CHUNKS + 1)], axis=2)

    kb, vb = band(k), band(v)
    chunk_id = jnp.arange(nc)[:, None] - WIN_CHUNKS + jnp.arange(WIN_CHUNKS + 1)[None, :]
    valid = jnp.repeat(chunk_id >= 0, CHUNK, axis=1)
    s = jnp.einsum('bnqhgd,bnkhd->bnhgqk', qb, kb).astype(jnp.float32) * (SWA_HEAD_DIM ** -0.5)
    s = jnp.where(valid[None, :, None, None, None, :], s, NEG)
    p = sink_softmax(s, sink.reshape(SWA_KV_HEADS, SWA_GROUP)).astype(v.dtype)
    o = jnp.einsum('bnhgqk,bnkhd->bnqhgd', p, vb)
    return o.reshape(B, S, SWA_Q)


def swa_step(q, k_all, v_all, sink):
    B, T = q.shape[:2]
    qg = q.reshape(B, T, SWA_KV_HEADS, SWA_GROUP, SWA_HEAD_DIM)
    s = jnp.einsum('bqhgd,bkhd->bhgqk', qg, k_all).astype(jnp.float32) * (SWA_HEAD_DIM ** -0.5)
    p = sink_softmax(s, sink.reshape(SWA_KV_HEADS, SWA_GROUP)).astype(v_all.dtype)
    o = jnp.einsum('bhgqk,bkhd->bqhgd', p, v_all)
    return o.reshape(B, T, SWA_Q)


def ret_log_decay():
    return jnp.log1p(-jnp.exp2(-5.0 - jnp.arange(RET_HEADS, dtype=jnp.float32)))


def retention_chunk(q, k, v, state):
    C = q.shape[2]
    log_g = ret_log_decay()
    n = jnp.arange(C, dtype=jnp.float32)
    diff = n[:, None] - n[None, :]
    decay = jnp.where(diff >= 0, jnp.exp(log_g[:, None, None] * jnp.maximum(diff, 0.0)), 0.0)
    qf, kf, vf, sf = (a.astype(jnp.float32) for a in (q, k, v, state))
    inner = jnp.einsum('bhnd,bhmd->bhnm', qf, kf) * decay
    o = jnp.einsum('bhnm,bhme->bhne', inner, vf)
    o = o + jnp.einsum('bhnd,bhde->bhne', qf, sf) * jnp.exp(log_g[:, None] * (n + 1.0))[:, :, None]
    zeta = jnp.exp(log_g[:, None] * (C - 1.0 - n))
    new_state = (jnp.exp(log_g * C)[:, None, None] * sf
                 + jnp.einsum('bhmd,bhme->bhde', kf * zeta[:, :, None], vf))
    return o, new_state


def retention_chunked(q, k, v):
    B, H, S, _ = q.shape
    nc = S // CHUNK

    def blocks(a):
        return jnp.moveaxis(a.reshape(B, H, nc, CHUNK, a.shape[-1]), 2, 0)

    def step(state, qkv):
        o, state = retention_chunk(qkv[0], qkv[1], qkv[2], state)
        return state, o

    s0 = jnp.zeros((B, H, RET_QK_DIM, RET_V_DIM), jnp.float32)
    s_final, o = lax.scan(step, s0, (blocks(q), blocks(k), blocks(v)))
    return jnp.moveaxis(o, 0, 2).reshape(B, H, S, RET_V_DIM), s_final


def retention_out(o, gate):
    B, H, T, _ = o.shape
    o = o * lax.rsqrt(jnp.mean(o * o, axis=-1, keepdims=True) + EPS)
    o = jnp.transpose(o, (0, 2, 1, 3)).reshape(B, T, RET_V)
    return (jax.nn.silu(gate.astype(jnp.float32)) * o).astype(gate.dtype)


def memory_kv(mem, g_mem, w_mem_kv):
    B, M, _ = mem.shape
    k, v = jnp.split(rmsnorm(mem, g_mem) @ w_mem_kv, 2, axis=-1)
    return (k.reshape(B, M, MEM_HEADS, MEM_HEAD_DIM), v.reshape(B, M, MEM_HEADS, MEM_HEAD_DIM))


def memory_attend(q, mk, mv):
    B, T = q.shape[:2]
    qh = q.reshape(B, T, MEM_HEADS, MEM_HEAD_DIM)
    s = jnp.einsum('bthd,bmhd->bhtm', qh, mk).astype(jnp.float32) * (MEM_HEAD_DIM ** -0.5)
    p = jax.nn.softmax(s, axis=-1).astype(mv.dtype)
    return jnp.einsum('bhtm,bmhd->bthd', p, mv).reshape(B, T, MEM_W)


def conv_ffn(u, conv_buf, w_up, w_conv, b_conv, w_down):
    T = u.shape[1]
    a = u @ w_up
    ext = jnp.concatenate([conv_buf.astype(a.dtype), a], axis=1)
    c = b_conv + sum(ext[:, j:j + T] * w_conv[j] for j in range(CONV_W))
    g, val = jnp.split(c, 2, axis=-1)
    return (jax.nn.silu(g) * val) @ w_down, ext[:, T:]


def run_layer(h, pos, mem_k, mem_v, swa_cache, ret_state, conv_buf,
              g_mix, w_in, b_gate, sink, w_br, w_o, g_ffn, w_up, w_conv, b_conv, w_down):
    B, T, _ = h.shape
    u = rmsnorm(h, g_mix)
    split_pts = np.cumsum(IN_SPLITS)[:-1].tolist()
    qa, ka, va, qr, kr, vr, gr, qm, gl = jnp.split(u @ w_in, split_pts, axis=-1)
    qa = qa.reshape(B, T, SWA_HEADS, SWA_HEAD_DIM)
    ka = ka.reshape(B, T, SWA_KV_HEADS, SWA_HEAD_DIM)
    va = va.reshape(B, T, SWA_KV_HEADS, SWA_HEAD_DIM)
    if swa_cache is None:
        o_swa = swa_banded(qa, ka, va, sink)
        new_k, new_v = ka[:, -WINDOW:], va[:, -WINDOW:]
    else:
        k_all = jnp.concatenate([swa_cache[0].astype(ka.dtype), ka], axis=1)
        v_all = jnp.concatenate([swa_cache[1].astype(va.dtype), va], axis=1)
        o_swa = swa_step(qa, k_all, v_all, sink)
        n_keep = swa_cache[0].shape[1]
        new_k, new_v = k_all[:, -n_keep:], v_all[:, -n_keep:]
    qr = rotary(qr.reshape(B, T, RET_HEADS, RET_QK_DIM), pos)
    kr = rotary(kr.reshape(B, T, RET_HEADS, RET_QK_DIM), pos) * (RET_QK_DIM ** -0.5)
    vr = vr.reshape(B, T, RET_HEADS, RET_V_DIM)
    qt, kt, vt = (jnp.transpose(a, (0, 2, 1, 3)) for a in (qr, kr, vr))
    if ret_state is None:
        o_r, s_new = retention_chunked(qt, kt, vt)
    else:
        o_r, s_new = retention_chunk(qt, kt, vt, ret_state)
    o_ret = retention_out(o_r, gr)
    o_mem = memory_attend(qm, mem_k.astype(h.dtype), mem_v.astype(h.dtype))
    gates = jax.nn.sigmoid((gl + b_gate).astype(jnp.float32)).astype(h.dtype).reshape(B, T, N_BRANCH, D_MODEL)
    merged = (gates[:, :, 0] * (o_swa @ w_br[:SWA_Q])
              + gates[:, :, 1] * (o_ret @ w_br[SWA_Q:SWA_Q + RET_V])
              + gates[:, :, 2] * (o_mem @ w_br[SWA_Q + RET_V:]))
    h = h + merged @ w_o
    f, new_buf = conv_ffn(rmsnorm(h, g_ffn), conv_buf, w_up, w_conv, b_conv, w_down)
    return h + f, new_k, new_v, s_new.astype(h.dtype), new_buf


def setup_inputs(seed: int = 0) -> dict:
    key = jax.random.key(seed)
    ks = jax.random.split(key, 24)
    f32 = jnp.float32

    def nrm(k, shape, scale):
        return jax.random.normal(k, shape, f32) * scale

    swa_len = min(WINDOW, PAST_LEN)
    return {
        "x_prompt": nrm(ks[0], (BATCH, SEQ, D_MODEL), 1.0),
        "x_sample": nrm(ks[1], (DEC_BATCH, DEC_SEQ, D_MODEL), 1.0),
        "mem_prompt": nrm(ks[2], (BATCH, N_MEM, D_MODEL), 1.0),
        "cache_swa_k": nrm(ks[3], (DEPTH, DEC_BATCH, swa_len, SWA_KV_HEADS, SWA_HEAD_DIM), 1.0),
        "cache_swa_v": nrm(ks[4], (DEPTH, DEC_BATCH, swa_len, SWA_KV_HEADS, SWA_HEAD_DIM), 1.0),
        "state_ret": nrm(ks[5], (DEPTH, DEC_BATCH, RET_HEADS, RET_QK_DIM, RET_V_DIM), 0.5),
        "state_ffn_conv": nrm(ks[6], (DEPTH, DEC_BATCH, CONV_W - 1, 2 * D_FF), 1.0),
        "cache_mem_k": nrm(ks[7], (DEPTH, DEC_BATCH, N_MEM, MEM_HEADS, MEM_HEAD_DIM), 1.0),
        "cache_mem_v": nrm(ks[8], (DEPTH, DEC_BATCH, N_MEM, MEM_HEADS, MEM_HEAD_DIM), 1.0),
        "g_mix": 1.0 + nrm(ks[9], (DEPTH, D_MODEL), 0.02),
        "w_in": nrm(ks[10], (DEPTH, D_MODEL, IN_COLS), D_MODEL ** -0.5),
        "b_gate": nrm(ks[11], (DEPTH, N_BRANCH * D_MODEL), 0.02),
        "sink": nrm(ks[12], (DEPTH, SWA_HEADS), 0.5),
        "w_br": nrm(ks[13], (DEPTH, MIX_W, D_MODEL), SWA_Q ** -0.5),
        "w_o": nrm(ks[14], (DEPTH, D_MODEL, D_MODEL), D_MODEL ** -0.5),
        "g_mem": 1.0 + nrm(ks[15], (DEPTH, D_MODEL), 0.02),
        "w_mem_kv": nrm(ks[16], (DEPTH, D_MODEL, 2 * MEM_W), D_MODEL ** -0.5),
        "g_ffn": 1.0 + nrm(ks[17], (DEPTH, D_MODEL), 0.02),
        "w_up": nrm(ks[18], (DEPTH, D_MODEL, 2 * D_FF), D_MODEL ** -0.5),
        "w_conv": nrm(ks[19], (DEPTH, CONV_W, 2 * D_FF), CONV_W ** -0.5),
        "b_conv": nrm(ks[20], (DEPTH, 2 * D_FF), 0.02),
        "w_down": nrm(ks[21], (DEPTH, D_FF, D_MODEL), D_FF ** -0.5),
        "g_final": 1.0 + nrm(ks[22], (D_MODEL,), 0.02),
    }


def reference(x_prompt, x_sample, mem_prompt, cache_swa_k, cache_swa_v, state_ret, state_ffn_conv,
              cache_mem_k, cache_mem_v, g_mix, w_in, b_gate, sink, w_br, w_o, g_mem, w_mem_kv,
              g_ffn, w_up, w_conv, b_conv, w_down, g_final):
    Bp, S, _ = x_prompt.shape
    T = x_sample.shape[1]
    pos_p = jnp.arange(S)
    pos_s = PAST_LEN + jnp.arange(T)
    hp, hs = x_prompt, x_sample
    kp_l, vp_l, sp_l, cp_l, mk_l, mv_l = [], [], [], [], [], []
    ks_l, vs_l, ss_l, cs_l = [], [], [], []
    for l in range(DEPTH):
        w = (g_mix[l], w_in[l], b_gate[l], sink[l], w_br[l], w_o[l],
             g_ffn[l], w_up[l], w_conv[l], b_conv[l], w_down[l])
        mk, mv = memory_kv(mem_prompt, g_mem[l], w_mem_kv[l])
        zero_buf = jnp.zeros((Bp, CONV_W - 1, 2 * D_FF), hp.dtype)
        hp, kp, vp, sp, cp = run_layer(hp, pos_p, mk, mv, None, None, zero_buf, *w)
        hs, ksn, vsn, ssn, csn = run_layer(hs, pos_s, cache_mem_k[l], cache_mem_v[l],
                                            (cache_swa_k[l], cache_swa_v[l]), state_ret[l],
                                            state_ffn_conv[l], *w)
        kp_l.append(kp); vp_l.append(vp); sp_l.append(sp); cp_l.append(cp); mk_l.append(mk); mv_l.append(mv)
        ks_l.append(ksn); vs_l.append(vsn); ss_l.append(ssn); cs_l.append(csn)
    y_prompt = rmsnorm(hp, g_final)
    y_sample = rmsnorm(hs, g_final)
    return (y_prompt, y_sample,
            jnp.stack(kp_l), jnp.stack(vp_l), jnp.stack(sp_l), jnp.stack(cp_l),
            jnp.stack(mk_l), jnp.stack(mv_l),
            jnp.stack(ks_l), jnp.stack(vs_l), jnp.stack(ss_l), jnp.stack(cs_l))
```

```python
import functools

import jax
import jax.numpy as jnp
from jax import lax
from jax.experimental import pallas as pl
from jax.experimental.pallas import tpu as pltpu

F32 = jnp.float32
BF16 = jnp.bfloat16

D_MODEL = 2048
CHUNK = 64
WINDOW = 128
SWA_HEADS = 16
SWA_KV_HEADS = 4
SWA_GROUP = SWA_HEADS // SWA_KV_HEADS
SWA_HEAD_DIM = 64
RET_HEADS = 8
RET_DIM = 128
RET_ROPE_BASE = 10000.0
N_MEM = 256
MEM_HEADS = 4
MEM_HEAD_DIM = 256
D_FF = 5632
N_BRANCH = 3
EPS = 1e-6
NEG = -1e30
PAST_LEN = 1024

SWA_Q = SWA_HEADS * SWA_HEAD_DIM
SWA_KV = SWA_KV_HEADS * SWA_HEAD_DIM
RET_W = RET_HEADS * RET_DIM
MEM_W = MEM_HEADS * MEM_HEAD_DIM
LANE = 128


def _sigmoid(x):
    return 1.0 / (1.0 + jnp.exp(-x))


def _rms(x, g):
    return x * lax.rsqrt(jnp.mean(x * x, axis=-1, keepdims=True) + EPS) * g


def _rmsnorm_kernel(x_ref, g_ref, o_ref):
    o_ref[...] = _rms(x_ref[...], g_ref[...]).astype(o_ref.dtype)


def _rmsnorm(x, g, tm):
    n, d = x.shape
    return pl.pallas_call(
        _rmsnorm_kernel,
        grid=(n // tm,),
        in_specs=[pl.BlockSpec((tm, d), lambda i: (i, 0)),
                  pl.BlockSpec((1, d), lambda i: (0, 0))],
        out_specs=pl.BlockSpec((tm, d), lambda i: (i, 0)),
        out_shape=jax.ShapeDtypeStruct((n, d), BF16),
        name="rmsnorm",
    )(x, g.reshape(1, d))


def _mm_kernel(x_ref, w_ref, *rest, epilogue, tn):
    o_ref = rest[-1]
    acc = jnp.dot(x_ref[...], w_ref[...], preferred_element_type=F32)
    if epilogue == "plain":
        out = acc
    elif epilogue == "scale":
        out = acc * rest[0][...]
    elif epilogue == "silu":
        out = acc * _sigmoid(acc)
    elif epilogue == "sigmoid_bias":
        out = _sigmoid(acc + rest[0][...])
    elif epilogue == "residual":
        out = rest[0][...] + acc
    elif epilogue == "rotary":
        cc, ss = rest[1][...], rest[2][...]
        pieces = []
        for j in range(tn // LANE):
            xh = acc[:, j * LANE:(j + 1) * LANE]
            pieces.append(xh * cc + pltpu.roll(xh, LANE // 2, 1) * ss)
        out = jnp.concatenate(pieces, axis=1) * rest[0][...]
    else:
        raise ValueError(epilogue)
    o_ref[...] = out.astype(o_ref.dtype)


def _mm(x, w, *, epilogue, out_dtype, tm, tn, colvec=None, residual=None, tables=None, name):
    n, k = x.shape
    ncols = w.shape[1]
    in_specs = [pl.BlockSpec((tm, k), lambda i, j: (i, 0)),
                pl.BlockSpec((k, tn), lambda i, j: (0, j))]
    args = [x, w]
    if colvec is not None:
        in_specs.append(pl.BlockSpec((1, tn), lambda i, j: (0, j)))
        args.append(colvec.reshape(1, ncols).astype(F32))
    if residual is not None:
        in_specs.append(pl.BlockSpec((tm, tn), lambda i, j: (i, j)))
        args.append(residual)
    if tables is not None:
        nblk = tables[0].shape[0] // tm
        for t in tables:
            in_specs.append(pl.BlockSpec((tm, LANE), lambda i, j: (i % nblk, 0)))
            args.append(t)
    return pl.pallas_call(
        functools.partial(_mm_kernel, epilogue=epilogue, tn=tn),
        grid=(n // tm, ncols // tn),
        in_specs=in_specs,
        out_specs=pl.BlockSpec((tm, tn), lambda i, j: (i, j)),
        out_shape=jax.ShapeDtypeStruct((n, ncols), out_dtype),
        compiler_params=pltpu.CompilerParams(dimension_semantics=("parallel", "parallel")),
        name=name,
    )(*args)


def _swa_kernel(sink_ref, q_ref, kp_ref, kc_ref, vp_ref, vc_ref, o_ref, *, qt, kv_off):
    i = pl.program_id(1)
    nq = qt // CHUNK
    w = (nq + 2) * CHUNK
    wp = -(-w // LANE) * LANE
    first_ok = jnp.where(i * qt + kv_off >= WINDOW, 0, 2)

    def window(p_ref, c_ref):
        parts = [p_ref[0], c_ref[0]]
        if wp > w:
            parts.append(jnp.zeros((wp - w, SWA_KV), F32))
        return jnp.concatenate(parts, axis=0)

    kwin = window(kp_ref, kc_ref)
    vwin = window(vp_ref, vc_ref)
    lane = lax.broadcasted_iota(jnp.int32, (wp, LANE), 1)
    lane_g = lax.broadcasted_iota(jnp.int32, (wp, SWA_KV), 1) // SWA_HEAD_DIM
    qc = lax.broadcasted_iota(jnp.int32, (qt, wp), 0) // CHUNK
    kc = lax.broadcasted_iota(jnp.int32, (qt, wp), 1) // CHUNK
    band = (kc >= qc) & (kc <= qc + 2) & (kc >= first_ok)
    out_g = lax.broadcasted_iota(jnp.int32, (qt, SWA_KV), 1) // SWA_HEAD_DIM

    def block_diag(win, h):
        col, half = divmod(h, 2)
        x = win[:, col * LANE:(col + 1) * LANE]
        xr = pltpu.roll(x, SWA_HEAD_DIM, 1)
        lo = lane < SWA_HEAD_DIM
        r = jnp.where(lo, x, xr) if half == 0 else jnp.where(lo, xr, x)
        r2 = jnp.concatenate([r, r], axis=1).astype(BF16)
        zero = jnp.zeros_like(r2)
        return jnp.concatenate([jnp.where(lane_g == g, r2, zero) for g in range(SWA_GROUP)], axis=0)

    for h in range(SWA_KV_HEADS):
        bk = block_diag(kwin, h)
        bv = block_diag(vwin, h)
        qh = q_ref[0, :, h * SWA_KV:(h + 1) * SWA_KV]
        s = lax.dot_general(qh, bk, (((1,), (1,)), ((), ())), preferred_element_type=F32)
        ps = []
        inv = jnp.zeros((qt, SWA_KV), F32)
        for g in range(SWA_GROUP):
            sg = jnp.where(band, s[:, g * wp:(g + 1) * wp], NEG)
            sk = sink_ref[h * SWA_GROUP + g]
            m = jnp.maximum(jnp.max(sg, axis=1, keepdims=True), sk)
            p = jnp.exp(sg - m)
            den = jnp.sum(p, axis=1, keepdims=True) + jnp.exp(sk - m)
            ps.append(p.astype(BF16))
            inv = jnp.where(out_g == g, 1.0 / den, inv)
        o = jnp.dot(jnp.concatenate(ps, axis=1), bv, preferred_element_type=F32)
        o_ref[0, :, h * SWA_KV:(h + 1) * SWA_KV] = (o * inv).astype(o_ref.dtype)


def _swa(q, q_col, k, k_col, v, v_col, sink, *, sq, qt, kv_off):
    b = q.shape[0]
    cur_off = kv_off // qt
    prev_off = kv_off // WINDOW - 1
    per = qt // WINDOW if qt >= WINDOW else 0

    def prev_map(kcol):
        return lambda bi, i, s: (bi, jnp.maximum(i * per + prev_off, 0), kcol)

    def cur_map(kcol):
        return lambda bi, i, s: (bi, i + cur_off, kcol)

    grid_spec = pltpu.PrefetchScalarGridSpec(
        num_scalar_prefetch=1,
        grid=(b, sq // qt),
        in_specs=[pl.BlockSpec((1, qt, SWA_Q), lambda bi, i, s: (bi, i, q_col)),
                  pl.BlockSpec((1, WINDOW, SWA_KV), prev_map(k_col)),
                  pl.BlockSpec((1, qt, SWA_KV), cur_map(k_col)),
                  pl.BlockSpec((1, WINDOW, SWA_KV), prev_map(v_col)),
                  pl.BlockSpec((1, qt, SWA_KV), cur_map(v_col))],
        out_specs=pl.BlockSpec((1, qt, SWA_Q), lambda bi, i, s: (bi, i, 0)),
    )
    return pl.pallas_call(
        functools.partial(_swa_kernel, qt=qt, kv_off=kv_off),
        grid_spec=grid_spec,
        out_shape=jax.ShapeDtypeStruct((b, sq, SWA_Q), BF16),
        compiler_params=pltpu.CompilerParams(dimension_semantics=("parallel", "parallel")),
        name="swa",
    )(sink.astype(F32), q, k, k, v, v)


def _ret_kernel(lg_ref, q_ref, k_ref, v_ref, g_ref, s0_ref, o_ref, st_ref, dec_ref, *, c):
    bi = pl.program_id(0)
    ci = pl.program_id(1)

    @pl.when((bi == 0) & (ci == 0))
    def _():
        n = lax.broadcasted_iota(jnp.int32, (c, c), 0)
        m = lax.broadcasted_iota(jnp.int32, (c, c), 1)
        diff = (n - m).astype(F32)
        for h in range(RET_HEADS):
            dec_ref[h] = jnp.where(diff >= 0, jnp.exp(lg_ref[h] * jnp.maximum(diff, 0.0)), 0.0)

    @pl.when(ci == 0)
    def _():
        st_ref[...] = s0_ref[...]

    nvec = lax.broadcasted_iota(jnp.int32, (c, 1), 0).astype(F32)
    for h in range(RET_HEADS):
        lg = lg_ref[h]
        sl = slice(h * RET_DIM, (h + 1) * RET_DIM)
        qh = q_ref[0, :, sl]
        kh = k_ref[0, :, sl]
        vh = v_ref[0, :, sl]
        state = st_ref[0, h]
        inner = lax.dot_general(qh, kh, (((1,), (1,)), ((), ())), preferred_element_type=F32) * dec_ref[h]
        o = jnp.dot(inner.astype(BF16), vh, preferred_element_type=F32)
        cross = jnp.dot(qh, state.astype(BF16), preferred_element_type=F32)
        o = o + cross * jnp.exp(lg * (nvec + 1.0))
        zeta = jnp.exp(lg * (float(c) - 1.0 - nvec))
        kz = (kh.astype(F32) * zeta).astype(BF16)
        upd = lax.dot_general(kz, vh, (((0,), (0,)), ((), ())), preferred_element_type=F32)
        decay_c = jnp.exp(lg * jnp.full((1, RET_DIM), float(c), F32))
        st_ref[0, h] = decay_c * state + upd
        on = o * lax.rsqrt(jnp.mean(o * o, axis=-1, keepdims=True) + EPS)
        o_ref[0, :, sl] = (g_ref[0, :, sl].astype(F32) * on).astype(o_ref.dtype)


def _retention(qk, q_col, k_col, v, v_col, gate, state0, log_g, *, s, c):
    b = qk.shape[0]

    def seq(col):
        return pl.BlockSpec((1, c, RET_W), lambda bi, ci, lg: (bi, ci, col))

    st_spec = pl.BlockSpec((1, RET_HEADS, RET_DIM, RET_DIM), lambda bi, ci, lg: (bi, 0, 0, 0))
    grid_spec = pltpu.PrefetchScalarGridSpec(
        num_scalar_prefetch=1,
        grid=(b, s // c),
        in_specs=[seq(q_col), seq(k_col), seq(v_col), seq(0), st_spec],
        out_specs=[seq(0), st_spec],
        scratch_shapes=[pltpu.VMEM((RET_HEADS, c, c), F32)],
    )
    return pl.pallas_call(
        functools.partial(_ret_kernel, c=c),
        grid_spec=grid_spec,
        out_shape=[jax.ShapeDtypeStruct((b, s, RET_W), BF16),
                   jax.ShapeDtypeStruct((b, RET_HEADS, RET_DIM, RET_DIM), F32)],
        compiler_params=pltpu.CompilerParams(dimension_semantics=("arbitrary", "arbitrary")),
        name="retention",
    )(log_g, qk, qk, v, gate, state0)


def _mem_kernel(q_ref, k_ref, v_ref, o_ref):
    for h in range(MEM_HEADS):
        sl = slice(h * MEM_HEAD_DIM, (h + 1) * MEM_HEAD_DIM)
        qh = q_ref[0, :, sl]
        kh = k_ref[0, :, sl].astype(BF16)
        vh = v_ref[0, :, sl].astype(BF16)
        s = lax.dot_general(qh, kh, (((1,), (1,)), ((), ())), preferred_element_type=F32)
        m = jnp.max(s, axis=1, keepdims=True)
        p = jnp.exp(s - m)
        den = jnp.sum(p, axis=1, keepdims=True)
        o = jnp.dot(p.astype(BF16), vh, preferred_element_type=F32)
        o_ref[0, :, sl] = (o * (1.0 / den)).astype(o_ref.dtype)


def _mem_attend(q, q_col, mk, mv, *, s, tq):
    b = q.shape[0]
    kv_spec = pl.BlockSpec((1, N_MEM, MEM_W), lambda bi, i: (bi, 0, 0))
    return pl.pallas_call(
        _mem_kernel,
        grid=(b, s // tq),
        in_specs=[pl.BlockSpec((1, tq, MEM_W), lambda bi, i: (bi, i, q_col)), kv_spec, kv_spec],
        out_specs=pl.BlockSpec((1, tq, MEM_W), lambda bi, i: (bi, i, 0)),
        out_shape=jax.ShapeDtypeStruct((b, s, MEM_W), BF16),
        compiler_params=pltpu.CompilerParams(dimension_semantics=("parallel", "parallel")),
        name="mem_attend",
    )(q, mk, mv)


def _merge_kernel(oa_ref, ob_ref, oc_ref, wa_ref, wb_ref, wc_ref, ga_ref, gb_ref, gc_ref, o_ref):
    acc = ga_ref[...].astype(F32) * jnp.dot(oa_ref[...], wa_ref[...], preferred_element_type=F32)
    acc = acc + gb_ref[...].astype(F32) * jnp.dot(ob_ref[...], wb_ref[...], preferred_element_type=F32)
    acc = acc + gc_ref[...].astype(F32) * jnp.dot(oc_ref[...], wc_ref[...], preferred_element_type=F32)
    o_ref[...] = acc.astype(o_ref.dtype)


def _merge(o_swa, o_ret, o_mem, w_br, gates, *, tm, tn):
    n = o_swa.shape[0]
    kb = o_swa.shape[1]
    nj = D_MODEL // tn
    o_spec = pl.BlockSpec((tm, kb), lambda i, j: (i, 0))

    def w_spec(br):
        return pl.BlockSpec((kb, tn), lambda i, j: (br, j))

    def g_spec(br):
        return pl.BlockSpec((tm, tn), lambda i, j: (i, br * nj + j))

    return pl.pallas_call(
        _merge_kernel,
        grid=(n // tm, nj),
        in_specs=[o_spec, o_spec, o_spec, w_spec(0), w_spec(1), w_spec(2), g_spec(0), g_spec(1), g_spec(2)],
        out_specs=pl.BlockSpec((tm, tn), lambda i, j: (i, j)),
        out_shape=jax.ShapeDtypeStruct((n, D_MODEL), BF16),
        compiler_params=pltpu.CompilerParams(dimension_semantics=("parallel", "parallel")),
        name="merge",
    )(o_swa, o_ret, o_mem, w_br, w_br, w_br, gates, gates, gates)


def _ffn_kernel(h_ref, gf_ref, wug_ref, wuv_ref, wcg_ref, wcv_ref, bg_ref, bv_ref, wd_ref,
                cbg_ref, cbv_ref, gfin_ref, y_ref, cog_ref, cov_ref,
                u_sc, acc_sc, cg_sc, cv_sc, *, tm, tf, seg, nseg):
    i = pl.program_id(1)
    f = pl.program_id(2)
    nf = pl.num_programs(2)

    @pl.when(f == 0)
    def _():
        u_sc[...] = _rms(h_ref[0], gf_ref[...]).astype(BF16)

    u = u_sc[...]
    row = lax.broadcasted_iota(jnp.int32, (tm, tf), 0)
    rowmod = row % seg if nseg > 1 else row

    def conv_branch(wu_ref, wc_ref, b_ref, cb_ref, carry_sc, co_ref):
        a = jnp.dot(u, wu_ref[...], preferred_element_type=F32)
        if nseg == 1:
            @pl.when(i == 0)
            def _():
                carry_sc[f, 6:8, :] = cb_ref[0]

            halo = carry_sc[f, 6:8, :]
            fix0 = jnp.broadcast_to(halo[0:1], (tm, tf))
            fix1 = jnp.broadcast_to(halo[1:2], (tm, tf))
            carry_sc[f] = a[tm - 8:tm]
            co_ref[0, 0] = a[tm - 8:tm]
        else:
            fix0 = jnp.concatenate([jnp.broadcast_to(cb_ref[s, 0:1, :], (seg, tf)) for s in range(nseg)], axis=0)
            fix1 = jnp.concatenate([jnp.broadcast_to(cb_ref[s, 1:2, :], (seg, tf)) for s in range(nseg)], axis=0)
            for s in range(nseg):
                co_ref[0, s] = a[(s + 1) * seg - 8:(s + 1) * seg]
        prev1 = jnp.where(rowmod == 0, fix1, pltpu.roll(a, 1, 0))
        prev2 = jnp.where(rowmod == 0, fix0, jnp.where(rowmod == 1, fix1, pltpu.roll(a, 2, 0)))
        wc = wc_ref[...]
        return b_ref[...] + (wc[0:1] * prev2 + wc[1:2] * prev1 + wc[2:3] * a)

    cg = conv_branch(wug_ref, wcg_ref, bg_ref, cbg_ref, cg_sc, cog_ref)
    cv = conv_branch(wuv_ref, wcv_ref, bv_ref, cbv_ref, cv_sc, cov_ref)
    act = (cg * _sigmoid(cg) * cv).astype(BF16)
    contrib = jnp.dot(act, wd_ref[...], preferred_element_type=F32)

    @pl.when(f == 0)
    def _():
        acc_sc[...] = contrib

    @pl.when(f > 0)
    def _():
        acc_sc[...] += contrib

    @pl.when(f == nf - 1)
    def _():
        y_ref[0] = _rms(h_ref[0] + acc_sc[...], gfin_ref[...])


def _ffn(h, conv_buf, g_ffn, w_up, w_conv, b_conv, w_down, g_final, *, tm, tf, seg, nseg):
    bt, rows, d = h.shape
    nf = D_FF // tf
    nt = rows // tm
    col = lambda off: (lambda b, i, f: (0, f + off))
    cb = lambda off: (lambda b, i, f: (b, 0, f + off))
    in_specs = [
        pl.BlockSpec((1, tm, d), lambda b, i, f: (b, i, 0)),
        pl.BlockSpec((1, d), lambda b, i, f: (0, 0)),
        pl.BlockSpec((d, tf), col(0)),
        pl.BlockSpec((d, tf), col(nf)),
        pl.BlockSpec((3, tf), col(0)),
        pl.BlockSpec((3, tf), col(nf)),
        pl.BlockSpec((1, tf), col(0)),
        pl.BlockSpec((1, tf), col(nf)),
        pl.BlockSpec((tf, d), lambda b, i, f: (f, 0)),
        pl.BlockSpec((nseg, 2, tf), cb(0)),
        pl.BlockSpec((nseg, 2, tf), cb(nf)),
        pl.BlockSpec((1, d), lambda b, i, f: (0, 0)),
    ]
    out_specs = [
        pl.BlockSpec((1, tm, d), lambda b, i, f: (b, i, 0)),
        pl.BlockSpec((1, nseg, 8, tf), lambda b, i, f: (b * nt + i, 0, 0, f)),
        pl.BlockSpec((1, nseg, 8, tf), lambda b, i, f: (b * nt + i, 0, 0, f)),
    ]
    return pl.pallas_call(
        functools.partial(_ffn_kernel, tm=tm, tf=tf, seg=seg, nseg=nseg),
        grid=(bt, rows // tm, nf),
        in_specs=in_specs,
        out_specs=out_specs,
        out_shape=[jax.ShapeDtypeStruct((bt, rows, d), F32),
                   jax.ShapeDtypeStruct((bt * nt, nseg, 8, D_FF), F32),
                   jax.ShapeDtypeStruct((bt * nt, nseg, 8, D_FF), F32)],
        scratch_shapes=[pltpu.VMEM((tm, d), BF16), pltpu.VMEM((tm, d), F32),
                        pltpu.VMEM((nf, 8, tf), F32), pltpu.VMEM((nf, 8, tf), F32)],
        compiler_params=pltpu.CompilerParams(dimension_semantics=("arbitrary", "arbitrary", "arbitrary")),
        name="conv_ffn",
    )(h, g_ffn.reshape(1, d), w_up, w_up, w_conv, w_conv, b_conv.reshape(1, -1), b_conv.reshape(1, -1),
      w_down, conv_buf, conv_buf, g_final.reshape(1, d))


def _rope_tables(pos0, s):
    half = RET_DIM // 2
    inv_freq = 1.0 / (RET_ROPE_BASE ** jnp.linspace(0.0, 1.0, half, dtype=F32))
    ang = (pos0 + jnp.arange(s)).astype(F32)[:, None] * inv_freq[None, :]
    cos, sin = jnp.cos(ang), jnp.sin(ang)
    return jnp.concatenate([cos, cos], axis=1), jnp.concatenate([-sin, sin], axis=1)


def _prep_weights(g_mix, w_in, b_gate, sink, w_br, w_o, g_ffn, w_up, w_conv, b_conv, w_down):
    o = 0
    cuts = {}
    for name, width in (("qa", SWA_Q), ("ka", SWA_KV), ("va", SWA_KV), ("qr", RET_W), ("kr", RET_W),
                        ("vr", RET_W), ("gr", RET_W), ("qm", MEM_W), ("gl", N_BRANCH * D_MODEL)):
        cuts[name] = (o, o + width)
        o += width
    cols = lambda *names: jnp.concatenate([w_in[:, cuts[n][0]:cuts[n][1]] for n in names], axis=1).astype(BF16)
    ones = lambda n, v: jnp.full((n,), v, F32)
    return dict(
        g_mix=g_mix, b_gate=b_gate, sink=sink, g_ffn=g_ffn, w_conv=w_conv, b_conv=b_conv,
        w_plain=cols("qa", "vr", "qm"),
        s_plain=jnp.concatenate([ones(SWA_Q, SWA_HEAD_DIM ** -0.5), ones(RET_W, 1.0),
                                 ones(MEM_W, MEM_HEAD_DIM ** -0.5)]),
        w_kv=cols("ka", "va"),
        w_rot=cols("qr", "kr"),
        s_rot=jnp.concatenate([ones(RET_W, 1.0), ones(RET_W, RET_DIM ** -0.5)]),
        w_g=cols("gr"),
        w_gate=cols("gl"),
        w_br=w_br.astype(BF16), w_o=w_o.astype(BF16), w_up=w_up.astype(BF16), w_down=w_down.astype(BF16),
    )


def _run_group(x, pos0, mem_k, mem_v, swa_cache, ret_state, conv_buf, wts, g_final, log_g):
    b, s, d = x.shape
    n = b * s
    tm = min(1024, n)
    x2 = x.reshape(n, d)
    u = _rmsnorm(x2, wts["g_mix"], min(512, n))

    plain = _mm(u, wts["w_plain"], epilogue="scale", out_dtype=BF16, tm=tm, tn=1024,
                colvec=wts["s_plain"], name="proj_plain")
    kv = _mm(u, wts["w_kv"], epilogue="plain", out_dtype=F32, tm=tm, tn=2 * SWA_KV, name="proj_kv")
    cc, ss = _rope_tables(pos0, s)
    if s < tm:
        cc, ss = jnp.tile(cc, (tm // s, 1)), jnp.tile(ss, (tm // s, 1))
    rot = _mm(u, wts["w_rot"], epilogue="rotary", out_dtype=BF16, tm=tm, tn=1024,
              colvec=wts["s_rot"], tables=(cc, ss), name="proj_rot")
    sgate = _mm(u, wts["w_g"], epilogue="silu", out_dtype=BF16, tm=tm, tn=1024, name="proj_silu")
    gates = _mm(u, wts["w_gate"], epilogue="sigmoid_bias", out_dtype=BF16, tm=tm, tn=1024,
                colvec=wts["b_gate"], name="proj_gate")

    plain3 = plain.reshape(b, s, -1)
    kv3 = kv.reshape(b, s, 2 * SWA_KV)
    if swa_cache is None:
        o_swa = _swa(plain3, 0, kv3, 0, kv3, 1, wts["sink"], sq=s, qt=min(256, s), kv_off=0)
        new_k = kv3[:, s - WINDOW:, :SWA_KV]
        new_v = kv3[:, s - WINDOW:, SWA_KV:]
    else:
        ck = swa_cache[0].reshape(b, -1, SWA_KV)
        cv = swa_cache[1].reshape(b, -1, SWA_KV)
        n_keep = ck.shape[1]
        k_all = jnp.concatenate([ck, kv3[:, :, :SWA_KV]], axis=1)
        v_all = jnp.concatenate([cv, kv3[:, :, SWA_KV:]], axis=1)
        o_swa = _swa(plain3, 0, k_all, 0, v_all, 0, wts["sink"], sq=s, qt=s, kv_off=n_keep)
        new_k = k_all[:, -n_keep:]
        new_v = v_all[:, -n_keep:]
    state0 = jnp.zeros((b, RET_HEADS, RET_DIM, RET_DIM), F32) if ret_state is None else ret_state
    o_ret, s_new = _retention(rot.reshape(b, s, -1), 0, 1, plain3, 1, sgate.reshape(b, s, -1), state0, log_g,
                              s=s, c=min(256, s))
    o_mem = _mem_attend(plain3, 2, mem_k.reshape(b, N_MEM, MEM_W), mem_v.reshape(b, N_MEM, MEM_W),
                        s=s, tq=min(512, s))
    merged = _merge(o_swa.reshape(n, -1), o_ret.reshape(n, -1), o_mem.reshape(n, -1), wts["w_br"], gates,
                    tm=tm, tn=512)
    h1 = _mm(merged, wts["w_o"], epilogue="residual", out_dtype=F32, tm=tm, tn=1024, residual=x2, name="proj_out")

    if s >= 512:
        y, cog, cov = _ffn(h1.reshape(b, s, d), conv_buf, wts["g_ffn"], wts["w_up"], wts["w_conv"], wts["b_conv"],
                           wts["w_down"], g_final, tm=512, tf=512, seg=s, nseg=1)
    else:
        y, cog, cov = _ffn(h1.reshape(1, n, d), conv_buf, wts["g_ffn"], wts["w_up"], wts["w_conv"], wts["b_conv"],
                           wts["w_down"], g_final, tm=n, tf=512, seg=s, nseg=b)
    last2 = lambda t: t.reshape(-1, t.shape[0] * t.shape[1] // b, 8, D_FF)[:, -1, 6:8]
    new_buf = jnp.concatenate([last2(cog), last2(cov)], axis=-1)
    return (y.reshape(b, s, d), new_k.reshape(b, -1, SWA_KV_HEADS, SWA_HEAD_DIM),
            new_v.reshape(b, -1, SWA_KV_HEADS, SWA_HEAD_DIM), s_new, new_buf)


def _memory_kv(mem, g_mem, w_mem_kv):
    b, m, d = mem.shape
    u = _rmsnorm(mem.reshape(b * m, d), g_mem, min(512, b * m))
    kv = _mm(u, w_mem_kv.astype(BF16), epilogue="plain", out_dtype=F32, tm=min(1024, b * m), tn=1024, name="mem_kv")
    return (kv[:, :MEM_W].reshape(b, m, MEM_HEADS, MEM_HEAD_DIM), kv[:, MEM_W:].reshape(b, m, MEM_HEADS, MEM_HEAD_DIM))


def kernel(x_prompt, x_sample, mem_prompt, cache_swa_k, cache_swa_v, state_ret, state_ffn_conv, cache_mem_k, cache_mem_v, g_mix, w_in, b_gate, sink, w_br, w_o, g_mem, w_mem_kv, g_ffn, w_up, w_conv, b_conv, w_down, g_final):
    bp = x_prompt.shape[0]
    depth = w_in.shape[0]
    log_g = jnp.log1p(-jnp.exp2(-5.0 - jnp.arange(RET_HEADS, dtype=F32)))
    hp, hs = x_prompt, x_sample
    outs_p = [[] for _ in range(6)]
    outs_s = [[] for _ in range(4)]
    for l in range(depth):
        wts = _prep_weights(g_mix[l], w_in[l], b_gate[l], sink[l], w_br[l], w_o[l],
                            g_ffn[l], w_up[l], w_conv[l], b_conv[l], w_down[l])
        mk, mv = _memory_kv(mem_prompt, g_mem[l], w_mem_kv[l])
        zero_buf = jnp.zeros((bp, 2, 2 * D_FF), F32)
        assert depth == 1
        hp, kp, vp, sp, cp = _run_group(hp, 0, mk, mv, None, None, zero_buf, wts, g_final, log_g)
        hs, ksn, vsn, ssn, csn = _run_group(hs, PAST_LEN, cache_mem_k[l], cache_mem_v[l],
                                            (cache_swa_k[l], cache_swa_v[l]), state_ret[l],
                                            state_ffn_conv[l], wts, g_final, log_g)
        for lst, val in zip(outs_p, (kp, vp, sp, cp, mk, mv)):
            lst.append(val)
        for lst, val in zip(outs_s, (ksn, vsn, ssn, csn)):
            lst.append(val)
    return (hp, hs, *[jnp.stack(v) for v in outs_p], *[jnp.stack(v) for v in outs_s])
```

```python
import functools

import jax
import jax.numpy as jnp
from jax import lax
from jax.experimental import pallas as pl
from jax.experimental.pallas import tpu as pltpu

F32 = jnp.float32
BF16 = jnp.bfloat16

D_MODEL = 2048
CHUNK = 64
WINDOW = 128
SWA_HEADS = 16
SWA_KV_HEADS = 4
SWA_GROUP = SWA_HEADS // SWA_KV_HEADS
SWA_HEAD_DIM = 64
RET_HEADS = 8
RET_DIM = 128
RET_ROPE_BASE = 10000.0
N_MEM = 256
MEM_HEADS = 4
MEM_HEAD_DIM = 256
D_FF = 5632
N_BRANCH = 3
EPS = 1e-6
NEG = -1e30
PAST_LEN = 1024

SWA_Q = SWA_HEADS * SWA_HEAD_DIM
SWA_KV = SWA_KV_HEADS * SWA_HEAD_DIM
RET_W = RET_HEADS * RET_DIM
MEM_W = MEM_HEADS * MEM_HEAD_DIM
LANE = 128
CONV_ROWS = 32

def _sigmoid(x):
    return 1.0 / (1.0 + jnp.exp(-x))


def _rms(x, g):
    return x * lax.rsqrt(jnp.mean(x * x, axis=-1, keepdims=True) + EPS) * g


def _rmsnorm_kernel(x_ref, g_ref, o_ref):
    o_ref[...] = _rms(x_ref[...], g_ref[...]).astype(o_ref.dtype)


def _rmsnorm(x, g, tm):
    n, d = x.shape
    return pl.pallas_call(
        _rmsnorm_kernel,
        grid=(n // tm,),
        in_specs=[pl.BlockSpec((tm, d), lambda i: (i, 0)),
                  pl.BlockSpec((1, d), lambda i: (0, 0))],
        out_specs=pl.BlockSpec((tm, d), lambda i: (i, 0)),
        out_shape=jax.ShapeDtypeStruct((n, d), BF16),
        name="rmsnorm",
    )(x, g.reshape(1, d))


def _mm_kernel(x_ref, w_ref, *rest, epilogue, tn):
    o_ref = rest[-1]
    acc = jnp.dot(x_ref[...], w_ref[...], preferred_element_type=F32)
    if epilogue == "plain":
        out = acc
    elif epilogue == "scale":
        out = acc * rest[0][...]
    elif epilogue == "silu":
        out = acc * _sigmoid(acc)
    elif epilogue == "sigmoid_bias":
        out = _sigmoid(acc + rest[0][...])
    elif epilogue == "residual":
        out = rest[0][...] + acc
    elif epilogue == "rotary":
        cc, ss = rest[1][...], rest[2][...]
        pieces = []
        for j in range(tn // LANE):
            xh = acc[:, j * LANE:(j + 1) * LANE]
            pieces.append(xh * cc + pltpu.roll(xh, LANE // 2, 1) * ss)
        out = jnp.concatenate(pieces, axis=1) * rest[0][...]
    else:
        raise ValueError(epilogue)
    o_ref[...] = out.astype(o_ref.dtype)


def _mm(x, w, *, epilogue, out_dtype, tm, tn, colvec=None, residual=None, tables=None, name):
    n, k = x.shape
    ncols = w.shape[1]
    in_specs = [pl.BlockSpec((tm, k), lambda i, j: (i, 0)),
                pl.BlockSpec((k, tn), lambda i, j: (0, j))]
    args = [x, w]
    if colvec is not None:
        in_specs.append(pl.BlockSpec((1, tn), lambda i, j: (0, j)))
        args.append(colvec.reshape(1, ncols).astype(F32))
    if residual is not None:
        in_specs.append(pl.BlockSpec((tm, tn), lambda i, j: (i, j)))
        args.append(residual)
    if tables is not None:
        nblk = tables[0].shape[0] // tm
        for t in tables:
            in_specs.append(pl.BlockSpec((tm, LANE), lambda i, j: (i % nblk, 0)))
            args.append(t)
    return pl.pallas_call(
        functools.partial(_mm_kernel, epilogue=epilogue, tn=tn),
        grid=(n // tm, ncols // tn),
        in_specs=in_specs,
        out_specs=pl.BlockSpec((tm, tn), lambda i, j: (i, j)),
        out_shape=jax.ShapeDtypeStruct((n, ncols), out_dtype),
        compiler_params=pltpu.CompilerParams(dimension_semantics=("parallel", "parallel")),
        name=name,
    )(*args)


def _swa_kernel(sink_ref, q_ref, kp_ref, kc_ref, vp_ref, vc_ref, o_ref, *, qt, kv_off):
    i = pl.program_id(1)
    nq = qt // CHUNK
    w = (nq + 2) * CHUNK
    wp = -(-w // LANE) * LANE
    first_ok = jnp.where(i * qt + kv_off >= WINDOW, 0, 2)

    def window(p_ref, c_ref):
        parts = [p_ref[0], c_ref[0]]
        if wp > w:
            parts.append(jnp.zeros((wp - w, SWA_KV), F32))
        return jnp.concatenate(parts, axis=0)

    kwin = window(kp_ref, kc_ref)
    vwin = window(vp_ref, vc_ref)
    lane = lax.broadcasted_iota(jnp.int32, (wp, LANE), 1)
    lane_g = lax.broadcasted_iota(jnp.int32, (wp, SWA_KV), 1) // SWA_HEAD_DIM
    qc = lax.broadcasted_iota(jnp.int32, (qt, wp), 0) // CHUNK
    kc = lax.broadcasted_iota(jnp.int32, (qt, wp), 1) // CHUNK
    band = (kc >= qc) & (kc <= qc + 2) & (kc >= first_ok)
    out_g = lax.broadcasted_iota(jnp.int32, (qt, SWA_KV), 1) // SWA_HEAD_DIM

    def block_diag(win, h):
        col, half = divmod(h, 2)
        x = win[:, col * LANE:(col + 1) * LANE]
        xr = pltpu.roll(x, SWA_HEAD_DIM, 1)
        lo = lane < SWA_HEAD_DIM
        r = jnp.where(lo, x, xr) if half == 0 else jnp.where(lo, xr, x)
        r2 = jnp.concatenate([r, r], axis=1).astype(BF16)
        zero = jnp.zeros_like(r2)
        return jnp.concatenate([jnp.where(lane_g == g, r2, zero) for g in range(SWA_GROUP)], axis=0)

    for h in range(SWA_KV_HEADS):
        bk = block_diag(kwin, h)
        bv = block_diag(vwin, h)
        qh = q_ref[0, :, h * SWA_KV:(h + 1) * SWA_KV]
        s = lax.dot_general(qh, bk, (((1,), (1,)), ((), ())), preferred_element_type=F32)
        ps = []
        inv = jnp.zeros((qt, SWA_KV), F32)
        for g in range(SWA_GROUP):
            sg = jnp.where(band, s[:, g * wp:(g + 1) * wp], NEG)
            sk = sink_ref[h * SWA_GROUP + g]
            m = jnp.maximum(jnp.max(sg, axis=1, keepdims=True), sk)
            p = jnp.exp(sg - m)
            den = jnp.sum(p, axis=1, keepdims=True) + jnp.exp(sk - m)
            ps.append(p.astype(BF16))
            inv = jnp.where(out_g == g, 1.0 / den, inv)
        o = jnp.dot(jnp.concatenate(ps, axis=1), bv, preferred_element_type=F32)
        o_ref[0, :, h * SWA_KV:(h + 1) * SWA_KV] = (o * inv).astype(o_ref.dtype)


def _swa(q, q_col, k, k_col, v, v_col, sink, *, sq, qt, kv_off):
    b = q.shape[0]
    cur_off = kv_off // qt
    prev_off = kv_off // WINDOW - 1
    per = qt // WINDOW if qt >= WINDOW else 0

    def prev_map(kcol):
        return lambda bi, i, s: (bi, jnp.maximum(i * per + prev_off, 0), kcol)

    def cur_map(kcol):
        return lambda bi, i, s: (bi, i + cur_off, kcol)

    grid_spec = pltpu.PrefetchScalarGridSpec(
        num_scalar_prefetch=1,
        grid=(b, sq // qt),
        in_specs=[pl.BlockSpec((1, qt, SWA_Q), lambda bi, i, s: (bi, i, q_col)),
                  pl.BlockSpec((1, WINDOW, SWA_KV), prev_map(k_col)),
                  pl.BlockSpec((1, qt, SWA_KV), cur_map(k_col)),
                  pl.BlockSpec((1, WINDOW, SWA_KV), prev_map(v_col)),
                  pl.BlockSpec((1, qt, SWA_KV), cur_map(v_col))],
        out_specs=pl.BlockSpec((1, qt, SWA_Q), lambda bi, i, s: (bi, i, 0)),
    )
    return pl.pallas_call(
        functools.partial(_swa_kernel, qt=qt, kv_off=kv_off),
        grid_spec=grid_spec,
        out_shape=jax.ShapeDtypeStruct((b, sq, SWA_Q), BF16),
        compiler_params=pltpu.CompilerParams(dimension_semantics=("parallel", "parallel")),
        name="swa",
    )(sink.astype(F32), q, k, k, v, v)


def _ret_kernel(lg_ref, q_ref, k_ref, v_ref, g_ref, s0_ref, o_ref, st_ref, dec_ref, *, c):
    bi = pl.program_id(0)
    ci = pl.program_id(1)

    @pl.when((bi == 0) & (ci == 0))
    def _():
        n = lax.broadcasted_iota(jnp.int32, (c, c), 0)
        m = lax.broadcasted_iota(jnp.int32, (c, c), 1)
        diff = (n - m).astype(F32)
        for h in range(RET_HEADS):
            dec_ref[h] = jnp.where(diff >= 0, jnp.exp(lg_ref[h] * jnp.maximum(diff, 0.0)), 0.0)

    @pl.when(ci == 0)
    def _():
        st_ref[...] = s0_ref[...]

    nvec = lax.broadcasted_iota(jnp.int32, (c, 1), 0).astype(F32)
    for h in range(RET_HEADS):
        lg = lg_ref[h]
        sl = slice(h * RET_DIM, (h + 1) * RET_DIM)
        qh = q_ref[0, :, sl]
        kh = k_ref[0, :, sl]
        vh = v_ref[0, :, sl]
        state = st_ref[0, h]
        inner = lax.dot_general(qh, kh, (((1,), (1,)), ((), ())), preferred_element_type=F32) * dec_ref[h]
        o = jnp.dot(inner.astype(BF16), vh, preferred_element_type=F32)
        cross = jnp.dot(qh, state.astype(BF16), preferred_element_type=F32)
        o = o + cross * jnp.exp(lg * (nvec + 1.0))
        zeta = jnp.exp(lg * (float(c) - 1.0 - nvec))
        kz = (kh.astype(F32) * zeta).astype(BF16)
        upd = lax.dot_general(kz, vh, (((0,), (0,)), ((), ())), preferred_element_type=F32)
        decay_c = jnp.exp(lg * jnp.full((1, RET_DIM), float(c), F32))
        st_ref[0, h] = decay_c * state + upd
        on = o * lax.rsqrt(jnp.mean(o * o, axis=-1, keepdims=True) + EPS)
        o_ref[0, :, sl] = (g_ref[0, :, sl].astype(F32) * on).astype(o_ref.dtype)


def _retention(qk, q_col, k_col, v, v_col, gate, state0, log_g, *, s, c):
    b = qk.shape[0]

    def seq(col):
        return pl.BlockSpec((1, c, RET_W), lambda bi, ci, lg: (bi, ci, col))

    st_spec = pl.BlockSpec((1, RET_HEADS, RET_DIM, RET_DIM), lambda bi, ci, lg: (bi, 0, 0, 0))
    grid_spec = pltpu.PrefetchScalarGridSpec(
        num_scalar_prefetch=1,
        grid=(b, s // c),
        in_specs=[seq(q_col), seq(k_col), seq(v_col), seq(0), st_spec],
        out_specs=[seq(0), st_spec],
        scratch_shapes=[pltpu.VMEM((RET_HEADS, c, c), F32)],
    )
    return pl.pallas_call(
        functools.partial(_ret_kernel, c=c),
        grid_spec=grid_spec,
        out_shape=[jax.ShapeDtypeStruct((b, s, RET_W), BF16),
                   jax.ShapeDtypeStruct((b, RET_HEADS, RET_DIM, RET_DIM), F32)],
        compiler_params=pltpu.CompilerParams(dimension_semantics=("arbitrary", "arbitrary")),
        name="retention",
    )(log_g, qk, qk, v, gate, state0)


def _mem_kernel(q_ref, k_ref, v_ref, o_ref):
    for h in range(MEM_HEADS):
        sl = slice(h * MEM_HEAD_DIM, (h + 1) * MEM_HEAD_DIM)
        qh = q_ref[0, :, sl]
        kh = k_ref[0, :, sl].astype(BF16)
        vh = v_ref[0, :, sl].astype(BF16)
        s = lax.dot_general(qh, kh, (((1,), (1,)), ((), ())), preferred_element_type=F32)
        m = jnp.max(s, axis=1, keepdims=True)
        p = jnp.exp(s - m)
        den = jnp.sum(p, axis=1, keepdims=True)
        o = jnp.dot(p.astype(BF16), vh, preferred_element_type=F32)
        o_ref[0, :, sl] = (o * (1.0 / den)).astype(o_ref.dtype)


def _mem_attend(q, q_col, mk, mv, *, s, tq):
    b = q.shape[0]
    kv_spec = pl.BlockSpec((1, N_MEM, MEM_W), lambda bi, i: (bi, 0, 0))
    return pl.pallas_call(
        _mem_kernel,
        grid=(b, s // tq),
        in_specs=[pl.BlockSpec((1, tq, MEM_W), lambda bi, i: (bi, i, q_col)), kv_spec, kv_spec],
        out_specs=pl.BlockSpec((1, tq, MEM_W), lambda bi, i: (bi, i, 0)),
        out_shape=jax.ShapeDtypeStruct((b, s, MEM_W), BF16),
        compiler_params=pltpu.CompilerParams(dimension_semantics=("parallel", "parallel")),
        name="mem_attend",
    )(q, mk, mv)


def _merge_kernel(oa_ref, ob_ref, oc_ref, wa_ref, wb_ref, wc_ref, ga_ref, gb_ref, gc_ref, o_ref):
    acc = ga_ref[...].astype(F32) * jnp.dot(oa_ref[...], wa_ref[...], preferred_element_type=F32)
    acc = acc + gb_ref[...].astype(F32) * jnp.dot(ob_ref[...], wb_ref[...], preferred_element_type=F32)
    acc = acc + gc_ref[...].astype(F32) * jnp.dot(oc_ref[...], wc_ref[...], preferred_element_type=F32)
    o_ref[...] = acc.astype(o_ref.dtype)


def _merge(o_swa, o_ret, o_mem, w_br, gates, *, tm, tn):
    n = o_swa.shape[0]
    kb = o_swa.shape[1]
    nj = D_MODEL // tn
    o_spec = pl.BlockSpec((tm, kb), lambda i, j: (i, 0))

    def w_spec(br):
        return pl.BlockSpec((kb, tn), lambda i, j: (br, j))

    def g_spec(br):
        return pl.BlockSpec((tm, tn), lambda i, j: (i, br * nj + j))

    return pl.pallas_call(
        _merge_kernel,
        grid=(n // tm, nj),
        in_specs=[o_spec, o_spec, o_spec, w_spec(0), w_spec(1), w_spec(2), g_spec(0), g_spec(1), g_spec(2)],
        out_specs=pl.BlockSpec((tm, tn), lambda i, j: (i, j)),
        out_shape=jax.ShapeDtypeStruct((n, D_MODEL), BF16),
        compiler_params=pltpu.CompilerParams(dimension_semantics=("parallel", "parallel")),
        name="merge",
    )(o_swa, o_ret, o_mem, w_br, w_br, w_br, gates, gates, gates)


def _ffn_kernel(h_ref, hj_ref, gf_ref, wug_ref, wuv_ref, wcg_ref, wcv_ref, bg_ref, bv_ref, wd_ref,
                cbg_ref, cbv_ref, gfin_ref, y_ref, cog_ref, cov_ref,
                u_sc, act_sc, ag_sc, av_sc, y_sc, cg_sc, cv_sc, *, tm, tf, td, nf, seg, nseg):
    i = pl.program_id(1)
    t = pl.program_id(2)
    nd = D_MODEL // td

    seg_t = tm // nseg
    stride = seg_t + 8

    def up_proj():
        u = u_sc[...]
        for a_sc, wu_ref in ((ag_sc, wug_ref), (av_sc, wuv_ref)):
            a = jnp.dot(u, wu_ref[...], preferred_element_type=F32)
            for s in range(nseg):
                a_sc[s * stride + 8:(s + 1) * stride, :] = a[s * seg_t:(s + 1) * seg_t]

    def conv_act():
        fb = t - 1
        for a_sc, cb_ref, carry_sc, co_ref in ((ag_sc, cbg_ref, cg_sc, cog_ref), (av_sc, cbv_ref, cv_sc, cov_ref)):
            if nseg == 1:
                @pl.when(i == 0)
                def _():
                    a_sc[6:8, :] = cb_ref[0]

                @pl.when(i > 0)
                def _():
                    a_sc[6:8, :] = carry_sc[fb, 6:8, :]

                carry_sc[fb] = a_sc[tm:tm + 8, :]
            else:
                for s in range(nseg):
                    a_sc[s * stride + 6:s * stride + 8, :] = cb_ref[s]
            for s in range(nseg):
                co_ref[0, s] = a_sc[(s + 1) * stride - 8:(s + 1) * stride, :]

        def conv(a_sc, wc_ref, b_ref, r0, c0):
            tap = lambda k: a_sc[r0 - k:r0 - k + CONV_ROWS, c0:c0 + LANE]
            return b_ref[:, c0:c0 + LANE] + (wc_ref[0:1, c0:c0 + LANE] * tap(2) + wc_ref[1:2, c0:c0 + LANE] * tap(1)
                                             + wc_ref[2:3, c0:c0 + LANE] * tap(0))

        for c0 in range(0, tf, LANE):
            for s in range(nseg):
                for r in range(0, seg_t, CONV_ROWS):
                    cg = conv(ag_sc, wcg_ref, bg_ref, s * stride + 8 + r, c0)
                    cv = conv(av_sc, wcv_ref, bv_ref, s * stride + 8 + r, c0)
                    act_sc[fb, s * seg_t + r:s * seg_t + r + CONV_ROWS, c0:c0 + LANE] = (
                        cg * _sigmoid(cg) * cv).astype(BF16)

    @pl.when(t == 0)
    def _():
        u_sc[...] = _rms(h_ref[0], gf_ref[...]).astype(BF16)
        up_proj()

    @pl.when((t >= 1) & (t < nf))
    def _():
        conv_act()
        up_proj()

    @pl.when(t == nf)
    def _():
        conv_act()

    @pl.when(t > nf)
    def _():
        act = jnp.concatenate([act_sc[f] for f in range(nf)], axis=1)
        y_sc[t - nf - 1] = hj_ref[0] + jnp.dot(act, wd_ref[...], preferred_element_type=F32)

    @pl.when(t == nf + nd)
    def _():
        full = jnp.concatenate([y_sc[k] for k in range(nd)], axis=1)
        y_ref[0] = _rms(full, gfin_ref[...])


def _ffn(h, conv_buf, g_ffn, w_up, w_conv, b_conv, w_down, g_final, *, tm, tf, td, seg, nseg):
    bt, rows, d = h.shape
    nf = D_FF // tf
    nd = d // td
    nt = rows // tm
    up_blk = lambda t: jnp.minimum(t, nf - 1)
    conv_blk = lambda t: jnp.clip(t - 1, 0, nf - 1)
    down_blk = lambda t: jnp.clip(t - nf - 1, 0, nd - 1)
    in_specs = [
        pl.BlockSpec((1, tm, d), lambda b, i, t: (b, i, 0)),
        pl.BlockSpec((1, tm, td), lambda b, i, t: (b, i, down_blk(t))),
        pl.BlockSpec((1, d), lambda b, i, t: (0, 0)),
        pl.BlockSpec((d, tf), lambda b, i, t: (0, up_blk(t))),
        pl.BlockSpec((d, tf), lambda b, i, t: (0, nf + up_blk(t))),
        pl.BlockSpec((3, tf), lambda b, i, t: (0, conv_blk(t))),
        pl.BlockSpec((3, tf), lambda b, i, t: (0, nf + conv_blk(t))),
        pl.BlockSpec((1, tf), lambda b, i, t: (0, conv_blk(t))),
        pl.BlockSpec((1, tf), lambda b, i, t: (0, nf + conv_blk(t))),
        pl.BlockSpec((D_FF, td), lambda b, i, t: (0, down_blk(t))),
        pl.BlockSpec((nseg, 2, tf), lambda b, i, t: (b, 0, conv_blk(t))),
        pl.BlockSpec((nseg, 2, tf), lambda b, i, t: (b, 0, nf + conv_blk(t))),
        pl.BlockSpec((1, d), lambda b, i, t: (0, 0)),
    ]
    out_specs = [
        pl.BlockSpec((1, tm, d), lambda b, i, t: (b, i, 0)),
        pl.BlockSpec((1, nseg, 8, tf), lambda b, i, t: (b * nt + i, 0, 0, conv_blk(t))),
        pl.BlockSpec((1, nseg, 8, tf), lambda b, i, t: (b * nt + i, 0, 0, conv_blk(t))),
    ]
    return pl.pallas_call(
        functools.partial(_ffn_kernel, tm=tm, tf=tf, td=td, nf=nf, seg=seg, nseg=nseg),
        grid=(bt, nt, nf + 1 + nd),
        in_specs=in_specs,
        out_specs=out_specs,
        out_shape=[jax.ShapeDtypeStruct((bt, rows, d), F32),
                   jax.ShapeDtypeStruct((bt * nt, nseg, 8, D_FF), F32),
                   jax.ShapeDtypeStruct((bt * nt, nseg, 8, D_FF), F32)],
        scratch_shapes=[pltpu.VMEM((tm, d), BF16), pltpu.VMEM((nf, tm, tf), BF16),
                        pltpu.VMEM((tm + 8 * nseg, tf), F32), pltpu.VMEM((tm + 8 * nseg, tf), F32),
                        pltpu.VMEM((nd, tm, td), F32),
                        pltpu.VMEM((nf, 8, tf), F32), pltpu.VMEM((nf, 8, tf), F32)],
        compiler_params=pltpu.CompilerParams(dimension_semantics=("arbitrary", "arbitrary", "arbitrary")),
        name="conv_ffn",
    )(h, h, g_ffn.reshape(1, d), w_up, w_up, w_conv, w_conv, b_conv.reshape(1, -1), b_conv.reshape(1, -1),
      w_down, conv_buf, conv_buf, g_final.reshape(1, d))


def _rope_tables(pos0, s):
    half = RET_DIM // 2
    inv_freq = 1.0 / (RET_ROPE_BASE ** jnp.linspace(0.0, 1.0, half, dtype=F32))
    ang = (pos0 + jnp.arange(s)).astype(F32)[:, None] * inv_freq[None, :]
    cos, sin = jnp.cos(ang), jnp.sin(ang)
    return jnp.concatenate([cos, cos], axis=1), jnp.concatenate([-sin, sin], axis=1)


def _prep_weights(g_mix, w_in, b_gate, sink, w_br, w_o, g_ffn, w_up, w_conv, b_conv, w_down):
    o = 0
    cuts = {}
    for name, width in (("qa", SWA_Q), ("ka", SWA_KV), ("va", SWA_KV), ("qr", RET_W), ("kr", RET_W),
                        ("vr", RET_W), ("gr", RET_W), ("qm", MEM_W), ("gl", N_BRANCH * D_MODEL)):
        cuts[name] = (o, o + width)
        o += width
    cols = lambda *names: jnp.concatenate([w_in[:, cuts[n][0]:cuts[n][1]] for n in names], axis=1).astype(BF16)
    ones = lambda n, v: jnp.full((n,), v, F32)
    return dict(
        g_mix=g_mix, b_gate=b_gate, sink=sink, g_ffn=g_ffn, w_conv=w_conv, b_conv=b_conv,
        w_plain=cols("qa", "vr", "qm"),
        s_plain=jnp.concatenate([ones(SWA_Q, SWA_HEAD_DIM ** -0.5), ones(RET_W, 1.0),
                                 ones(MEM_W, MEM_HEAD_DIM ** -0.5)]),
        w_kv=cols("ka", "va"),
        w_rot=cols("qr", "kr"),
        s_rot=jnp.concatenate([ones(RET_W, 1.0), ones(RET_W, RET_DIM ** -0.5)]),
        w_g=cols("gr"),
        w_gate=cols("gl"),
        w_br=w_br.astype(BF16), w_o=w_o.astype(BF16), w_up=w_up.astype(BF16), w_down=w_down.astype(BF16),
    )


def _run_group(x, pos0, mem_k, mem_v, swa_cache, ret_state, conv_buf, wts, g_final, log_g):
    b, s, d = x.shape
    n = b * s
    tm = min(1024, n)
    x2 = x.reshape(n, d)
    u = _rmsnorm(x2, wts["g_mix"], min(512, n))

    plain = _mm(u, wts["w_plain"], epilogue="scale", out_dtype=BF16, tm=tm, tn=1024,
                colvec=wts["s_plain"], name="proj_plain")
    kv = _mm(u, wts["w_kv"], epilogue="plain", out_dtype=F32, tm=tm, tn=2 * SWA_KV, name="proj_kv")
    cc, ss = _rope_tables(pos0, s)
    if s < tm:
        cc, ss = jnp.tile(cc, (tm // s, 1)), jnp.tile(ss, (tm // s, 1))
    rot = _mm(u, wts["w_rot"], epilogue="rotary", out_dtype=BF16, tm=tm, tn=1024,
              colvec=wts["s_rot"], tables=(cc, ss), name="proj_rot")
    sgate = _mm(u, wts["w_g"], epilogue="silu", out_dtype=BF16, tm=tm, tn=1024, name="proj_silu")
    gates = _mm(u, wts["w_gate"], epilogue="sigmoid_bias", out_dtype=BF16, tm=tm, tn=1024,
                colvec=wts["b_gate"], name="proj_gate")

    plain3 = plain.reshape(b, s, -1)
    kv3 = kv.reshape(b, s, 2 * SWA_KV)
    if swa_cache is None:
        o_swa = _swa(plain3, 0, kv3, 0, kv3, 1, wts["sink"], sq=s, qt=min(256, s), kv_off=0)
        new_k = kv3[:, s - WINDOW:, :SWA_KV]
        new_v = kv3[:, s - WINDOW:, SWA_KV:]
    else:
        ck = swa_cache[0].reshape(b, -1, SWA_KV)
        cv = swa_cache[1].reshape(b, -1, SWA_KV)
        n_keep = ck.shape[1]
        k_all = jnp.concatenate([ck, kv3[:, :, :SWA_KV]], axis=1)
        v_all = jnp.concatenate([cv, kv3[:, :, SWA_KV:]], axis=1)
        o_swa = _swa(plain3, 0, k_all, 0, v_all, 0, wts["sink"], sq=s, qt=s, kv_off=n_keep)
        new_k = k_all[:, -n_keep:]
        new_v = v_all[:, -n_keep:]
    state0 = jnp.zeros((b, RET_HEADS, RET_DIM, RET_DIM), F32) if ret_state is None else ret_state
    o_ret, s_new = _retention(rot.reshape(b, s, -1), 0, 1, plain3, 1, sgate.reshape(b, s, -1), state0, log_g,
                              s=s, c=min(256, s))
    o_mem = _mem_attend(plain3, 2, mem_k.reshape(b, N_MEM, MEM_W), mem_v.reshape(b, N_MEM, MEM_W),
                        s=s, tq=min(512, s))
    merged = _merge(o_swa.reshape(n, -1), o_ret.reshape(n, -1), o_mem.reshape(n, -1), wts["w_br"], gates,
                    tm=tm, tn=512)
    h1 = _mm(merged, wts["w_o"], epilogue="residual", out_dtype=F32, tm=tm, tn=1024, residual=x2, name="proj_out")

    if s >= 512:
        y, cog, cov = _ffn(h1.reshape(b, s, d), conv_buf, wts["g_ffn"], wts["w_up"], wts["w_conv"], wts["b_conv"],
                           wts["w_down"], g_final, tm=512, tf=512, td=512, seg=s, nseg=1)
    else:
        y, cog, cov = _ffn(h1.reshape(1, n, d), conv_buf, wts["g_ffn"], wts["w_up"], wts["w_conv"], wts["b_conv"],
                           wts["w_down"], g_final, tm=n, tf=512, td=512, seg=s, nseg=b)
    last2 = lambda t: t.reshape(-1, t.shape[0] * t.shape[1] // b, 8, D_FF)[:, -1, 6:8]
    new_buf = jnp.concatenate([last2(cog), last2(cov)], axis=-1)
    return (y.reshape(b, s, d), new_k.reshape(b, -1, SWA_KV_HEADS, SWA_HEAD_DIM),
            new_v.reshape(b, -1, SWA_KV_HEADS, SWA_HEAD_DIM), s_new, new_buf)


def _memory_kv(mem, g_mem, w_mem_kv):
    b, m, d = mem.shape
    u = _rmsnorm(mem.reshape(b * m, d), g_mem, min(512, b * m))
    kv = _mm(u, w_mem_kv.astype(BF16), epilogue="plain", out_dtype=F32, tm=min(1024, b * m), tn=1024, name="mem_kv")
    return (kv[:, :MEM_W].reshape(b, m, MEM_HEADS, MEM_HEAD_DIM), kv[:, MEM_W:].reshape(b, m, MEM_HEADS, MEM_HEAD_DIM))


def kernel(x_prompt, x_sample, mem_prompt, cache_swa_k, cache_swa_v, state_ret, state_ffn_conv, cache_mem_k, cache_mem_v, g_mix, w_in, b_gate, sink, w_br, w_o, g_mem, w_mem_kv, g_ffn, w_up, w_conv, b_conv, w_down, g_final):
    bp = x_prompt.shape[0]
    depth = w_in.shape[0]
    log_g = jnp.log1p(-jnp.exp2(-5.0 - jnp.arange(RET_HEADS, dtype=F32)))
    hp, hs = x_prompt, x_sample
    outs_p = [[] for _ in range(6)]
    outs_s = [[] for _ in range(4)]
    for l in range(depth):
        wts = _prep_weights(g_mix[l], w_in[l], b_gate[l], sink[l], w_br[l], w_o[l],
                            g_ffn[l], w_up[l], w_conv[l], b_conv[l], w_down[l])
        mk, mv = _memory_kv(mem_prompt, g_mem[l], w_mem_kv[l])
        zero_buf = jnp.zeros((bp, 2, 2 * D_FF), F32)
        assert depth == 1
        hp, kp, vp, sp, cp = _run_group(hp, 0, mk, mv, None, None, zero_buf, wts, g_final, log_g)
        hs, ksn, vsn, ssn, csn = _run_group(hs, PAST_LEN, cache_mem_k[l], cache_mem_v[l],
                                            (cache_swa_k[l], cache_swa_v[l]), state_ret[l],
                                            state_ffn_conv[l], wts, g_final, log_g)
        for lst, val in zip(outs_p, (kp, vp, sp, cp, mk, mv)):
            lst.append(val)
        for lst, val in zip(outs_s, (ksn, vsn, ssn, csn)):
            lst.append(val)
    return (hp, hs, *[jnp.stack(v) for v in outs_p], *[jnp.stack(v) for v in outs_s])
```

```python
import functools

import jax
import jax.numpy as jnp
from jax import lax
from jax.experimental import pallas as pl
from jax.experimental.pallas import tpu as pltpu

F32 = jnp.float32
BF16 = jnp.bfloat16

D_MODEL = 2048
CHUNK = 64
WINDOW = 128
SWA_HEADS = 16
SWA_KV_HEADS = 4
SWA_GROUP = SWA_HEADS // SWA_KV_HEADS
SWA_HEAD_DIM = 64
RET_HEADS = 8
RET_DIM = 128
RET_ROPE_BASE = 10000.0
N_MEM = 256
MEM_HEADS = 4
MEM_HEAD_DIM = 256
D_FF = 5632
N_BRANCH = 3
EPS = 1e-6
NEG = -1e30
PAST_LEN = 1024

SWA_Q = SWA_HEADS * SWA_HEAD_DIM
SWA_KV = SWA_KV_HEADS * SWA_HEAD_DIM
RET_W = RET_HEADS * RET_DIM
MEM_W = MEM_HEADS * MEM_HEAD_DIM
LANE = 128
CONV_ROWS = 32

def _sigmoid(x):
    return 1.0 / (1.0 + jnp.exp(-x))


def _rms(x, g):
    return x * lax.rsqrt(jnp.mean(x * x, axis=-1, keepdims=True) + EPS) * g


def _rmsnorm_kernel(x_ref, g_ref, o_ref):
    o_ref[...] = _rms(x_ref[...], g_ref[...]).astype(o_ref.dtype)


def _rmsnorm(x, g, tm):
    n, d = x.shape
    return pl.pallas_call(
        _rmsnorm_kernel,
        grid=(n // tm,),
        in_specs=[pl.BlockSpec((tm, d), lambda i: (i, 0)),
                  pl.BlockSpec((1, d), lambda i: (0, 0))],
        out_specs=pl.BlockSpec((tm, d), lambda i: (i, 0)),
        out_shape=jax.ShapeDtypeStruct((n, d), BF16),
        name="rmsnorm",
    )(x, g.reshape(1, d))


def _mm_kernel(x_ref, w_ref, *rest, epilogue, tn):
    o_ref = rest[-1]
    acc = jnp.dot(x_ref[...], w_ref[...], preferred_element_type=F32)
    if epilogue == "plain":
        out = acc
    elif epilogue == "scale":
        out = acc * rest[0][...]
    elif epilogue == "silu":
        out = acc * _sigmoid(acc)
    elif epilogue == "sigmoid_bias":
        out = _sigmoid(acc + rest[0][...])
    elif epilogue == "residual":
        out = rest[0][...] + acc
    elif epilogue == "rotary":
        cc, ss = rest[1][...], rest[2][...]
        pieces = []
        for j in range(tn // LANE):
            xh = acc[:, j * LANE:(j + 1) * LANE]
            pieces.append(xh * cc + pltpu.roll(xh, LANE // 2, 1) * ss)
        out = jnp.concatenate(pieces, axis=1) * rest[0][...]
    else:
        raise ValueError(epilogue)
    o_ref[...] = out.astype(o_ref.dtype)


def _mm(x, w, *, epilogue, out_dtype, tm, tn, colvec=None, residual=None, tables=None, name):
    n, k = x.shape
    ncols = w.shape[1]
    in_specs = [pl.BlockSpec((tm, k), lambda i, j: (i, 0)),
                pl.BlockSpec((k, tn), lambda i, j: (0, j))]
    args = [x, w]
    if colvec is not None:
        in_specs.append(pl.BlockSpec((1, tn), lambda i, j: (0, j)))
        args.append(colvec.reshape(1, ncols).astype(F32))
    if residual is not None:
        in_specs.append(pl.BlockSpec((tm, tn), lambda i, j: (i, j)))
        args.append(residual)
    if tables is not None:
        nblk = tables[0].shape[0] // tm
        for t in tables:
            in_specs.append(pl.BlockSpec((tm, LANE), lambda i, j: (i % nblk, 0)))
            args.append(t)
    return pl.pallas_call(
        functools.partial(_mm_kernel, epilogue=epilogue, tn=tn),
        grid=(n // tm, ncols // tn),
        in_specs=in_specs,
        out_specs=pl.BlockSpec((tm, tn), lambda i, j: (i, j)),
        out_shape=jax.ShapeDtypeStruct((n, ncols), out_dtype),
        compiler_params=pltpu.CompilerParams(dimension_semantics=("parallel", "parallel")),
        name=name,
    )(*args)


def _swa_kernel(sink_ref, q_ref, kp_ref, kc_ref, vp_ref, vc_ref, o_ref, *, qt, kv_off):
    i = pl.program_id(1)
    nq = qt // CHUNK
    w = (nq + 2) * CHUNK
    wp = -(-w // LANE) * LANE
    first_ok = jnp.where(i * qt + kv_off >= WINDOW, 0, 2)

    def window(p_ref, c_ref):
        parts = [p_ref[0], c_ref[0]]
        if wp > w:
            parts.append(jnp.zeros((wp - w, SWA_KV), F32))
        return jnp.concatenate(parts, axis=0)

    kwin = window(kp_ref, kc_ref)
    vwin = window(vp_ref, vc_ref)
    lane = lax.broadcasted_iota(jnp.int32, (wp, LANE), 1)
    lane_g = lax.broadcasted_iota(jnp.int32, (wp, SWA_KV), 1) // SWA_HEAD_DIM
    qc = lax.broadcasted_iota(jnp.int32, (qt, wp), 0) // CHUNK
    kc = lax.broadcasted_iota(jnp.int32, (qt, wp), 1) // CHUNK
    band = (kc >= qc) & (kc <= qc + 2) & (kc >= first_ok)
    out_g = lax.broadcasted_iota(jnp.int32, (qt, SWA_KV), 1) // SWA_HEAD_DIM

    def block_diag(win, h):
        col, half = divmod(h, 2)
        x = win[:, col * LANE:(col + 1) * LANE]
        xr = pltpu.roll(x, SWA_HEAD_DIM, 1)
        lo = lane < SWA_HEAD_DIM
        r = jnp.where(lo, x, xr) if half == 0 else jnp.where(lo, xr, x)
        r2 = jnp.concatenate([r, r], axis=1).astype(BF16)
        zero = jnp.zeros_like(r2)
        return jnp.concatenate([jnp.where(lane_g == g, r2, zero) for g in range(SWA_GROUP)], axis=0)

    for h in range(SWA_KV_HEADS):
        bk = block_diag(kwin, h)
        bv = block_diag(vwin, h)
        qh = q_ref[0, :, h * SWA_KV:(h + 1) * SWA_KV]
        s = lax.dot_general(qh, bk, (((1,), (1,)), ((), ())), preferred_element_type=F32)
        ps = []
        inv = jnp.zeros((qt, SWA_KV), F32)
        for g in range(SWA_GROUP):
            sg = jnp.where(band, s[:, g * wp:(g + 1) * wp], NEG)
            sk = sink_ref[h * SWA_GROUP + g]
            m = jnp.maximum(jnp.max(sg, axis=1, keepdims=True), sk)
            p = jnp.exp(sg - m)
            den = jnp.sum(p, axis=1, keepdims=True) + jnp.exp(sk - m)
            ps.append(p.astype(BF16))
            inv = jnp.where(out_g == g, 1.0 / den, inv)
        o = jnp.dot(jnp.concatenate(ps, axis=1), bv, preferred_element_type=F32)
        o_ref[0, :, h * SWA_KV:(h + 1) * SWA_KV] = (o * inv).astype(o_ref.dtype)


def _swa(q, q_col, k, k_col, v, v_col, sink, *, sq, qt, kv_off):
    b = q.shape[0]
    cur_off = kv_off // qt
    prev_off = kv_off // WINDOW - 1
    per = qt // WINDOW if qt >= WINDOW else 0

    def prev_map(kcol):
        return lambda bi, i, s: (bi, jnp.maximum(i * per + prev_off, 0), kcol)

    def cur_map(kcol):
        return lambda bi, i, s: (bi, i + cur_off, kcol)

    grid_spec = pltpu.PrefetchScalarGridSpec(
        num_scalar_prefetch=1,
        grid=(b, sq // qt),
        in_specs=[pl.BlockSpec((1, qt, SWA_Q), lambda bi, i, s: (bi, i, q_col)),
                  pl.BlockSpec((1, WINDOW, SWA_KV), prev_map(k_col)),
                  pl.BlockSpec((1, qt, SWA_KV), cur_map(k_col)),
                  pl.BlockSpec((1, WINDOW, SWA_KV), prev_map(v_col)),
                  pl.BlockSpec((1, qt, SWA_KV), cur_map(v_col))],
        out_specs=pl.BlockSpec((1, qt, SWA_Q), lambda bi, i, s: (bi, i, 0)),
    )
    return pl.pallas_call(
        functools.partial(_swa_kernel, qt=qt, kv_off=kv_off),
        grid_spec=grid_spec,
        out_shape=jax.ShapeDtypeStruct((b, sq, SWA_Q), BF16),
        compiler_params=pltpu.CompilerParams(dimension_semantics=("parallel", "parallel")),
        name="swa",
    )(sink.astype(F32), q, k, k, v, v)


def _ret_kernel(lg_ref, q_ref, k_ref, v_ref, g_ref, s0_ref, o_ref, st_ref, dec_ref, *, c):
    bi = pl.program_id(0)
    ci = pl.program_id(1)

    @pl.when((bi == 0) & (ci == 0))
    def _():
        n = lax.broadcasted_iota(jnp.int32, (c, c), 0)
        m = lax.broadcasted_iota(jnp.int32, (c, c), 1)
        diff = (n - m).astype(F32)
        for h in range(RET_HEADS):
            dec_ref[h] = jnp.where(diff >= 0, jnp.exp(lg_ref[h] * jnp.maximum(diff, 0.0)), 0.0)

    @pl.when(ci == 0)
    def _():
        st_ref[...] = s0_ref[...]

    nvec = lax.broadcasted_iota(jnp.int32, (c, 1), 0).astype(F32)
    for h in range(RET_HEADS):
        lg = lg_ref[h]
        sl = slice(h * RET_DIM, (h + 1) * RET_DIM)
        qh = q_ref[0, :, sl]
        kh = k_ref[0, :, sl]
        vh = v_ref[0, :, sl]
        state = st_ref[0, h]
        inner = lax.dot_general(qh, kh, (((1,), (1,)), ((), ())), preferred_element_type=F32) * dec_ref[h]
        o = jnp.dot(inner.astype(BF16), vh, preferred_element_type=F32)
        cross = jnp.dot(qh, state.astype(BF16), preferred_element_type=F32)
        o = o + cross * jnp.exp(lg * (nvec + 1.0))
        zeta = jnp.exp(lg * (float(c) - 1.0 - nvec))
        kz = (kh.astype(F32) * zeta).astype(BF16)
        upd = lax.dot_general(kz, vh, (((0,), (0,)), ((), ())), preferred_element_type=F32)
        decay_c = jnp.exp(lg * jnp.full((1, RET_DIM), float(c), F32))
        st_ref[0, h] = decay_c * state + upd
        on = o * lax.rsqrt(jnp.mean(o * o, axis=-1, keepdims=True) + EPS)
        o_ref[0, :, sl] = (g_ref[0, :, sl].astype(F32) * on).astype(o_ref.dtype)


def _retention(qk, q_col, k_col, v, v_col, gate, state0, log_g, *, s, c):
    b = qk.shape[0]

    def seq(col):
        return pl.BlockSpec((1, c, RET_W), lambda bi, ci, lg: (bi, ci, col))

    st_spec = pl.BlockSpec((1, RET_HEADS, RET_DIM, RET_DIM), lambda bi, ci, lg: (bi, 0, 0, 0))
    grid_spec = pltpu.PrefetchScalarGridSpec(
        num_scalar_prefetch=1,
        grid=(b, s // c),
        in_specs=[seq(q_col), seq(k_col), seq(v_col), seq(0), st_spec],
        out_specs=[seq(0), st_spec],
        scratch_shapes=[pltpu.VMEM((RET_HEADS, c, c), F32)],
    )
    return pl.pallas_call(
        functools.partial(_ret_kernel, c=c),
        grid_spec=grid_spec,
        out_shape=[jax.ShapeDtypeStruct((b, s, RET_W), BF16),
                   jax.ShapeDtypeStruct((b, RET_HEADS, RET_DIM, RET_DIM), F32)],
        compiler_params=pltpu.CompilerParams(dimension_semantics=("arbitrary", "arbitrary")),
        name="retention",
    )(log_g, qk, qk, v, gate, state0)


def _mem_kernel(q_ref, k_ref, v_ref, o_ref):
    for h in range(MEM_HEADS):
        sl = slice(h * MEM_HEAD_DIM, (h + 1) * MEM_HEAD_DIM)
        qh = q_ref[0, :, sl]
        kh = k_ref[0, :, sl].astype(BF16)
        vh = v_ref[0, :, sl].astype(BF16)
        s = lax.dot_general(qh, kh, (((1,), (1,)), ((), ())), preferred_element_type=F32)
        m = jnp.max(s, axis=1, keepdims=True)
        p = jnp.exp(s - m)
        den = jnp.sum(p, axis=1, keepdims=True)
        o = jnp.dot(p.astype(BF16), vh, preferred_element_type=F32)
        o_ref[0, :, sl] = (o * (1.0 / den)).astype(o_ref.dtype)


def _mem_attend(q, q_col, mk, mv, *, s, tq):
    b = q.shape[0]
    kv_spec = pl.BlockSpec((1, N_MEM, MEM_W), lambda bi, i: (bi, 0, 0))
    return pl.pallas_call(
        _mem_kernel,
        grid=(b, s // tq),
        in_specs=[pl.BlockSpec((1, tq, MEM_W), lambda bi, i: (bi, i, q_col)), kv_spec, kv_spec],
        out_specs=pl.BlockSpec((1, tq, MEM_W), lambda bi, i: (bi, i, 0)),
        out_shape=jax.ShapeDtypeStruct((b, s, MEM_W), BF16),
        compiler_params=pltpu.CompilerParams(dimension_semantics=("parallel", "parallel")),
        name="mem_attend",
    )(q, mk, mv)


def _merge_kernel(oa_ref, ob_ref, oc_ref, wa_ref, wb_ref, wc_ref, ga_ref, gb_ref, gc_ref, o_ref):
    acc = ga_ref[...].astype(F32) * jnp.dot(oa_ref[...], wa_ref[...], preferred_element_type=F32)
    acc = acc + gb_ref[...].astype(F32) * jnp.dot(ob_ref[...], wb_ref[...], preferred_element_type=F32)
    acc = acc + gc_ref[...].astype(F32) * jnp.dot(oc_ref[...], wc_ref[...], preferred_element_type=F32)
    o_ref[...] = acc.astype(o_ref.dtype)


def _merge(o_swa, o_ret, o_mem, w_br, gates, *, tm, tn):
    n = o_swa.shape[0]
    kb = o_swa.shape[1]
    nj = D_MODEL // tn
    o_spec = pl.BlockSpec((tm, kb), lambda i, j: (i, 0))

    def w_spec(br):
        return pl.BlockSpec((kb, tn), lambda i, j: (br, j))

    def g_spec(br):
        return pl.BlockSpec((tm, tn), lambda i, j: (i, br * nj + j))

    return pl.pallas_call(
        _merge_kernel,
        grid=(n // tm, nj),
        in_specs=[o_spec, o_spec, o_spec, w_spec(0), w_spec(1), w_spec(2), g_spec(0), g_spec(1), g_spec(2)],
        out_specs=pl.BlockSpec((tm, tn), lambda i, j: (i, j)),
        out_shape=jax.ShapeDtypeStruct((n, D_MODEL), BF16),
        compiler_params=pltpu.CompilerParams(dimension_semantics=("parallel", "parallel")),
        name="merge",
    )(o_swa, o_ret, o_mem, w_br, w_br, w_br, gates, gates, gates)


def _ffn_kernel(h_ref, hj_ref, gf_ref, wug_ref, wuv_ref, wcg_ref, wcv_ref, bg_ref, bv_ref, wd_ref,
                cbg_ref, cbv_ref, gfin_ref, y_ref, cog_ref, cov_ref,
                u_sc, act_sc, actl_sc, ag0_sc, av0_sc, ag1_sc, av1_sc, y_sc, cg_sc, cv_sc,
                *, tm, tf, td, nf, seg, nseg):
    i = pl.program_id(1)
    t = pl.program_id(2)
    nd = D_MODEL // td
    a_bufs = ((ag0_sc, av0_sc), (ag1_sc, av1_sc))

    seg_t = tm // nseg
    stride = seg_t + 8

    def up_proj(par):
        u = u_sc[...]
        for a_sc, wu_ref in zip(a_bufs[par], (wug_ref, wuv_ref)):
            a = jnp.dot(u, wu_ref[...], preferred_element_type=F32)
            for s in range(nseg):
                a_sc[s * stride + 8:(s + 1) * stride, :] = a[s * seg_t:(s + 1) * seg_t]

    def conv_act(par, last=False):
        fb = t - 1
        ag_sc, av_sc = a_bufs[par]
        act_w = actl_sc if last else act_sc.at[fb]
        for a_sc, cb_ref, carry_sc, co_ref in ((ag_sc, cbg_ref, cg_sc, cog_ref), (av_sc, cbv_ref, cv_sc, cov_ref)):
            if nseg == 1:
                @pl.when(i == 0)
                def _():
                    a_sc[6:8, :] = cb_ref[0]

                @pl.when(i > 0)
                def _():
                    a_sc[6:8, :] = carry_sc[fb, 6:8, :]

                carry_sc[fb] = a_sc[tm:tm + 8, :]
            else:
                for s in range(nseg):
                    a_sc[s * stride + 6:s * stride + 8, :] = cb_ref[s]
            for s in range(nseg):
                co_ref[0, s] = a_sc[(s + 1) * stride - 8:(s + 1) * stride, :]

        def conv(a_sc, wc_ref, b_ref, r0, c0):
            tap = lambda k: a_sc[r0 - k:r0 - k + CONV_ROWS, c0:c0 + LANE]
            return b_ref[:, c0:c0 + LANE] + (wc_ref[0:1, c0:c0 + LANE] * tap(2) + wc_ref[1:2, c0:c0 + LANE] * tap(1)
                                             + wc_ref[2:3, c0:c0 + LANE] * tap(0))

        for c0 in range(0, tf, LANE):
            for s in range(nseg):
                for r in range(0, seg_t, CONV_ROWS):
                    cg = conv(ag_sc, wcg_ref, bg_ref, s * stride + 8 + r, c0)
                    cv = conv(av_sc, wcv_ref, bv_ref, s * stride + 8 + r, c0)
                    act_w[s * seg_t + r:s * seg_t + r + CONV_ROWS, c0:c0 + LANE] = (
                        cg * _sigmoid(cg) * cv).astype(BF16)

    @pl.when(t == 0)
    def _():
        u_sc[...] = _rms(h_ref[0], gf_ref[...]).astype(BF16)
        up_proj(0)

    for par in (0, 1):
        @pl.when((t >= 1) & (t < nf) & (t % 2 == par))
        def _():
            conv_act(1 - par)
            up_proj(par)

    def down(j):
        act = jnp.concatenate([act_sc[f] for f in range(nf - 1)] + [actl_sc[...]], axis=1)
        y_sc[j] = hj_ref[0] + jnp.dot(act, wd_ref[...], preferred_element_type=F32)

    @pl.when(t == nf)
    def _():
        conv_act((nf - 1) % 2, last=True)
        down(0)

    @pl.when(t > nf)
    def _():
        down(t - nf)

    @pl.when(t == nf + nd - 1)
    def _():
        full = jnp.concatenate([y_sc[k] for k in range(nd)], axis=1)
        y_ref[0] = _rms(full, gfin_ref[...])


def _ffn(h, conv_buf, g_ffn, w_up, w_conv, b_conv, w_down, g_final, *, tm, tf, td, seg, nseg):
    bt, rows, d = h.shape
    nf = D_FF // tf
    nd = d // td
    nt = rows // tm
    up_blk = lambda t: jnp.minimum(t, nf - 1)
    conv_blk = lambda t: jnp.clip(t - 1, 0, nf - 1)
    down_blk = lambda t: jnp.clip(t - nf, 0, nd - 1)
    in_specs = [
        pl.BlockSpec((1, tm, d), lambda b, i, t: (b, i, 0)),
        pl.BlockSpec((1, tm, td), lambda b, i, t: (b, i, down_blk(t))),
        pl.BlockSpec((1, d), lambda b, i, t: (0, 0)),
        pl.BlockSpec((d, tf), lambda b, i, t: (0, up_blk(t))),
        pl.BlockSpec((d, tf), lambda b, i, t: (0, nf + up_blk(t))),
        pl.BlockSpec((3, tf), lambda b, i, t: (0, conv_blk(t))),
        pl.BlockSpec((3, tf), lambda b, i, t: (0, nf + conv_blk(t))),
        pl.BlockSpec((1, tf), lambda b, i, t: (0, conv_blk(t))),
        pl.BlockSpec((1, tf), lambda b, i, t: (0, nf + conv_blk(t))),
        pl.BlockSpec((D_FF, td), lambda b, i, t: (0, down_blk(t))),
        pl.BlockSpec((nseg, 2, tf), lambda b, i, t: (b, 0, conv_blk(t))),
        pl.BlockSpec((nseg, 2, tf), lambda b, i, t: (b, 0, nf + conv_blk(t))),
        pl.BlockSpec((1, d), lambda b, i, t: (0, 0)),
    ]
    out_specs = [
        pl.BlockSpec((1, tm, d), lambda b, i, t: (b, i, 0)),
        pl.BlockSpec((1, nseg, 8, tf), lambda b, i, t: (b * nt + i, 0, 0, conv_blk(t))),
        pl.BlockSpec((1, nseg, 8, tf), lambda b, i, t: (b * nt + i, 0, 0, conv_blk(t))),
    ]
    return pl.pallas_call(
        functools.partial(_ffn_kernel, tm=tm, tf=tf, td=td, nf=nf, seg=seg, nseg=nseg),
        grid=(bt, nt, nf + nd),
        in_specs=in_specs,
        out_specs=out_specs,
        out_shape=[jax.ShapeDtypeStruct((bt, rows, d), F32),
                   jax.ShapeDtypeStruct((bt * nt, nseg, 8, D_FF), F32),
                   jax.ShapeDtypeStruct((bt * nt, nseg, 8, D_FF), F32)],
        scratch_shapes=[pltpu.VMEM((tm, d), BF16), pltpu.VMEM((nf - 1, tm, tf), BF16), pltpu.VMEM((tm, tf), BF16),
                        pltpu.VMEM((tm + 8 * nseg, tf), F32), pltpu.VMEM((tm + 8 * nseg, tf), F32),
                        pltpu.VMEM((tm + 8 * nseg, tf), F32), pltpu.VMEM((tm + 8 * nseg, tf), F32),
                        pltpu.VMEM((nd, tm, td), F32),
                        pltpu.VMEM((nf, 8, tf), F32), pltpu.VMEM((nf, 8, tf), F32)],
        compiler_params=pltpu.CompilerParams(dimension_semantics=("arbitrary", "arbitrary", "arbitrary")),
        name="conv_ffn",
    )(h, h, g_ffn.reshape(1, d), w_up, w_up, w_conv, w_conv, b_conv.reshape(1, -1), b_conv.reshape(1, -1),
      w_down, conv_buf, conv_buf, g_final.reshape(1, d))


def _rope_tables(pos0, s):
    half = RET_DIM // 2
    inv_freq = 1.0 / (RET_ROPE_BASE ** jnp.linspace(0.0, 1.0, half, dtype=F32))
    ang = (pos0 + jnp.arange(s)).astype(F32)[:, None] * inv_freq[None, :]
    cos, sin = jnp.cos(ang), jnp.sin(ang)
    return jnp.concatenate([cos, cos], axis=1), jnp.concatenate([-sin, sin], axis=1)


def _prep_weights(g_mix, w_in, b_gate, sink, w_br, w_o, g_ffn, w_up, w_conv, b_conv, w_down):
    o = 0
    cuts = {}
    for name, width in (("qa", SWA_Q), ("ka", SWA_KV), ("va", SWA_KV), ("qr", RET_W), ("kr", RET_W),
                        ("vr", RET_W), ("gr", RET_W), ("qm", MEM_W), ("gl", N_BRANCH * D_MODEL)):
        cuts[name] = (o, o + width)
        o += width
    cols = lambda *names: jnp.concatenate([w_in[:, cuts[n][0]:cuts[n][1]] for n in names], axis=1).astype(BF16)
    ones = lambda n, v: jnp.full((n,), v, F32)
    return dict(
        g_mix=g_mix, b_gate=b_gate, sink=sink, g_ffn=g_ffn, w_conv=w_conv, b_conv=b_conv,
        w_plain=cols("qa", "vr", "qm"),
        s_plain=jnp.concatenate([ones(SWA_Q, SWA_HEAD_DIM ** -0.5), ones(RET_W, 1.0),
                                 ones(MEM_W, MEM_HEAD_DIM ** -0.5)]),
        w_kv=cols("ka", "va"),
        w_rot=cols("qr", "kr"),
        s_rot=jnp.concatenate([ones(RET_W, 1.0), ones(RET_W, RET_DIM ** -0.5)]),
        w_g=cols("gr"),
        w_gate=cols("gl"),
        w_br=w_br.astype(BF16), w_o=w_o.astype(BF16), w_up=w_up.astype(BF16), w_down=w_down.astype(BF16),
    )


def _run_group(x, pos0, mem_k, mem_v, swa_cache, ret_state, conv_buf, wts, g_final, log_g):
    b, s, d = x.shape
    n = b * s
    tm = min(1024, n)
    x2 = x.reshape(n, d)
    u = _rmsnorm(x2, wts["g_mix"], min(512, n))

    plain = _mm(u, wts["w_plain"], epilogue="scale", out_dtype=BF16, tm=tm, tn=1024,
                colvec=wts["s_plain"], name="proj_plain")
    kv = _mm(u, wts["w_kv"], epilogue="plain", out_dtype=F32, tm=tm, tn=2 * SWA_KV, name="proj_kv")
    cc, ss = _rope_tables(pos0, s)
    if s < tm:
        cc, ss = jnp.tile(cc, (tm // s, 1)), jnp.tile(ss, (tm // s, 1))
    rot = _mm(u, wts["w_rot"], epilogue="rotary", out_dtype=BF16, tm=tm, tn=1024,
              colvec=wts["s_rot"], tables=(cc, ss), name="proj_rot")
    sgate = _mm(u, wts["w_g"], epilogue="silu", out_dtype=BF16, tm=tm, tn=1024, name="proj_silu")
    gates = _mm(u, wts["w_gate"], epilogue="sigmoid_bias", out_dtype=BF16, tm=tm, tn=1024,
                colvec=wts["b_gate"], name="proj_gate")

    plain3 = plain.reshape(b, s, -1)
    kv3 = kv.reshape(b, s, 2 * SWA_KV)
    if swa_cache is None:
        o_swa = _swa(plain3, 0, kv3, 0, kv3, 1, wts["sink"], sq=s, qt=min(256, s), kv_off=0)
        new_k = kv3[:, s - WINDOW:, :SWA_KV]
        new_v = kv3[:, s - WINDOW:, SWA_KV:]
    else:
        ck = swa_cache[0].reshape(b, -1, SWA_KV)
        cv = swa_cache[1].reshape(b, -1, SWA_KV)
        n_keep = ck.shape[1]
        k_all = jnp.concatenate([ck, kv3[:, :, :SWA_KV]], axis=1)
        v_all = jnp.concatenate([cv, kv3[:, :, SWA_KV:]], axis=1)
        o_swa = _swa(plain3, 0, k_all, 0, v_all, 0, wts["sink"], sq=s, qt=s, kv_off=n_keep)
        new_k = k_all[:, -n_keep:]
        new_v = v_all[:, -n_keep:]
    state0 = jnp.zeros((b, RET_HEADS, RET_DIM, RET_DIM), F32) if ret_state is None else ret_state
    o_ret, s_new = _retention(rot.reshape(b, s, -1), 0, 1, plain3, 1, sgate.reshape(b, s, -1), state0, log_g,
                              s=s, c=min(256, s))
    o_mem = _mem_attend(plain3, 2, mem_k.reshape(b, N_MEM, MEM_W), mem_v.reshape(b, N_MEM, MEM_W),
                        s=s, tq=min(512, s))
    merged = _merge(o_swa.reshape(n, -1), o_ret.reshape(n, -1), o_mem.reshape(n, -1), wts["w_br"], gates,
                    tm=tm, tn=512)
    h1 = _mm(merged, wts["w_o"], epilogue="residual", out_dtype=F32, tm=tm, tn=1024, residual=x2, name="proj_out")

    if s >= 512:
        y, cog, cov = _ffn(h1.reshape(b, s, d), conv_buf, wts["g_ffn"], wts["w_up"], wts["w_conv"], wts["b_conv"],
                           wts["w_down"], g_final, tm=512, tf=512, td=512, seg=s, nseg=1)
    else:
        y, cog, cov = _ffn(h1.reshape(1, n, d), conv_buf, wts["g_ffn"], wts["w_up"], wts["w_conv"], wts["b_conv"],
                           wts["w_down"], g_final, tm=n, tf=512, td=512, seg=s, nseg=b)
    last2 = lambda t: t.reshape(-1, t.shape[0] * t.shape[1] // b, 8, D_FF)[:, -1, 6:8]
    new_buf = jnp.concatenate([last2(cog), last2(cov)], axis=-1)
    return (y.reshape(b, s, d), new_k.reshape(b, -1, SWA_KV_HEADS, SWA_HEAD_DIM),
            new_v.reshape(b, -1, SWA_KV_HEADS, SWA_HEAD_DIM), s_new, new_buf)


def _memory_kv(mem, g_mem, w_mem_kv):
    b, m, d = mem.shape
    u = _rmsnorm(mem.reshape(b * m, d), g_mem, min(512, b * m))
    kv = _mm(u, w_mem_kv.astype(BF16), epilogue="plain", out_dtype=F32, tm=min(1024, b * m), tn=1024, name="mem_kv")
    return (kv[:, :MEM_W].reshape(b, m, MEM_HEADS, MEM_HEAD_DIM), kv[:, MEM_W:].reshape(b, m, MEM_HEADS, MEM_HEAD_DIM))


def kernel(x_prompt, x_sample, mem_prompt, cache_swa_k, cache_swa_v, state_ret, state_ffn_conv, cache_mem_k, cache_mem_v, g_mix, w_in, b_gate, sink, w_br, w_o, g_mem, w_mem_kv, g_ffn, w_up, w_conv, b_conv, w_down, g_final):
    bp = x_prompt.shape[0]
    depth = w_in.shape[0]
    log_g = jnp.log1p(-jnp.exp2(-5.0 - jnp.arange(RET_HEADS, dtype=F32)))
    hp, hs = x_prompt, x_sample
    outs_p = [[] for _ in range(6)]
    outs_s = [[] for _ in range(4)]
    for l in range(depth):
        wts = _prep_weights(g_mix[l], w_in[l], b_gate[l], sink[l], w_br[l], w_o[l],
                            g_ffn[l], w_up[l], w_conv[l], b_conv[l], w_down[l])
        mk, mv = _memory_kv(mem_prompt, g_mem[l], w_mem_kv[l])
        zero_buf = jnp.zeros((bp, 2, 2 * D_FF), F32)
        assert depth == 1
        hp, kp, vp, sp, cp = _run_group(hp, 0, mk, mv, None, None, zero_buf, wts, g_final, log_g)
        hs, ksn, vsn, ssn, csn = _run_group(hs, PAST_LEN, cache_mem_k[l], cache_mem_v[l],
                                            (cache_swa_k[l], cache_swa_v[l]), state_ret[l],
                                            state_ffn_conv[l], wts, g_final, log_g)
        for lst, val in zip(outs_p, (kp, vp, sp, cp, mk, mv)):
            lst.append(val)
        for lst, val in zip(outs_s, (ksn, vsn, ssn, csn)):
            lst.append(val)
    return (hp, hs, *[jnp.stack(v) for v in outs_p], *[jnp.stack(v) for v in outs_s])
```

```python
import functools

import jax
import jax.numpy as jnp
from jax import lax
from jax.experimental import pallas as pl
from jax.experimental.pallas import tpu as pltpu

F32 = jnp.float32
BF16 = jnp.bfloat16

D_MODEL = 2048
CHUNK = 64
WINDOW = 128
SWA_HEADS = 16
SWA_KV_HEADS = 4
SWA_GROUP = SWA_HEADS // SWA_KV_HEADS
SWA_HEAD_DIM = 64
RET_HEADS = 8
RET_DIM = 128
RET_ROPE_BASE = 10000.0
N_MEM = 256
MEM_HEADS = 4
MEM_HEAD_DIM = 256
D_FF = 5632
N_BRANCH = 3
EPS = 1e-6
NEG = -1e30
PAST_LEN = 1024

SWA_Q = SWA_HEADS * SWA_HEAD_DIM
SWA_KV = SWA_KV_HEADS * SWA_HEAD_DIM
RET_W = RET_HEADS * RET_DIM
MEM_W = MEM_HEADS * MEM_HEAD_DIM
LANE = 128
CONV_ROWS = 32

def _sigmoid(x):
    return 1.0 / (1.0 + jnp.exp(-x))


def _rms(x, g):
    return x * lax.rsqrt(jnp.mean(x * x, axis=-1, keepdims=True) + EPS) * g


def _rmsnorm_kernel(x_ref, g_ref, o_ref):
    o_ref[...] = _rms(x_ref[...], g_ref[...]).astype(o_ref.dtype)


def _rmsnorm(x, g, tm):
    n, d = x.shape
    return pl.pallas_call(
        _rmsnorm_kernel,
        grid=(n // tm,),
        in_specs=[pl.BlockSpec((tm, d), lambda i: (i, 0)),
                  pl.BlockSpec((1, d), lambda i: (0, 0))],
        out_specs=pl.BlockSpec((tm, d), lambda i: (i, 0)),
        out_shape=jax.ShapeDtypeStruct((n, d), BF16),
        name="rmsnorm",
    )(x, g.reshape(1, d))


def _mm_kernel(x_ref, w_ref, *rest, epilogue, tn):
    o_ref = rest[-1]
    acc = jnp.dot(x_ref[...], w_ref[...], preferred_element_type=F32)
    if epilogue == "plain":
        out = acc
    elif epilogue == "scale":
        out = acc * rest[0][...]
    elif epilogue == "silu":
        out = acc * _sigmoid(acc)
    elif epilogue == "sigmoid_bias":
        out = _sigmoid(acc + rest[0][...])
    elif epilogue == "residual":
        out = rest[0][...] + acc
    elif epilogue == "rotary":
        cc, ss = rest[1][...], rest[2][...]
        pieces = []
        for j in range(tn // LANE):
            xh = acc[:, j * LANE:(j + 1) * LANE]
            pieces.append(xh * cc + pltpu.roll(xh, LANE // 2, 1) * ss)
        out = jnp.concatenate(pieces, axis=1) * rest[0][...]
    else:
        raise ValueError(epilogue)
    o_ref[...] = out.astype(o_ref.dtype)


def _mm(x, w, *, epilogue, out_dtype, tm, tn, colvec=None, residual=None, tables=None, name):
    n, k = x.shape
    ncols = w.shape[1]
    in_specs = [pl.BlockSpec((tm, k), lambda i, j: (i, 0)),
                pl.BlockSpec((k, tn), lambda i, j: (0, j))]
    args = [x, w]
    if colvec is not None:
        in_specs.append(pl.BlockSpec((1, tn), lambda i, j: (0, j)))
        args.append(colvec.reshape(1, ncols).astype(F32))
    if residual is not None:
        in_specs.append(pl.BlockSpec((tm, tn), lambda i, j: (i, j)))
        args.append(residual)
    if tables is not None:
        nblk = tables[0].shape[0] // tm
        for t in tables:
            in_specs.append(pl.BlockSpec((tm, LANE), lambda i, j: (i % nblk, 0)))
            args.append(t)
    return pl.pallas_call(
        functools.partial(_mm_kernel, epilogue=epilogue, tn=tn),
        grid=(n // tm, ncols // tn),
        in_specs=in_specs,
        out_specs=pl.BlockSpec((tm, tn), lambda i, j: (i, j)),
        out_shape=jax.ShapeDtypeStruct((n, ncols), out_dtype),
        compiler_params=pltpu.CompilerParams(dimension_semantics=("parallel", "parallel")),
        name=name,
    )(*args)


def _swa_kernel(sink_ref, q_ref, kp_ref, kc_ref, vp_ref, vc_ref, o_ref, *, qt, kv_off):
    i = pl.program_id(1)
    nq = qt // CHUNK
    w = (nq + 2) * CHUNK
    wp = -(-w // LANE) * LANE
    first_ok = jnp.where(i * qt + kv_off >= WINDOW, 0, 2)

    def window(p_ref, c_ref):
        parts = [p_ref[0], c_ref[0]]
        if wp > w:
            parts.append(jnp.zeros((wp - w, SWA_KV), F32))
        return jnp.concatenate(parts, axis=0)

    kwin = window(kp_ref, kc_ref)
    vwin = window(vp_ref, vc_ref)
    lane = lax.broadcasted_iota(jnp.int32, (wp, LANE), 1)
    lane_g = lax.broadcasted_iota(jnp.int32, (wp, SWA_KV), 1) // SWA_HEAD_DIM
    qc = lax.broadcasted_iota(jnp.int32, (qt, wp), 0) // CHUNK
    kc = lax.broadcasted_iota(jnp.int32, (qt, wp), 1) // CHUNK
    band = (kc >= qc) & (kc <= qc + 2) & (kc >= first_ok)
    out_g = lax.broadcasted_iota(jnp.int32, (qt, SWA_KV), 1) // SWA_HEAD_DIM

    def block_diag(win, h):
        col, half = divmod(h, 2)
        x = win[:, col * LANE:(col + 1) * LANE]
        xr = pltpu.roll(x, SWA_HEAD_DIM, 1)
        lo = lane < SWA_HEAD_DIM
        r = jnp.where(lo, x, xr) if half == 0 else jnp.where(lo, xr, x)
        r2 = jnp.concatenate([r, r], axis=1).astype(BF16)
        zero = jnp.zeros_like(r2)
        return jnp.concatenate([jnp.where(lane_g == g, r2, zero) for g in range(SWA_GROUP)], axis=0)

    for h in range(SWA_KV_HEADS):
        bk = block_diag(kwin, h)
        bv = block_diag(vwin, h)
        qh = q_ref[0, :, h * SWA_KV:(h + 1) * SWA_KV]
        s = lax.dot_general(qh, bk, (((1,), (1,)), ((), ())), preferred_element_type=F32)
        ps = []
        inv = jnp.zeros((qt, SWA_KV), F32)
        for g in range(SWA_GROUP):
            sg = jnp.where(band, s[:, g * wp:(g + 1) * wp], NEG)
            sk = sink_ref[h * SWA_GROUP + g]
            m = jnp.maximum(jnp.max(sg, axis=1, keepdims=True), sk)
            p = jnp.exp(sg - m)
            den = jnp.sum(p, axis=1, keepdims=True) + jnp.exp(sk - m)
            ps.append(p.astype(BF16))
            inv = jnp.where(out_g == g, 1.0 / den, inv)
        o = jnp.dot(jnp.concatenate(ps, axis=1), bv, preferred_element_type=F32)
        o_ref[0, :, h * SWA_KV:(h + 1) * SWA_KV] = (o * inv).astype(o_ref.dtype)


def _swa(q, q_col, k, k_col, v, v_col, sink, *, sq, qt, kv_off):
    b = q.shape[0]
    cur_off = kv_off // qt
    prev_off = kv_off // WINDOW - 1
    per = qt // WINDOW if qt >= WINDOW else 0

    def prev_map(kcol):
        return lambda bi, i, s: (bi, jnp.maximum(i * per + prev_off, 0), kcol)

    def cur_map(kcol):
        return lambda bi, i, s: (bi, i + cur_off, kcol)

    grid_spec = pltpu.PrefetchScalarGridSpec(
        num_scalar_prefetch=1,
        grid=(b, sq // qt),
        in_specs=[pl.BlockSpec((1, qt, SWA_Q), lambda bi, i, s: (bi, i, q_col)),
                  pl.BlockSpec((1, WINDOW, SWA_KV), prev_map(k_col)),
                  pl.BlockSpec((1, qt, SWA_KV), cur_map(k_col)),
                  pl.BlockSpec((1, WINDOW, SWA_KV), prev_map(v_col)),
                  pl.BlockSpec((1, qt, SWA_KV), cur_map(v_col))],
        out_specs=pl.BlockSpec((1, qt, SWA_Q), lambda bi, i, s: (bi, i, 0)),
    )
    return pl.pallas_call(
        functools.partial(_swa_kernel, qt=qt, kv_off=kv_off),
        grid_spec=grid_spec,
        out_shape=jax.ShapeDtypeStruct((b, sq, SWA_Q), BF16),
        compiler_params=pltpu.CompilerParams(dimension_semantics=("parallel", "parallel")),
        name="swa",
    )(sink.astype(F32), q, k, k, v, v)


def _ret_kernel(lg_ref, q_ref, k_ref, v_ref, g_ref, s0_ref, o_ref, st_ref, dec_ref, *, c):
    bi = pl.program_id(0)
    ci = pl.program_id(1)

    @pl.when((bi == 0) & (ci == 0))
    def _():
        n = lax.broadcasted_iota(jnp.int32, (c, c), 0)
        m = lax.broadcasted_iota(jnp.int32, (c, c), 1)
        diff = (n - m).astype(F32)
        for h in range(RET_HEADS):
            dec_ref[h] = jnp.where(diff >= 0, jnp.exp(lg_ref[h] * jnp.maximum(diff, 0.0)), 0.0)

    @pl.when(ci == 0)
    def _():
        st_ref[...] = s0_ref[...]

    nvec = lax.broadcasted_iota(jnp.int32, (c, 1), 0).astype(F32)
    for h in range(RET_HEADS):
        lg = lg_ref[h]
        sl = slice(h * RET_DIM, (h + 1) * RET_DIM)
        qh = q_ref[0, :, sl]
        kh = k_ref[0, :, sl]
        vh = v_ref[0, :, sl]
        state = st_ref[0, h]
        inner = lax.dot_general(qh, kh, (((1,), (1,)), ((), ())), preferred_element_type=F32) * dec_ref[h]
        o = jnp.dot(inner.astype(BF16), vh, preferred_element_type=F32)
        cross = jnp.dot(qh, state.astype(BF16), preferred_element_type=F32)
        o = o + cross * jnp.exp(lg * (nvec + 1.0))
        zeta = jnp.exp(lg * (float(c) - 1.0 - nvec))
        kz = (kh.astype(F32) * zeta).astype(BF16)
        upd = lax.dot_general(kz, vh, (((0,), (0,)), ((), ())), preferred_element_type=F32)
        decay_c = jnp.exp(lg * jnp.full((1, RET_DIM), float(c), F32))
        st_ref[0, h] = decay_c * state + upd
        on = o * lax.rsqrt(jnp.mean(o * o, axis=-1, keepdims=True) + EPS)
        o_ref[0, :, sl] = (g_ref[0, :, sl].astype(F32) * on).astype(o_ref.dtype)


def _retention(qk, q_col, k_col, v, v_col, gate, state0, log_g, *, s, c):
    b = qk.shape[0]

    def seq(col):
        return pl.BlockSpec((1, c, RET_W), lambda bi, ci, lg: (bi, ci, col))

    st_spec = pl.BlockSpec((1, RET_HEADS, RET_DIM, RET_DIM), lambda bi, ci, lg: (bi, 0, 0, 0))
    grid_spec = pltpu.PrefetchScalarGridSpec(
        num_scalar_prefetch=1,
        grid=(b, s // c),
        in_specs=[seq(q_col), seq(k_col), seq(v_col), seq(0), st_spec],
        out_specs=[seq(0), st_spec],
        scratch_shapes=[pltpu.VMEM((RET_HEADS, c, c), F32)],
    )
    return pl.pallas_call(
        functools.partial(_ret_kernel, c=c),
        grid_spec=grid_spec,
        out_shape=[jax.ShapeDtypeStruct((b, s, RET_W), BF16),
                   jax.ShapeDtypeStruct((b, RET_HEADS, RET_DIM, RET_DIM), F32)],
        compiler_params=pltpu.CompilerParams(dimension_semantics=("arbitrary", "arbitrary")),
        name="retention",
    )(log_g, qk, qk, v, gate, state0)


def _mem_kernel(q_ref, k_ref, v_ref, o_ref):
    for h in range(MEM_HEADS):
        sl = slice(h * MEM_HEAD_DIM, (h + 1) * MEM_HEAD_DIM)
        qh = q_ref[0, :, sl]
        kh = k_ref[0, :, sl].astype(BF16)
        vh = v_ref[0, :, sl].astype(BF16)
        s = lax.dot_general(qh, kh, (((1,), (1,)), ((), ())), preferred_element_type=F32)
        m = jnp.max(s, axis=1, keepdims=True)
        p = jnp.exp(s - m)
        den = jnp.sum(p, axis=1, keepdims=True)
        o = jnp.dot(p.astype(BF16), vh, preferred_element_type=F32)
        o_ref[0, :, sl] = (o * (1.0 / den)).astype(o_ref.dtype)


def _mem_attend(q, q_col, mk, mv, *, s, tq):
    b = q.shape[0]
    kv_spec = pl.BlockSpec((1, N_MEM, MEM_W), lambda bi, i: (bi, 0, 0))
    return pl.pallas_call(
        _mem_kernel,
        grid=(b, s // tq),
        in_specs=[pl.BlockSpec((1, tq, MEM_W), lambda bi, i: (bi, i, q_col)), kv_spec, kv_spec],
        out_specs=pl.BlockSpec((1, tq, MEM_W), lambda bi, i: (bi, i, 0)),
        out_shape=jax.ShapeDtypeStruct((b, s, MEM_W), BF16),
        compiler_params=pltpu.CompilerParams(dimension_semantics=("parallel", "parallel")),
        name="mem_attend",
    )(q, mk, mv)


def _merge_kernel(u_ref, oa_ref, ob_ref, oc_ref, wga_ref, wgb_ref, wgc_ref, ba_ref, bb_ref, bc_ref,
                  wa_ref, wb_ref, wc_ref, o_ref):
    u = u_ref[...]
    acc = None
    for o_r, wg_ref, b_ref, w_ref in ((oa_ref, wga_ref, ba_ref, wa_ref), (ob_ref, wgb_ref, bb_ref, wb_ref),
                                      (oc_ref, wgc_ref, bc_ref, wc_ref)):
        gate = _sigmoid(jnp.dot(u, wg_ref[...], preferred_element_type=F32) + b_ref[...])
        term = gate * jnp.dot(o_r[...], w_ref[...], preferred_element_type=F32)
        acc = term if acc is None else acc + term
    o_ref[...] = acc.astype(o_ref.dtype)


def _merge(u, o_swa, o_ret, o_mem, w_gate, b_gate, w_br, *, tm, tn):
    n, d = u.shape
    kb = o_swa.shape[1]
    nj = D_MODEL // tn
    o_spec = pl.BlockSpec((tm, kb), lambda i, j: (i, 0))

    def wg_spec(br):
        return pl.BlockSpec((d, tn), lambda i, j: (0, br * nj + j))

    def b_spec(br):
        return pl.BlockSpec((1, tn), lambda i, j: (0, br * nj + j))

    def w_spec(br):
        return pl.BlockSpec((kb, tn), lambda i, j: (br, j))

    bias = b_gate.reshape(1, -1).astype(F32)
    return pl.pallas_call(
        _merge_kernel,
        grid=(n // tm, nj),
        in_specs=[pl.BlockSpec((tm, d), lambda i, j: (i, 0)), o_spec, o_spec, o_spec,
                  wg_spec(0), wg_spec(1), wg_spec(2), b_spec(0), b_spec(1), b_spec(2),
                  w_spec(0), w_spec(1), w_spec(2)],
        out_specs=pl.BlockSpec((tm, tn), lambda i, j: (i, j)),
        out_shape=jax.ShapeDtypeStruct((n, D_MODEL), BF16),
        compiler_params=pltpu.CompilerParams(dimension_semantics=("parallel", "parallel")),
        name="merge",
    )(u, o_swa, o_ret, o_mem, w_gate, w_gate, w_gate, bias, bias, bias, w_br, w_br, w_br)


def _ffn_kernel(h_ref, hj_ref, gf_ref, wug_ref, wuv_ref, wcg_ref, wcv_ref, bg_ref, bv_ref, wd_ref,
                cbg_ref, cbv_ref, gfin_ref, y_ref, cog_ref, cov_ref,
                u_sc, act_sc, ag_sc, av_sc, y_sc, cg_sc, cv_sc, *, tm, tf, td, nf, seg, nseg):
    i = pl.program_id(1)
    t = pl.program_id(2)
    nd = D_MODEL // td

    seg_t = tm // nseg
    stride = seg_t + 8

    def up_proj():
        u = u_sc[...]
        for a_sc, wu_ref in ((ag_sc, wug_ref), (av_sc, wuv_ref)):
            a = jnp.dot(u, wu_ref[...], preferred_element_type=F32)
            for s in range(nseg):
                a_sc[s * stride + 8:(s + 1) * stride, :] = a[s * seg_t:(s + 1) * seg_t]

    def conv_act():
        fb = t - 1
        for a_sc, cb_ref, carry_sc, co_ref in ((ag_sc, cbg_ref, cg_sc, cog_ref), (av_sc, cbv_ref, cv_sc, cov_ref)):
            if nseg == 1:
                @pl.when(i == 0)
                def _():
                    a_sc[6:8, :] = cb_ref[0]

                @pl.when(i > 0)
                def _():
                    a_sc[6:8, :] = carry_sc[fb, 6:8, :]

                carry_sc[fb] = a_sc[tm:tm + 8, :]
            else:
                for s in range(nseg):
                    a_sc[s * stride + 6:s * stride + 8, :] = cb_ref[s]
            for s in range(nseg):
                co_ref[0, s] = a_sc[(s + 1) * stride - 8:(s + 1) * stride, :]

        def conv(a_sc, wc_ref, b_ref, r0, c0):
            tap = lambda k: a_sc[r0 - k:r0 - k + CONV_ROWS, c0:c0 + LANE]
            return b_ref[:, c0:c0 + LANE] + (wc_ref[0:1, c0:c0 + LANE] * tap(2) + wc_ref[1:2, c0:c0 + LANE] * tap(1)
                                             + wc_ref[2:3, c0:c0 + LANE] * tap(0))

        for c0 in range(0, tf, LANE):
            for s in range(nseg):
                for r in range(0, seg_t, CONV_ROWS):
                    cg = conv(ag_sc, wcg_ref, bg_ref, s * stride + 8 + r, c0)
                    cv = conv(av_sc, wcv_ref, bv_ref, s * stride + 8 + r, c0)
                    act_sc[fb, s * seg_t + r:s * seg_t + r + CONV_ROWS, c0:c0 + LANE] = (
                        cg * _sigmoid(cg) * cv).astype(BF16)

    @pl.when(t == 0)
    def _():
        u_sc[...] = _rms(h_ref[0], gf_ref[...]).astype(BF16)
        up_proj()

    @pl.when((t >= 1) & (t < nf))
    def _():
        conv_act()
        up_proj()

    def down(j):
        act = jnp.concatenate([act_sc[f] for f in range(nf)], axis=1)
        y_sc[j] = hj_ref[0] + jnp.dot(act, wd_ref[...], preferred_element_type=F32)

    @pl.when(t == nf)
    def _():
        conv_act()
        down(0)

    @pl.when(t > nf)
    def _():
        down(t - nf)

    @pl.when(t == nf + nd - 1)
    def _():
        full = jnp.concatenate([y_sc[k] for k in range(nd)], axis=1)
        y_ref[0] = _rms(full, gfin_ref[...])


def _ffn(h, conv_buf, g_ffn, w_up, w_conv, b_conv, w_down, g_final, *, tm, tf, td, seg, nseg):
    bt, rows, d = h.shape
    nf = D_FF // tf
    nd = d // td
    nt = rows // tm
    up_blk = lambda t: jnp.minimum(t, nf - 1)
    conv_blk = lambda t: jnp.clip(t - 1, 0, nf - 1)
    down_blk = lambda t: jnp.clip(t - nf, 0, nd - 1)
    in_specs = [
        pl.BlockSpec((1, tm, d), lambda b, i, t: (b, i, 0)),
        pl.BlockSpec((1, tm, td), lambda b, i, t: (b, i, down_blk(t))),
        pl.BlockSpec((1, d), lambda b, i, t: (0, 0)),
        pl.BlockSpec((d, tf), lambda b, i, t: (0, up_blk(t))),
        pl.BlockSpec((d, tf), lambda b, i, t: (0, nf + up_blk(t))),
        pl.BlockSpec((3, tf), lambda b, i, t: (0, conv_blk(t))),
        pl.BlockSpec((3, tf), lambda b, i, t: (0, nf + conv_blk(t))),
        pl.BlockSpec((1, tf), lambda b, i, t: (0, conv_blk(t))),
        pl.BlockSpec((1, tf), lambda b, i, t: (0, nf + conv_blk(t))),
        pl.BlockSpec((D_FF, td), lambda b, i, t: (0, down_blk(t))),
        pl.BlockSpec((nseg, 2, tf), lambda b, i, t: (b, 0, conv_blk(t))),
        pl.BlockSpec((nseg, 2, tf), lambda b, i, t: (b, 0, nf + conv_blk(t))),
        pl.BlockSpec((1, d), lambda b, i, t: (0, 0)),
    ]
    out_specs = [
        pl.BlockSpec((1, tm, d), lambda b, i, t: (b, i, 0)),
        pl.BlockSpec((1, nseg, 8, tf), lambda b, i, t: (b * nt + i, 0, 0, conv_blk(t))),
        pl.BlockSpec((1, nseg, 8, tf), lambda b, i, t: (b * nt + i, 0, 0, conv_blk(t))),
    ]
    return pl.pallas_call(
        functools.partial(_ffn_kernel, tm=tm, tf=tf, td=td, nf=nf, seg=seg, nseg=nseg),
        grid=(bt, nt, nf + nd),
        in_specs=in_specs,
        out_specs=out_specs,
        out_shape=[jax.ShapeDtypeStruct((bt, rows, d), F32),
                   jax.ShapeDtypeStruct((bt * nt, nseg, 8, D_FF), F32),
                   jax.ShapeDtypeStruct((bt * nt, nseg, 8, D_FF), F32)],
        scratch_shapes=[pltpu.VMEM((tm, d), BF16), pltpu.VMEM((nf, tm, tf), BF16),
                        pltpu.VMEM((tm + 8 * nseg, tf), F32), pltpu.VMEM((tm + 8 * nseg, tf), F32),
                        pltpu.VMEM((nd, tm, td), F32),
                        pltpu.VMEM((nf, 8, tf), F32), pltpu.VMEM((nf, 8, tf), F32)],
        compiler_params=pltpu.CompilerParams(dimension_semantics=("arbitrary", "arbitrary", "arbitrary")),
        name="conv_ffn",
    )(h, h, g_ffn.reshape(1, d), w_up, w_up, w_conv, w_conv, b_conv.reshape(1, -1), b_conv.reshape(1, -1),
      w_down, conv_buf, conv_buf, g_final.reshape(1, d))


def _rope_tables(pos0, s):
    half = RET_DIM // 2
    inv_freq = 1.0 / (RET_ROPE_BASE ** jnp.linspace(0.0, 1.0, half, dtype=F32))
    ang = (pos0 + jnp.arange(s)).astype(F32)[:, None] * inv_freq[None, :]
    cos, sin = jnp.cos(ang), jnp.sin(ang)
    return jnp.concatenate([cos, cos], axis=1), jnp.concatenate([-sin, sin], axis=1)


def _prep_weights(g_mix, w_in, b_gate, sink, w_br, w_o, g_ffn, w_up, w_conv, b_conv, w_down):
    o = 0
    cuts = {}
    for name, width in (("qa", SWA_Q), ("ka", SWA_KV), ("va", SWA_KV), ("qr", RET_W), ("kr", RET_W),
                        ("vr", RET_W), ("gr", RET_W), ("qm", MEM_W), ("gl", N_BRANCH * D_MODEL)):
        cuts[name] = (o, o + width)
        o += width
    cols = lambda *names: jnp.concatenate([w_in[:, cuts[n][0]:cuts[n][1]] for n in names], axis=1).astype(BF16)
    ones = lambda n, v: jnp.full((n,), v, F32)
    return dict(
        g_mix=g_mix, b_gate=b_gate, sink=sink, g_ffn=g_ffn, w_conv=w_conv, b_conv=b_conv,
        w_plain=cols("qa", "vr", "qm"),
        s_plain=jnp.concatenate([ones(SWA_Q, SWA_HEAD_DIM ** -0.5), ones(RET_W, 1.0),
                                 ones(MEM_W, MEM_HEAD_DIM ** -0.5)]),
        w_kv=cols("ka", "va"),
        w_rot=cols("qr", "kr"),
        s_rot=jnp.concatenate([ones(RET_W, 1.0), ones(RET_W, RET_DIM ** -0.5)]),
        w_g=cols("gr"),
        w_gate=cols("gl"),
        w_br=w_br.astype(BF16), w_o=w_o.astype(BF16), w_up=w_up.astype(BF16), w_down=w_down.astype(BF16),
    )


def _run_group(x, pos0, mem_k, mem_v, swa_cache, ret_state, conv_buf, wts, g_final, log_g):
    b, s, d = x.shape
    n = b * s
    tm = min(1024, n)
    x2 = x.reshape(n, d)
    u = _rmsnorm(x2, wts["g_mix"], min(512, n))

    plain = _mm(u, wts["w_plain"], epilogue="scale", out_dtype=BF16, tm=tm, tn=1024,
                colvec=wts["s_plain"], name="proj_plain")
    kv = _mm(u, wts["w_kv"], epilogue="plain", out_dtype=F32, tm=tm, tn=2 * SWA_KV, name="proj_kv")
    cc, ss = _rope_tables(pos0, s)
    if s < tm:
        cc, ss = jnp.tile(cc, (tm // s, 1)), jnp.tile(ss, (tm // s, 1))
    rot = _mm(u, wts["w_rot"], epilogue="rotary", out_dtype=BF16, tm=tm, tn=1024,
              colvec=wts["s_rot"], tables=(cc, ss), name="proj_rot")
    sgate = _mm(u, wts["w_g"], epilogue="silu", out_dtype=BF16, tm=tm, tn=1024, name="proj_silu")

    plain3 = plain.reshape(b, s, -1)
    kv3 = kv.reshape(b, s, 2 * SWA_KV)
    if swa_cache is None:
        o_swa = _swa(plain3, 0, kv3, 0, kv3, 1, wts["sink"], sq=s, qt=min(256, s), kv_off=0)
        new_k = kv3[:, s - WINDOW:, :SWA_KV]
        new_v = kv3[:, s - WINDOW:, SWA_KV:]
    else:
        ck = swa_cache[0].reshape(b, -1, SWA_KV)
        cv = swa_cache[1].reshape(b, -1, SWA_KV)
        n_keep = ck.shape[1]
        k_all = jnp.concatenate([ck, kv3[:, :, :SWA_KV]], axis=1)
        v_all = jnp.concatenate([cv, kv3[:, :, SWA_KV:]], axis=1)
        o_swa = _swa(plain3, 0, k_all, 0, v_all, 0, wts["sink"], sq=s, qt=s, kv_off=n_keep)
        new_k = k_all[:, -n_keep:]
        new_v = v_all[:, -n_keep:]
    state0 = jnp.zeros((b, RET_HEADS, RET_DIM, RET_DIM), F32) if ret_state is None else ret_state
    o_ret, s_new = _retention(rot.reshape(b, s, -1), 0, 1, plain3, 1, sgate.reshape(b, s, -1), state0, log_g,
                              s=s, c=min(256, s))
    o_mem = _mem_attend(plain3, 2, mem_k.reshape(b, N_MEM, MEM_W), mem_v.reshape(b, N_MEM, MEM_W),
                        s=s, tq=min(512, s))
    merged = _merge(u, o_swa.reshape(n, -1), o_ret.reshape(n, -1), o_mem.reshape(n, -1), wts["w_gate"],
                    wts["b_gate"], wts["w_br"], tm=tm, tn=512)
    h1 = _mm(merged, wts["w_o"], epilogue="residual", out_dtype=F32, tm=min(512, n), tn=d, residual=x2,
             name="proj_out")

    if s >= 512:
        y, cog, cov = _ffn(h1.reshape(b, s, d), conv_buf, wts["g_ffn"], wts["w_up"], wts["w_conv"], wts["b_conv"],
                           wts["w_down"], g_final, tm=512, tf=512, td=512, seg=s, nseg=1)
    else:
        y, cog, cov = _ffn(h1.reshape(1, n, d), conv_buf, wts["g_ffn"], wts["w_up"], wts["w_conv"], wts["b_conv"],
                           wts["w_down"], g_final, tm=n, tf=512, td=512, seg=s, nseg=b)
    last2 = lambda t: t.reshape(-1, t.shape[0] * t.shape[1] // b, 8, D_FF)[:, -1, 6:8]
    new_buf = jnp.concatenate([last2(cog), last2(cov)], axis=-1)
    return (y.reshape(b, s, d), new_k.reshape(b, -1, SWA_KV_HEADS, SWA_HEAD_DIM),
            new_v.reshape(b, -1, SWA_KV_HEADS, SWA_HEAD_DIM), s_new, new_buf)


def _memory_kv(mem, g_mem, w_mem_kv):
    b, m, d = mem.shape
    u = _rmsnorm(mem.reshape(b * m, d), g_mem, min(512, b * m))
    kv = _mm(u, w_mem_kv.astype(BF16), epilogue="plain", out_dtype=F32, tm=min(1024, b * m), tn=1024, name="mem_kv")
    return (kv[:, :MEM_W].reshape(b, m, MEM_HEADS, MEM_HEAD_DIM), kv[:, MEM_W:].reshape(b, m, MEM_HEADS, MEM_HEAD_DIM))


def kernel(x_prompt, x_sample, mem_prompt, cache_swa_k, cache_swa_v, state_ret, state_ffn_conv, cache_mem_k, cache_mem_v, g_mix, w_in, b_gate, sink, w_br, w_o, g_mem, w_mem_kv, g_ffn, w_up, w_conv, b_conv, w_down, g_final):
    bp = x_prompt.shape[0]
    depth = w_in.shape[0]
    log_g = jnp.log1p(-jnp.exp2(-5.0 - jnp.arange(RET_HEADS, dtype=F32)))
    hp, hs = x_prompt, x_sample
    outs_p = [[] for _ in range(6)]
    outs_s = [[] for _ in range(4)]
    for l in range(depth):
        wts = _prep_weights(g_mix[l], w_in[l], b_gate[l], sink[l], w_br[l], w_o[l],
                            g_ffn[l], w_up[l], w_conv[l], b_conv[l], w_down[l])
        mk, mv = _memory_kv(mem_prompt, g_mem[l], w_mem_kv[l])
        zero_buf = jnp.zeros((bp, 2, 2 * D_FF), F32)
        assert depth == 1
        hp, kp, vp, sp, cp = _run_group(hp, 0, mk, mv, None, None, zero_buf, wts, g_final, log_g)
        hs, ksn, vsn, ssn, csn = _run_group(hs, PAST_LEN, cache_mem_k[l], cache_mem_v[l],
                                            (cache_swa_k[l], cache_swa_v[l]), state_ret[l],
                                            state_ffn_conv[l], wts, g_final, log_g)
        for lst, val in zip(outs_p, (kp, vp, sp, cp, mk, mv)):
            lst.append(val)
        for lst, val in zip(outs_s, (ksn, vsn, ssn, csn)):
            lst.append(val)
    return (hp, hs, *[jnp.stack(v) for v in outs_p], *[jnp.stack(v) for v in outs_s])
```

```python
import functools

import jax
import jax.numpy as jnp
from jax import lax
from jax.experimental import pallas as pl
from jax.experimental.pallas import tpu as pltpu

F32 = jnp.float32
BF16 = jnp.bfloat16

D_MODEL = 2048
CHUNK = 64
WINDOW = 128
SWA_HEADS = 16
SWA_KV_HEADS = 4
SWA_GROUP = SWA_HEADS // SWA_KV_HEADS
SWA_HEAD_DIM = 64
RET_HEADS = 8
RET_DIM = 128
RET_ROPE_BASE = 10000.0
N_MEM = 256
MEM_HEADS = 4
MEM_HEAD_DIM = 256
D_FF = 5632
N_BRANCH = 3
EPS = 1e-6
NEG = -1e30
PAST_LEN = 1024

SWA_Q = SWA_HEADS * SWA_HEAD_DIM
SWA_KV = SWA_KV_HEADS * SWA_HEAD_DIM
RET_W = RET_HEADS * RET_DIM
MEM_W = MEM_HEADS * MEM_HEAD_DIM
LANE = 128
CONV_ROWS = 32

def _sigmoid(x):
    return 1.0 / (1.0 + jnp.exp(-x))


def _rms(x, g):
    return x * lax.rsqrt(jnp.mean(x * x, axis=-1, keepdims=True) + EPS) * g


def _rmsnorm_kernel(x_ref, g_ref, o_ref):
    o_ref[...] = _rms(x_ref[...], g_ref[...]).astype(o_ref.dtype)


def _rmsnorm(x, g, tm):
    n, d = x.shape
    return pl.pallas_call(
        _rmsnorm_kernel,
        grid=(n // tm,),
        in_specs=[pl.BlockSpec((tm, d), lambda i: (i, 0)),
                  pl.BlockSpec((1, d), lambda i: (0, 0))],
        out_specs=pl.BlockSpec((tm, d), lambda i: (i, 0)),
        out_shape=jax.ShapeDtypeStruct((n, d), BF16),
        name="rmsnorm",
    )(x, g.reshape(1, d))


def _mm_kernel(x_ref, w_ref, *rest, epilogue, tn):
    o_ref = rest[-1]
    acc = jnp.dot(x_ref[...], w_ref[...], preferred_element_type=F32)
    if epilogue == "plain":
        out = acc
    elif epilogue == "scale":
        out = acc * rest[0][...]
    elif epilogue == "silu":
        out = acc * _sigmoid(acc)
    elif epilogue == "sigmoid_bias":
        out = _sigmoid(acc + rest[0][...])
    elif epilogue == "residual":
        out = rest[0][...] + acc
    elif epilogue == "rotary":
        cc, ss = rest[1][...], rest[2][...]
        pieces = []
        for j in range(tn // LANE):
            xh = acc[:, j * LANE:(j + 1) * LANE]
            pieces.append(xh * cc + pltpu.roll(xh, LANE // 2, 1) * ss)
        out = jnp.concatenate(pieces, axis=1) * rest[0][...]
    else:
        raise ValueError(epilogue)
    o_ref[...] = out.astype(o_ref.dtype)


def _mm(x, w, *, epilogue, out_dtype, tm, tn, colvec=None, residual=None, tables=None, name):
    n, k = x.shape
    ncols = w.shape[1]
    in_specs = [pl.BlockSpec((tm, k), lambda i, j: (i, 0)),
                pl.BlockSpec((k, tn), lambda i, j: (0, j))]
    args = [x, w]
    if colvec is not None:
        in_specs.append(pl.BlockSpec((1, tn), lambda i, j: (0, j)))
        args.append(colvec.reshape(1, ncols).astype(F32))
    if residual is not None:
        in_specs.append(pl.BlockSpec((tm, tn), lambda i, j: (i, j)))
        args.append(residual)
    if tables is not None:
        nblk = tables[0].shape[0] // tm
        for t in tables:
            in_specs.append(pl.BlockSpec((tm, LANE), lambda i, j: (i % nblk, 0)))
            args.append(t)
    return pl.pallas_call(
        functools.partial(_mm_kernel, epilogue=epilogue, tn=tn),
        grid=(n // tm, ncols // tn),
        in_specs=in_specs,
        out_specs=pl.BlockSpec((tm, tn), lambda i, j: (i, j)),
        out_shape=jax.ShapeDtypeStruct((n, ncols), out_dtype),
        compiler_params=pltpu.CompilerParams(dimension_semantics=("parallel", "parallel")),
        name=name,
    )(*args)


def _norm_mm_kernel(x_ref, g_ref, w_ref, cv_ref, o_ref, u_ref):
    @pl.when(pl.program_id(1) == 0)
    def _():
        u_ref[...] = _rms(x_ref[...], g_ref[...]).astype(u_ref.dtype)

    acc = jnp.dot(u_ref[...], w_ref[...], preferred_element_type=F32)
    o_ref[...] = (acc * cv_ref[...]).astype(o_ref.dtype)


def _norm_mm(x, g, w, colvec, *, tm, tn, name):
    n, k = x.shape
    ncols = w.shape[1]
    return pl.pallas_call(
        _norm_mm_kernel,
        grid=(n // tm, ncols // tn),
        in_specs=[pl.BlockSpec((tm, k), lambda i, j: (i, 0)),
                  pl.BlockSpec((1, k), lambda i, j: (0, 0)),
                  pl.BlockSpec((k, tn), lambda i, j: (0, j)),
                  pl.BlockSpec((1, tn), lambda i, j: (0, j))],
        out_specs=[pl.BlockSpec((tm, tn), lambda i, j: (i, j)),
                   pl.BlockSpec((tm, k), lambda i, j: (i, 0))],
        out_shape=[jax.ShapeDtypeStruct((n, ncols), BF16), jax.ShapeDtypeStruct((n, k), BF16)],
        compiler_params=pltpu.CompilerParams(dimension_semantics=("parallel", "arbitrary")),
        name=name,
    )(x, g.reshape(1, k), w, colvec.reshape(1, ncols).astype(F32))


def _swa_kernel(sink_ref, q_ref, kp_ref, kc_ref, vp_ref, vc_ref, o_ref, *, qt, kv_off):
    i = pl.program_id(1)
    nq = qt // CHUNK
    w = (nq + 2) * CHUNK
    wp = -(-w // LANE) * LANE
    first_ok = jnp.where(i * qt + kv_off >= WINDOW, 0, 2)

    def window(p_ref, c_ref):
        parts = [p_ref[0], c_ref[0]]
        if wp > w:
            parts.append(jnp.zeros((wp - w, SWA_KV), F32))
        return jnp.concatenate(parts, axis=0)

    kwin = window(kp_ref, kc_ref)
    vwin = window(vp_ref, vc_ref)
    lane = lax.broadcasted_iota(jnp.int32, (wp, LANE), 1)
    lane_g = lax.broadcasted_iota(jnp.int32, (wp, SWA_KV), 1) // SWA_HEAD_DIM
    qc = lax.broadcasted_iota(jnp.int32, (qt, wp), 0) // CHUNK
    kc = lax.broadcasted_iota(jnp.int32, (qt, wp), 1) // CHUNK
    band = (kc >= qc) & (kc <= qc + 2) & (kc >= first_ok)
    out_g = lax.broadcasted_iota(jnp.int32, (qt, SWA_KV), 1) // SWA_HEAD_DIM

    def block_diag(win, h):
        col, half = divmod(h, 2)
        x = win[:, col * LANE:(col + 1) * LANE]
        xr = pltpu.roll(x, SWA_HEAD_DIM, 1)
        lo = lane < SWA_HEAD_DIM
        r = jnp.where(lo, x, xr) if half == 0 else jnp.where(lo, xr, x)
        r2 = jnp.concatenate([r, r], axis=1).astype(BF16)
        zero = jnp.zeros_like(r2)
        return jnp.concatenate([jnp.where(lane_g == g, r2, zero) for g in range(SWA_GROUP)], axis=0)

    for h in range(SWA_KV_HEADS):
        bk = block_diag(kwin, h)
        bv = block_diag(vwin, h)
        qh = q_ref[0, :, h * SWA_KV:(h + 1) * SWA_KV]
        s = lax.dot_general(qh, bk, (((1,), (1,)), ((), ())), preferred_element_type=F32)
        ps = []
        inv = jnp.zeros((qt, SWA_KV), F32)
        for g in range(SWA_GROUP):
            sg = jnp.where(band, s[:, g * wp:(g + 1) * wp], NEG)
            sk = sink_ref[h * SWA_GROUP + g]
            m = jnp.maximum(jnp.max(sg, axis=1, keepdims=True), sk)
            p = jnp.exp(sg - m)
            den = jnp.sum(p, axis=1, keepdims=True) + jnp.exp(sk - m)
            ps.append(p.astype(BF16))
            inv = jnp.where(out_g == g, 1.0 / den, inv)
        o = jnp.dot(jnp.concatenate(ps, axis=1), bv, preferred_element_type=F32)
        o_ref[0, :, h * SWA_KV:(h + 1) * SWA_KV] = (o * inv).astype(o_ref.dtype)


def _swa(q, q_col, k, k_col, v, v_col, sink, *, sq, qt, kv_off):
    b = q.shape[0]
    cur_off = kv_off // qt
    prev_off = kv_off // WINDOW - 1
    per = qt // WINDOW if qt >= WINDOW else 0

    def prev_map(kcol):
        return lambda bi, i, s: (bi, jnp.maximum(i * per + prev_off, 0), kcol)

    def cur_map(kcol):
        return lambda bi, i, s: (bi, i + cur_off, kcol)

    grid_spec = pltpu.PrefetchScalarGridSpec(
        num_scalar_prefetch=1,
        grid=(b, sq // qt),
        in_specs=[pl.BlockSpec((1, qt, SWA_Q), lambda bi, i, s: (bi, i, q_col)),
                  pl.BlockSpec((1, WINDOW, SWA_KV), prev_map(k_col)),
                  pl.BlockSpec((1, qt, SWA_KV), cur_map(k_col)),
                  pl.BlockSpec((1, WINDOW, SWA_KV), prev_map(v_col)),
                  pl.BlockSpec((1, qt, SWA_KV), cur_map(v_col))],
        out_specs=pl.BlockSpec((1, qt, SWA_Q), lambda bi, i, s: (bi, i, 0)),
    )
    return pl.pallas_call(
        functools.partial(_swa_kernel, qt=qt, kv_off=kv_off),
        grid_spec=grid_spec,
        out_shape=jax.ShapeDtypeStruct((b, sq, SWA_Q), BF16),
        compiler_params=pltpu.CompilerParams(dimension_semantics=("parallel", "parallel")),
        name="swa",
    )(sink.astype(F32), q, k, k, v, v)


def _ret_kernel(lg_ref, q_ref, k_ref, v_ref, g_ref, s0_ref, o_ref, st_ref, dec_ref, *, c):
    bi = pl.program_id(0)
    ci = pl.program_id(1)

    @pl.when((bi == 0) & (ci == 0))
    def _():
        n = lax.broadcasted_iota(jnp.int32, (c, c), 0)
        m = lax.broadcasted_iota(jnp.int32, (c, c), 1)
        diff = (n - m).astype(F32)
        for h in range(RET_HEADS):
            dec_ref[h] = jnp.where(diff >= 0, jnp.exp(lg_ref[h] * jnp.maximum(diff, 0.0)), 0.0)

    @pl.when(ci == 0)
    def _():
        st_ref[...] = s0_ref[...]

    nvec = lax.broadcasted_iota(jnp.int32, (c, 1), 0).astype(F32)
    for h in range(RET_HEADS):
        lg = lg_ref[h]
        sl = slice(h * RET_DIM, (h + 1) * RET_DIM)
        qh = q_ref[0, :, sl]
        kh = k_ref[0, :, sl]
        vh = v_ref[0, :, sl]
        state = st_ref[0, h]
        inner = lax.dot_general(qh, kh, (((1,), (1,)), ((), ())), preferred_element_type=F32) * dec_ref[h]
        o = jnp.dot(inner.astype(BF16), vh, preferred_element_type=F32)
        cross = jnp.dot(qh, state.astype(BF16), preferred_element_type=F32)
        o = o + cross * jnp.exp(lg * (nvec + 1.0))
        zeta = jnp.exp(lg * (float(c) - 1.0 - nvec))
        kz = (kh.astype(F32) * zeta).astype(BF16)
        upd = lax.dot_general(kz, vh, (((0,), (0,)), ((), ())), preferred_element_type=F32)
        decay_c = jnp.exp(lg * jnp.full((1, RET_DIM), float(c), F32))
        st_ref[0, h] = decay_c * state + upd
        on = o * lax.rsqrt(jnp.mean(o * o, axis=-1, keepdims=True) + EPS)
        o_ref[0, :, sl] = (g_ref[0, :, sl].astype(F32) * on).astype(o_ref.dtype)


def _retention(qk, q_col, k_col, v, v_col, gate, state0, log_g, *, s, c):
    b = qk.shape[0]

    def seq(col):
        return pl.BlockSpec((1, c, RET_W), lambda bi, ci, lg: (bi, ci, col))

    st_spec = pl.BlockSpec((1, RET_HEADS, RET_DIM, RET_DIM), lambda bi, ci, lg: (bi, 0, 0, 0))
    grid_spec = pltpu.PrefetchScalarGridSpec(
        num_scalar_prefetch=1,
        grid=(b, s // c),
        in_specs=[seq(q_col), seq(k_col), seq(v_col), seq(0), st_spec],
        out_specs=[seq(0), st_spec],
        scratch_shapes=[pltpu.VMEM((RET_HEADS, c, c), F32)],
    )
    return pl.pallas_call(
        functools.partial(_ret_kernel, c=c),
        grid_spec=grid_spec,
        out_shape=[jax.ShapeDtypeStruct((b, s, RET_W), BF16),
                   jax.ShapeDtypeStruct((b, RET_HEADS, RET_DIM, RET_DIM), F32)],
        compiler_params=pltpu.CompilerParams(dimension_semantics=("arbitrary", "arbitrary")),
        name="retention",
    )(log_g, qk, qk, v, gate, state0)


def _mem_kernel(q_ref, k_ref, v_ref, o_ref):
    for h in range(MEM_HEADS):
        sl = slice(h * MEM_HEAD_DIM, (h + 1) * MEM_HEAD_DIM)
        qh = q_ref[0, :, sl]
        kh = k_ref[0, :, sl].astype(BF16)
        vh = v_ref[0, :, sl].astype(BF16)
        s = lax.dot_general(qh, kh, (((1,), (1,)), ((), ())), preferred_element_type=F32)
        m = jnp.max(s, axis=1, keepdims=True)
        p = jnp.exp(s - m)
        den = jnp.sum(p, axis=1, keepdims=True)
        o = jnp.dot(p.astype(BF16), vh, preferred_element_type=F32)
        o_ref[0, :, sl] = (o * (1.0 / den)).astype(o_ref.dtype)


def _mem_attend(q, q_col, mk, mv, *, s, tq):
    b = q.shape[0]
    kv_spec = pl.BlockSpec((1, N_MEM, MEM_W), lambda bi, i: (bi, 0, 0))
    return pl.pallas_call(
        _mem_kernel,
        grid=(b, s // tq),
        in_specs=[pl.BlockSpec((1, tq, MEM_W), lambda bi, i: (bi, i, q_col)), kv_spec, kv_spec],
        out_specs=pl.BlockSpec((1, tq, MEM_W), lambda bi, i: (bi, i, 0)),
        out_shape=jax.ShapeDtypeStruct((b, s, MEM_W), BF16),
        compiler_params=pltpu.CompilerParams(dimension_semantics=("parallel", "parallel")),
        name="mem_attend",
    )(q, mk, mv)


def _merge_kernel(u_ref, oa_ref, ob_ref, oc_ref, wga_ref, wgb_ref, wgc_ref, ba_ref, bb_ref, bc_ref,
                  wa_ref, wb_ref, wc_ref, o_ref):
    u = u_ref[...]
    acc = None
    for o_r, wg_ref, b_ref, w_ref in ((oa_ref, wga_ref, ba_ref, wa_ref), (ob_ref, wgb_ref, bb_ref, wb_ref),
                                      (oc_ref, wgc_ref, bc_ref, wc_ref)):
        gate = _sigmoid(jnp.dot(u, wg_ref[...], preferred_element_type=F32) + b_ref[...])
        term = gate * jnp.dot(o_r[...], w_ref[...], preferred_element_type=F32)
        acc = term if acc is None else acc + term
    o_ref[...] = acc.astype(o_ref.dtype)


def _merge(u, o_swa, o_ret, o_mem, w_gate, b_gate, w_br, *, tm, tn):
    n, d = u.shape
    kb = o_swa.shape[1]
    nj = D_MODEL // tn
    o_spec = pl.BlockSpec((tm, kb), lambda i, j: (i, 0))

    def wg_spec(br):
        return pl.BlockSpec((d, tn), lambda i, j: (0, br * nj + j))

    def b_spec(br):
        return pl.BlockSpec((1, tn), lambda i, j: (0, br * nj + j))

    def w_spec(br):
        return pl.BlockSpec((kb, tn), lambda i, j: (br, j))

    bias = b_gate.reshape(1, -1).astype(F32)
    return pl.pallas_call(
        _merge_kernel,
        grid=(n // tm, nj),
        in_specs=[pl.BlockSpec((tm, d), lambda i, j: (i, 0)), o_spec, o_spec, o_spec,
                  wg_spec(0), wg_spec(1), wg_spec(2), b_spec(0), b_spec(1), b_spec(2),
                  w_spec(0), w_spec(1), w_spec(2)],
        out_specs=pl.BlockSpec((tm, tn), lambda i, j: (i, j)),
        out_shape=jax.ShapeDtypeStruct((n, D_MODEL), BF16),
        compiler_params=pltpu.CompilerParams(dimension_semantics=("parallel", "parallel")),
        name="merge",
    )(u, o_swa, o_ret, o_mem, w_gate, w_gate, w_gate, bias, bias, bias, w_br, w_br, w_br)


def _ffn_kernel(h_ref, hj_ref, gf_ref, wug_ref, wuv_ref, wc_ref, b_ref, wd_ref, cb_ref, gfin_ref, y_ref, co_ref,
                u_sc, act_sc, ag_sc, av_sc, y_sc, cg_sc, cv_sc, *, tm, tf, td, nf, seg, nseg):
    i = pl.program_id(1)
    t = pl.program_id(2)
    nd = D_MODEL // td

    seg_t = tm // nseg
    stride = seg_t + 8

    def up_proj():
        u = u_sc[...]
        for a_sc, wu_ref in ((ag_sc, wug_ref), (av_sc, wuv_ref)):
            a = jnp.dot(u, wu_ref[...], preferred_element_type=F32)
            for s in range(nseg):
                a_sc[s * stride + 8:(s + 1) * stride, :] = a[s * seg_t:(s + 1) * seg_t]

    def conv_act():
        fb = t - 1
        for a_sc, blk, carry_sc in ((ag_sc, fb, cg_sc), (av_sc, nf + fb, cv_sc)):
            if nseg == 1:
                @pl.when(i == 0)
                def _():
                    a_sc[6:8, :] = cb_ref[0, blk]

                @pl.when(i > 0)
                def _():
                    a_sc[6:8, :] = carry_sc[fb, 6:8, :]

                carry_sc[fb] = a_sc[tm:tm + 8, :]
            else:
                for s in range(nseg):
                    a_sc[s * stride + 6:s * stride + 8, :] = cb_ref[s, blk]
            for s in range(nseg):
                co_ref[0, s, blk] = a_sc[(s + 1) * stride - 8:(s + 1) * stride, :]

        def conv(a_sc, blk, r0, c0):
            tap = lambda k: a_sc[r0 - k:r0 - k + CONV_ROWS, c0:c0 + LANE]
            wc = lambda k: wc_ref[blk, k:k + 1, c0:c0 + LANE]
            return b_ref[blk, :, c0:c0 + LANE] + (wc(0) * tap(2) + wc(1) * tap(1) + wc(2) * tap(0))

        for c0 in range(0, tf, LANE):
            for s in range(nseg):
                for r in range(0, seg_t, CONV_ROWS):
                    cg = conv(ag_sc, fb, s * stride + 8 + r, c0)
                    cv = conv(av_sc, nf + fb, s * stride + 8 + r, c0)
                    act_sc[fb, s * seg_t + r:s * seg_t + r + CONV_ROWS, c0:c0 + LANE] = (
                        cg * _sigmoid(cg) * cv).astype(BF16)

    @pl.when(t == 0)
    def _():
        u_sc[...] = _rms(h_ref[0], gf_ref[...]).astype(BF16)
        up_proj()

    @pl.when((t >= 1) & (t < nf))
    def _():
        conv_act()
        up_proj()

    def down(j):
        act = jnp.concatenate([act_sc[f] for f in range(nf)], axis=1)
        y_sc[j] = hj_ref[0] + jnp.dot(act, wd_ref[...], preferred_element_type=F32)

    @pl.when(t == nf)
    def _():
        conv_act()
        down(0)

    @pl.when(t > nf)
    def _():
        down(t - nf)

    @pl.when(t == nf + nd - 1)
    def _():
        full = jnp.concatenate([y_sc[k] for k in range(nd)], axis=1)
        y_ref[0] = _rms(full, gfin_ref[...])


def _ffn(h, conv_buf, g_ffn, w_up, w_conv, b_conv, w_down, g_final, *, tm, tf, td, seg, nseg):
    bt, rows, d = h.shape
    nf = D_FF // tf
    nd = d // td
    nt = rows // tm
    nb = bt * nseg
    up_blk = lambda t: jnp.minimum(t, nf - 1)
    down_blk = lambda t: jnp.clip(t - nf, 0, nd - 1)
    wc_blocks = w_conv.reshape(3, 2 * nf, tf).transpose(1, 0, 2)
    b_blocks = b_conv.reshape(2 * nf, 1, tf)
    cb_blocks = conv_buf.reshape(nb, 2, 2 * nf, tf).transpose(0, 2, 1, 3)
    whole = lambda *shape: pl.BlockSpec(shape, lambda b, i, t: (0,) * len(shape))
    in_specs = [
        pl.BlockSpec((1, tm, d), lambda b, i, t: (b, i, 0)),
        pl.BlockSpec((1, tm, td), lambda b, i, t: (b, i, down_blk(t))),
        whole(1, d),
        pl.BlockSpec((d, tf), lambda b, i, t: (0, up_blk(t))),
        pl.BlockSpec((d, tf), lambda b, i, t: (0, nf + up_blk(t))),
        whole(2 * nf, 3, tf),
        whole(2 * nf, 1, tf),
        pl.BlockSpec((D_FF, td), lambda b, i, t: (0, down_blk(t))),
        pl.BlockSpec((nseg, 2 * nf, 2, tf), lambda b, i, t: (b, 0, 0, 0)),
        whole(1, d),
    ]
    out_specs = [
        pl.BlockSpec((1, tm, d), lambda b, i, t: (b, i, 0)),
        pl.BlockSpec((1, nseg, 2 * nf, 8, tf), lambda b, i, t: (b * nt + i, 0, 0, 0, 0)),
    ]
    return pl.pallas_call(
        functools.partial(_ffn_kernel, tm=tm, tf=tf, td=td, nf=nf, seg=seg, nseg=nseg),
        grid=(bt, nt, nf + nd),
        in_specs=in_specs,
        out_specs=out_specs,
        out_shape=[jax.ShapeDtypeStruct((bt, rows, d), F32),
                   jax.ShapeDtypeStruct((bt * nt, nseg, 2 * nf, 8, tf), F32)],
        scratch_shapes=[pltpu.VMEM((tm, d), BF16), pltpu.VMEM((nf, tm, tf), BF16),
                        pltpu.VMEM((tm + 8 * nseg, tf), F32), pltpu.VMEM((tm + 8 * nseg, tf), F32),
                        pltpu.VMEM((nd, tm, td), F32),
                        pltpu.VMEM((nf, 8, tf), F32), pltpu.VMEM((nf, 8, tf), F32)],
        compiler_params=pltpu.CompilerParams(dimension_semantics=("arbitrary", "arbitrary", "arbitrary")),
        name="conv_ffn",
    )(h, h, g_ffn.reshape(1, d), w_up, w_up, wc_blocks, b_blocks, w_down, cb_blocks, g_final.reshape(1, d))


def _rope_tables(pos0, s):
    half = RET_DIM // 2
    inv_freq = 1.0 / (RET_ROPE_BASE ** jnp.linspace(0.0, 1.0, half, dtype=F32))
    ang = (pos0 + jnp.arange(s)).astype(F32)[:, None] * inv_freq[None, :]
    cos, sin = jnp.cos(ang), jnp.sin(ang)
    return jnp.concatenate([cos, cos], axis=1), jnp.concatenate([-sin, sin], axis=1)


def _prep_weights(g_mix, w_in, b_gate, sink, w_br, w_o, g_ffn, w_up, w_conv, b_conv, w_down):
    o = 0
    cuts = {}
    for name, width in (("qa", SWA_Q), ("ka", SWA_KV), ("va", SWA_KV), ("qr", RET_W), ("kr", RET_W),
                        ("vr", RET_W), ("gr", RET_W), ("qm", MEM_W), ("gl", N_BRANCH * D_MODEL)):
        cuts[name] = (o, o + width)
        o += width
    cols = lambda *names: jnp.concatenate([w_in[:, cuts[n][0]:cuts[n][1]] for n in names], axis=1).astype(BF16)
    ones = lambda n, v: jnp.full((n,), v, F32)
    return dict(
        g_mix=g_mix, b_gate=b_gate, sink=sink, g_ffn=g_ffn, w_conv=w_conv, b_conv=b_conv,
        w_plain=cols("qa", "vr", "qm"),
        s_plain=jnp.concatenate([ones(SWA_Q, SWA_HEAD_DIM ** -0.5), ones(RET_W, 1.0),
                                 ones(MEM_W, MEM_HEAD_DIM ** -0.5)]),
        w_kv=cols("ka", "va"),
        w_rot=cols("qr", "kr"),
        s_rot=jnp.concatenate([ones(RET_W, 1.0), ones(RET_W, RET_DIM ** -0.5)]),
        w_g=cols("gr"),
        w_gate=cols("gl"),
        w_br=w_br.astype(BF16), w_o=w_o.astype(BF16), w_up=w_up.astype(BF16), w_down=w_down.astype(BF16),
    )


def _run_group(x, pos0, mem_k, mem_v, swa_cache, ret_state, conv_buf, wts, g_final, log_g):
    b, s, d = x.shape
    n = b * s
    tm = min(1024, n)
    x2 = x.reshape(n, d)
    plain, u = _norm_mm(x2, wts["g_mix"], wts["w_plain"], wts["s_plain"], tm=tm, tn=1024, name="proj_plain")
    kv = _mm(u, wts["w_kv"], epilogue="plain", out_dtype=F32, tm=tm, tn=2 * SWA_KV, name="proj_kv")
    cc, ss = _rope_tables(pos0, s)
    if s < tm:
        cc, ss = jnp.tile(cc, (tm // s, 1)), jnp.tile(ss, (tm // s, 1))
    rot = _mm(u, wts["w_rot"], epilogue="rotary", out_dtype=BF16, tm=tm, tn=1024,
              colvec=wts["s_rot"], tables=(cc, ss), name="proj_rot")
    sgate = _mm(u, wts["w_g"], epilogue="silu", out_dtype=BF16, tm=tm, tn=1024, name="proj_silu")

    plain3 = plain.reshape(b, s, -1)
    kv3 = kv.reshape(b, s, 2 * SWA_KV)
    if swa_cache is None:
        o_swa = _swa(plain3, 0, kv3, 0, kv3, 1, wts["sink"], sq=s, qt=min(256, s), kv_off=0)
        new_k = kv3[:, s - WINDOW:, :SWA_KV]
        new_v = kv3[:, s - WINDOW:, SWA_KV:]
    else:
        ck = swa_cache[0].reshape(b, -1, SWA_KV)
        cv = swa_cache[1].reshape(b, -1, SWA_KV)
        n_keep = ck.shape[1]
        k_all = jnp.concatenate([ck, kv3[:, :, :SWA_KV]], axis=1)
        v_all = jnp.concatenate([cv, kv3[:, :, SWA_KV:]], axis=1)
        o_swa = _swa(plain3, 0, k_all, 0, v_all, 0, wts["sink"], sq=s, qt=s, kv_off=n_keep)
        new_k = k_all[:, -n_keep:]
        new_v = v_all[:, -n_keep:]
    state0 = jnp.zeros((b, RET_HEADS, RET_DIM, RET_DIM), F32) if ret_state is None else ret_state
    o_ret, s_new = _retention(rot.reshape(b, s, -1), 0, 1, plain3, 1, sgate.reshape(b, s, -1), state0, log_g,
                              s=s, c=min(256, s))
    o_mem = _mem_attend(plain3, 2, mem_k.reshape(b, N_MEM, MEM_W), mem_v.reshape(b, N_MEM, MEM_W),
                        s=s, tq=min(512, s))
    merged = _merge(u, o_swa.reshape(n, -1), o_ret.reshape(n, -1), o_mem.reshape(n, -1), wts["w_gate"],
                    wts["b_gate"], wts["w_br"], tm=tm, tn=512)
    h1 = _mm(merged, wts["w_o"], epilogue="residual", out_dtype=F32, tm=min(512, n), tn=d, residual=x2,
             name="proj_out")

    if s >= 512:
        y, co = _ffn(h1.reshape(b, s, d), conv_buf, wts["g_ffn"], wts["w_up"], wts["w_conv"], wts["b_conv"],
                     wts["w_down"], g_final, tm=512, tf=512, td=512, seg=s, nseg=1)
    else:
        y, co = _ffn(h1.reshape(1, n, d), conv_buf, wts["g_ffn"], wts["w_up"], wts["w_conv"], wts["b_conv"],
                     wts["w_down"], g_final, tm=n, tf=512, td=512, seg=s, nseg=b)
    nblk, tfb = co.shape[2], co.shape[4]
    last = co.reshape(b, -1, nblk, 8, tfb)[:, -1, :, 6:8, :]
    new_buf = last.transpose(0, 2, 1, 3).reshape(b, 2, nblk * tfb)
    return (y.reshape(b, s, d), new_k.reshape(b, -1, SWA_KV_HEADS, SWA_HEAD_DIM),
            new_v.reshape(b, -1, SWA_KV_HEADS, SWA_HEAD_DIM), s_new, new_buf)


def _memory_kv(mem, g_mem, w_mem_kv):
    b, m, d = mem.shape
    u = _rmsnorm(mem.reshape(b * m, d), g_mem, min(512, b * m))
    kv = _mm(u, w_mem_kv.astype(BF16), epilogue="plain", out_dtype=F32, tm=min(1024, b * m), tn=1024, name="mem_kv")
    return (kv[:, :MEM_W].reshape(b, m, MEM_HEADS, MEM_HEAD_DIM), kv[:, MEM_W:].reshape(b, m, MEM_HEADS, MEM_HEAD_DIM))


def kernel(x_prompt, x_sample, mem_prompt, cache_swa_k, cache_swa_v, state_ret, state_ffn_conv, cache_mem_k, cache_mem_v, g_mix, w_in, b_gate, sink, w_br, w_o, g_mem, w_mem_kv, g_ffn, w_up, w_conv, b_conv, w_down, g_final):
    bp = x_prompt.shape[0]
    depth = w_in.shape[0]
    log_g = jnp.log1p(-jnp.exp2(-5.0 - jnp.arange(RET_HEADS, dtype=F32)))
    hp, hs = x_prompt, x_sample
    outs_p = [[] for _ in range(6)]
    outs_s = [[] for _ in range(4)]
    for l in range(depth):
        wts = _prep_weights(g_mix[l], w_in[l], b_gate[l], sink[l], w_br[l], w_o[l],
                            g_ffn[l], w_up[l], w_conv[l], b_conv[l], w_down[l])
        mk, mv = _memory_kv(mem_prompt, g_mem[l], w_mem_kv[l])
        zero_buf = jnp.zeros((bp, 2, 2 * D_FF), F32)
        assert depth == 1
        hp, kp, vp, sp, cp = _run_group(hp, 0, mk, mv, None, None, zero_buf, wts, g_final, log_g)
        hs, ksn, vsn, ssn, csn = _run_group(hs, PAST_LEN, cache_mem_k[l], cache_mem_v[l],
                                            (cache_swa_k[l], cache_swa_v[l]), state_ret[l],
                                            state_ffn_conv[l], wts, g_final, log_g)
        for lst, val in zip(outs_p, (kp, vp, sp, cp, mk, mv)):
            lst.append(val)
        for lst, val in zip(outs_s, (ksn, vsn, ssn, csn)):
            lst.append(val)
    return (hp, hs, *[jnp.stack(v) for v in outs_p], *[jnp.stack(v) for v in outs_s])
```

```python
import functools

import jax
import jax.numpy as jnp
from jax import lax
from jax.experimental import pallas as pl
from jax.experimental.pallas import tpu as pltpu

F32 = jnp.float32
BF16 = jnp.bfloat16

D_MODEL = 2048
CHUNK = 64
WINDOW = 128
SWA_HEADS = 16
SWA_KV_HEADS = 4
SWA_GROUP = SWA_HEADS // SWA_KV_HEADS
SWA_HEAD_DIM = 64
RET_HEADS = 8
RET_DIM = 128
RET_ROPE_BASE = 10000.0
N_MEM = 256
MEM_HEADS = 4
MEM_HEAD_DIM = 256
D_FF = 5632
N_BRANCH = 3
EPS = 1e-6
NEG = -1e30
PAST_LEN = 1024

SWA_Q = SWA_HEADS * SWA_HEAD_DIM
SWA_KV = SWA_KV_HEADS * SWA_HEAD_DIM
RET_W = RET_HEADS * RET_DIM
MEM_W = MEM_HEADS * MEM_HEAD_DIM
LANE = 128
CONV_ROWS = 64

def _sigmoid(x):
    return 1.0 / (1.0 + jnp.exp(-x))


def _rms(x, g):
    return x * lax.rsqrt(jnp.mean(x * x, axis=-1, keepdims=True) + EPS) * g


def _rmsnorm_kernel(x_ref, g_ref, o_ref):
    o_ref[...] = _rms(x_ref[...], g_ref[...]).astype(o_ref.dtype)


def _rmsnorm(x, g, tm):
    n, d = x.shape
    return pl.pallas_call(
        _rmsnorm_kernel,
        grid=(n // tm,),
        in_specs=[pl.BlockSpec((tm, d), lambda i: (i, 0)),
                  pl.BlockSpec((1, d), lambda i: (0, 0))],
        out_specs=pl.BlockSpec((tm, d), lambda i: (i, 0)),
        out_shape=jax.ShapeDtypeStruct((n, d), BF16),
        name="rmsnorm",
    )(x, g.reshape(1, d))


def _mm_kernel(x_ref, w_ref, *rest, epilogue, tn):
    o_ref = rest[-1]
    acc = jnp.dot(x_ref[...], w_ref[...], preferred_element_type=F32)
    if epilogue == "plain":
        out = acc
    elif epilogue == "scale":
        out = acc * rest[0][...]
    elif epilogue == "silu":
        out = acc * _sigmoid(acc)
    elif epilogue == "sigmoid_bias":
        out = _sigmoid(acc + rest[0][...])
    elif epilogue == "residual":
        out = rest[0][...] + acc
    elif epilogue == "rotary":
        cc, ss = rest[1][...], rest[2][...]
        pieces = []
        for j in range(tn // LANE):
            xh = acc[:, j * LANE:(j + 1) * LANE]
            pieces.append(xh * cc + pltpu.roll(xh, LANE // 2, 1) * ss)
        out = jnp.concatenate(pieces, axis=1) * rest[0][...]
    else:
        raise ValueError(epilogue)
    o_ref[...] = out.astype(o_ref.dtype)


def _mm(x, w, *, epilogue, out_dtype, tm, tn, colvec=None, residual=None, tables=None, name):
    n, k = x.shape
    ncols = w.shape[1]
    in_specs = [pl.BlockSpec((tm, k), lambda i, j: (i, 0)),
                pl.BlockSpec((k, tn), lambda i, j: (0, j))]
    args = [x, w]
    if colvec is not None:
        in_specs.append(pl.BlockSpec((1, tn), lambda i, j: (0, j)))
        args.append(colvec.reshape(1, ncols).astype(F32))
    if residual is not None:
        in_specs.append(pl.BlockSpec((tm, tn), lambda i, j: (i, j)))
        args.append(residual)
    if tables is not None:
        nblk = tables[0].shape[0] // tm
        for t in tables:
            in_specs.append(pl.BlockSpec((tm, LANE), lambda i, j: (i % nblk, 0)))
            args.append(t)
    return pl.pallas_call(
        functools.partial(_mm_kernel, epilogue=epilogue, tn=tn),
        grid=(n // tm, ncols // tn),
        in_specs=in_specs,
        out_specs=pl.BlockSpec((tm, tn), lambda i, j: (i, j)),
        out_shape=jax.ShapeDtypeStruct((n, ncols), out_dtype),
        compiler_params=pltpu.CompilerParams(dimension_semantics=("parallel", "parallel")),
        name=name,
    )(*args)


def _norm_mm_kernel(x_ref, g_ref, w_ref, cv_ref, o_ref, u_ref):
    @pl.when(pl.program_id(1) == 0)
    def _():
        u_ref[...] = _rms(x_ref[...], g_ref[...]).astype(u_ref.dtype)

    acc = jnp.dot(u_ref[...], w_ref[...], preferred_element_type=F32)
    o_ref[...] = (acc * cv_ref[...]).astype(o_ref.dtype)


def _norm_mm(x, g, w, colvec, *, tm, tn, name):
    n, k = x.shape
    ncols = w.shape[1]
    return pl.pallas_call(
        _norm_mm_kernel,
        grid=(n // tm, ncols // tn),
        in_specs=[pl.BlockSpec((tm, k), lambda i, j: (i, 0)),
                  pl.BlockSpec((1, k), lambda i, j: (0, 0)),
                  pl.BlockSpec((k, tn), lambda i, j: (0, j)),
                  pl.BlockSpec((1, tn), lambda i, j: (0, j))],
        out_specs=[pl.BlockSpec((tm, tn), lambda i, j: (i, j)),
                   pl.BlockSpec((tm, k), lambda i, j: (i, 0))],
        out_shape=[jax.ShapeDtypeStruct((n, ncols), BF16), jax.ShapeDtypeStruct((n, k), BF16)],
        compiler_params=pltpu.CompilerParams(dimension_semantics=("parallel", "arbitrary")),
        name=name,
    )(x, g.reshape(1, k), w, colvec.reshape(1, ncols).astype(F32))


def _swa_kernel(sink_ref, q_ref, kp_ref, kc_ref, vp_ref, vc_ref, o_ref, *scratch, qt, kv_off):
    s_scs, p_scs = scratch[:SWA_KV_HEADS], scratch[SWA_KV_HEADS:]
    i = pl.program_id(1)
    nq = qt // CHUNK
    w = (nq + 2) * CHUNK
    wp = -(-w // LANE) * LANE
    first_ok = jnp.where(i * qt + kv_off >= WINDOW, 0, 2)

    def window(p_ref, c_ref):
        parts = [p_ref[0], c_ref[0]]
        if wp > w:
            parts.append(jnp.zeros((wp - w, SWA_KV), F32))
        return jnp.concatenate(parts, axis=0)

    kwin = window(kp_ref, kc_ref)
    vwin = window(vp_ref, vc_ref)
    lane = lax.broadcasted_iota(jnp.int32, (wp, LANE), 1)
    lane_g = lax.broadcasted_iota(jnp.int32, (wp, SWA_KV), 1) // SWA_HEAD_DIM
    out_g = lax.broadcasted_iota(jnp.int32, (CHUNK, SWA_KV), 1) // SWA_HEAD_DIM

    def block_diag(win, h):
        col, half = divmod(h, 2)
        x = win[:, col * LANE:(col + 1) * LANE]
        xr = pltpu.roll(x, SWA_HEAD_DIM, 1)
        lo = lane < SWA_HEAD_DIM
        r = jnp.where(lo, x, xr) if half == 0 else jnp.where(lo, xr, x)
        r2 = jnp.concatenate([r, r], axis=1).astype(BF16)
        zero = jnp.zeros_like(r2)
        return jnp.concatenate([jnp.where(lane_g == g, r2, zero) for g in range(SWA_GROUP)], axis=0)

    for h in range(SWA_KV_HEADS):
        bk = block_diag(kwin, h)
        bv = block_diag(vwin, h)
        qh = q_ref[0, :, h * SWA_KV:(h + 1) * SWA_KV]
        s_sc, p_sc = s_scs[h], p_scs[h]
        s_sc[...] = lax.dot_general(qh, bk, (((1,), (1,)), ((), ())), preferred_element_type=F32)
        invs = []
        for jq in range(nq):
            rows = slice(jq * CHUNK, (jq + 1) * CHUNK)
            c_lo, c_hi = jq * CHUNK // LANE, ((jq + 3) * CHUNK - 1) // LANE
            width = (c_hi - c_lo + 1) * LANE
            kc = c_lo * (LANE // CHUNK) + lax.broadcasted_iota(jnp.int32, (CHUNK, width), 1) // CHUNK
            visible = (kc >= jq) & (kc <= jq + 2) & (kc >= first_ok)
            inv = jnp.zeros((CHUNK, SWA_KV), F32)
            for g in range(SWA_GROUP):
                l0 = g * wp + c_lo * LANE
                sg = jnp.where(visible, s_sc[rows, l0:l0 + width], NEG)
                sk = sink_ref[h * SWA_GROUP + g]
                m = jnp.maximum(jnp.max(sg, axis=1, keepdims=True), sk)
                p = jnp.exp(sg - m)
                den = jnp.sum(p, axis=1, keepdims=True) + jnp.exp(sk - m)
                p_sc[rows, l0:l0 + width] = p.astype(BF16)
                for c in range(wp // LANE):
                    if not c_lo <= c <= c_hi:
                        p_sc[rows, g * wp + c * LANE:g * wp + (c + 1) * LANE] = jnp.zeros((CHUNK, LANE), BF16)
                inv = jnp.where(out_g == g, 1.0 / den, inv)
            invs.append(inv)
        o = jnp.dot(p_sc[...], bv, preferred_element_type=F32)
        o_ref[0, :, h * SWA_KV:(h + 1) * SWA_KV] = (o * jnp.concatenate(invs, axis=0)).astype(o_ref.dtype)


def _swa(q, q_col, k, k_col, v, v_col, sink, *, sq, qt, kv_off):
    b = q.shape[0]
    cur_off = kv_off // qt
    prev_off = kv_off // WINDOW - 1
    per = qt // WINDOW if qt >= WINDOW else 0
    wp = -(-(qt + WINDOW) // LANE) * LANE

    def prev_map(kcol):
        return lambda bi, i, s: (bi, jnp.maximum(i * per + prev_off, 0), kcol)

    def cur_map(kcol):
        return lambda bi, i, s: (bi, i + cur_off, kcol)

    grid_spec = pltpu.PrefetchScalarGridSpec(
        num_scalar_prefetch=1,
        grid=(b, sq // qt),
        in_specs=[pl.BlockSpec((1, qt, SWA_Q), lambda bi, i, s: (bi, i, q_col)),
                  pl.BlockSpec((1, WINDOW, SWA_KV), prev_map(k_col)),
                  pl.BlockSpec((1, qt, SWA_KV), cur_map(k_col)),
                  pl.BlockSpec((1, WINDOW, SWA_KV), prev_map(v_col)),
                  pl.BlockSpec((1, qt, SWA_KV), cur_map(v_col))],
        out_specs=pl.BlockSpec((1, qt, SWA_Q), lambda bi, i, s: (bi, i, 0)),
        scratch_shapes=([pltpu.VMEM((qt, SWA_GROUP * wp), F32)] * SWA_KV_HEADS
                        + [pltpu.VMEM((qt, SWA_GROUP * wp), BF16)] * SWA_KV_HEADS),
    )
    return pl.pallas_call(
        functools.partial(_swa_kernel, qt=qt, kv_off=kv_off),
        grid_spec=grid_spec,
        out_shape=jax.ShapeDtypeStruct((b, sq, SWA_Q), BF16),
        compiler_params=pltpu.CompilerParams(dimension_semantics=("parallel", "parallel")),
        name="swa",
    )(sink.astype(F32), q, k, k, v, v)


def _ret_kernel(lg_ref, q_ref, k_ref, v_ref, g_ref, s0_ref, o_ref, st_ref, dec_ref, xi_ref, zeta_ref, *, c):
    bi = pl.program_id(0)
    ci = pl.program_id(1)

    @pl.when((bi == 0) & (ci == 0))
    def _():
        n = lax.broadcasted_iota(jnp.int32, (c, c), 0)
        m = lax.broadcasted_iota(jnp.int32, (c, c), 1)
        diff = (n - m).astype(F32)
        nrow = lax.broadcasted_iota(jnp.int32, (c, RET_DIM), 0).astype(F32)
        for h in range(RET_HEADS):
            dec_ref[h] = jnp.where(diff >= 0, jnp.exp(lg_ref[h] * jnp.maximum(diff, 0.0)), 0.0)
            xi_ref[h] = jnp.exp(lg_ref[h] * (nrow + 1.0))
            zeta_ref[h] = jnp.exp(lg_ref[h] * (float(c) - 1.0 - nrow))

    @pl.when(ci == 0)
    def _():
        st_ref[...] = s0_ref[...]

    for h in range(RET_HEADS):
        lg = lg_ref[h]
        sl = slice(h * RET_DIM, (h + 1) * RET_DIM)
        qh = q_ref[0, :, sl]
        kh = k_ref[0, :, sl]
        vh = v_ref[0, :, sl]
        state = st_ref[0, h]
        inner = lax.dot_general(qh, kh, (((1,), (1,)), ((), ())), preferred_element_type=F32) * dec_ref[h]
        o = jnp.dot(inner.astype(BF16), vh, preferred_element_type=F32)
        cross = jnp.dot(qh, state.astype(BF16), preferred_element_type=F32)
        o = o + cross * xi_ref[h]
        kz = (kh.astype(F32) * zeta_ref[h]).astype(BF16)
        upd = lax.dot_general(kz, vh, (((0,), (0,)), ((), ())), preferred_element_type=F32)
        decay_c = jnp.exp(lg * jnp.full((1, RET_DIM), float(c), F32))
        st_ref[0, h] = decay_c * state + upd
        on = o * lax.rsqrt(jnp.mean(o * o, axis=-1, keepdims=True) + EPS)
        o_ref[0, :, sl] = (g_ref[0, :, sl].astype(F32) * on).astype(o_ref.dtype)


def _retention(qk, q_col, k_col, v, v_col, gate, state0, log_g, *, s, c):
    b = qk.shape[0]

    def seq(col):
        return pl.BlockSpec((1, c, RET_W), lambda bi, ci, lg: (bi, ci, col))

    st_spec = pl.BlockSpec((1, RET_HEADS, RET_DIM, RET_DIM), lambda bi, ci, lg: (bi, 0, 0, 0))
    grid_spec = pltpu.PrefetchScalarGridSpec(
        num_scalar_prefetch=1,
        grid=(b, s // c),
        in_specs=[seq(q_col), seq(k_col), seq(v_col), seq(0), st_spec],
        out_specs=[seq(0), st_spec],
        scratch_shapes=[pltpu.VMEM((RET_HEADS, c, c), F32), pltpu.VMEM((RET_HEADS, c, RET_DIM), F32),
                        pltpu.VMEM((RET_HEADS, c, RET_DIM), F32)],
    )
    return pl.pallas_call(
        functools.partial(_ret_kernel, c=c),
        grid_spec=grid_spec,
        out_shape=[jax.ShapeDtypeStruct((b, s, RET_W), BF16),
                   jax.ShapeDtypeStruct((b, RET_HEADS, RET_DIM, RET_DIM), F32)],
        compiler_params=pltpu.CompilerParams(dimension_semantics=("arbitrary", "arbitrary")),
        name="retention",
    )(log_g, qk, qk, v, gate, state0)


def _mem_kernel(q_ref, k_ref, v_ref, o_ref):
    for h in range(MEM_HEADS):
        sl = slice(h * MEM_HEAD_DIM, (h + 1) * MEM_HEAD_DIM)
        qh = q_ref[0, :, sl]
        kh = k_ref[0, :, sl].astype(BF16)
        vh = v_ref[0, :, sl].astype(BF16)
        s = lax.dot_general(qh, kh, (((1,), (1,)), ((), ())), preferred_element_type=F32)
        m = jnp.max(s, axis=1, keepdims=True)
        p = jnp.exp(s - m)
        den = jnp.sum(p, axis=1, keepdims=True)
        o = jnp.dot(p.astype(BF16), vh, preferred_element_type=F32)
        o_ref[0, :, sl] = (o * (1.0 / den)).astype(o_ref.dtype)


def _mem_attend(q, q_col, mk, mv, *, s, tq):
    b = q.shape[0]
    kv_spec = pl.BlockSpec((1, N_MEM, MEM_W), lambda bi, i: (bi, 0, 0))
    return pl.pallas_call(
        _mem_kernel,
        grid=(b, s // tq),
        in_specs=[pl.BlockSpec((1, tq, MEM_W), lambda bi, i: (bi, i, q_col)), kv_spec, kv_spec],
        out_specs=pl.BlockSpec((1, tq, MEM_W), lambda bi, i: (bi, i, 0)),
        out_shape=jax.ShapeDtypeStruct((b, s, MEM_W), BF16),
        compiler_params=pltpu.CompilerParams(dimension_semantics=("parallel", "parallel")),
        name="mem_attend",
    )(q, mk, mv)


def _merge_kernel(u_ref, oa_ref, ob_ref, oc_ref, wga_ref, wgb_ref, wgc_ref, ba_ref, bb_ref, bc_ref,
                  wa_ref, wb_ref, wc_ref, o_ref):
    u = u_ref[...]
    acc = None
    for o_r, wg_ref, b_ref, w_ref in ((oa_ref, wga_ref, ba_ref, wa_ref), (ob_ref, wgb_ref, bb_ref, wb_ref),
                                      (oc_ref, wgc_ref, bc_ref, wc_ref)):
        gate = _sigmoid(jnp.dot(u, wg_ref[...], preferred_element_type=F32) + b_ref[...])
        term = gate * jnp.dot(o_r[...], w_ref[...], preferred_element_type=F32)
        acc = term if acc is None else acc + term
    o_ref[...] = acc.astype(o_ref.dtype)


def _merge(u, o_swa, o_ret, o_mem, w_gate, b_gate, w_br, *, tm, tn):
    n, d = u.shape
    kb = o_swa.shape[1]
    nj = D_MODEL // tn
    o_spec = pl.BlockSpec((tm, kb), lambda i, j: (i, 0))

    def wg_spec(br):
        return pl.BlockSpec((d, tn), lambda i, j: (0, br * nj + j))

    def b_spec(br):
        return pl.BlockSpec((1, tn), lambda i, j: (0, br * nj + j))

    def w_spec(br):
        return pl.BlockSpec((kb, tn), lambda i, j: (br, j))

    bias = b_gate.reshape(1, -1).astype(F32)
    return pl.pallas_call(
        _merge_kernel,
        grid=(n // tm, nj),
        in_specs=[pl.BlockSpec((tm, d), lambda i, j: (i, 0)), o_spec, o_spec, o_spec,
                  wg_spec(0), wg_spec(1), wg_spec(2), b_spec(0), b_spec(1), b_spec(2),
                  w_spec(0), w_spec(1), w_spec(2)],
        out_specs=pl.BlockSpec((tm, tn), lambda i, j: (i, j)),
        out_shape=jax.ShapeDtypeStruct((n, D_MODEL), BF16),
        compiler_params=pltpu.CompilerParams(dimension_semantics=("parallel", "parallel")),
        name="merge",
    )(u, o_swa, o_ret, o_mem, w_gate, w_gate, w_gate, bias, bias, bias, w_br, w_br, w_br)


def _ffn_kernel(h_ref, hj_ref, gf_ref, wug_ref, wuv_ref, wc_ref, b_ref, wd_ref, cb_ref, gfin_ref, y_ref, co_ref,
                u_sc, act_sc, ag_sc, av_sc, y_sc, cg_sc, cv_sc, *, tm, tf, td, nf, seg, nseg):
    i = pl.program_id(1)
    t = pl.program_id(2)
    nd = D_MODEL // td

    seg_t = tm // nseg
    stride = seg_t + 8

    def up_proj():
        u = u_sc[...]
        for a_sc, wu_ref in ((ag_sc, wug_ref), (av_sc, wuv_ref)):
            a = jnp.dot(u, wu_ref[...], preferred_element_type=F32)
            for s in range(nseg):
                a_sc[s * stride + 8:(s + 1) * stride, :] = a[s * seg_t:(s + 1) * seg_t]

    def conv_act():
        fb = t - 1
        for a_sc, blk, carry_sc in ((ag_sc, fb, cg_sc), (av_sc, nf + fb, cv_sc)):
            if nseg == 1:
                @pl.when(i == 0)
                def _():
                    a_sc[6:8, :] = cb_ref[0, blk]

                @pl.when(i > 0)
                def _():
                    a_sc[6:8, :] = carry_sc[fb, 6:8, :]

                carry_sc[fb] = a_sc[tm:tm + 8, :]
            else:
                for s in range(nseg):
                    a_sc[s * stride + 6:s * stride + 8, :] = cb_ref[s, blk]
            for s in range(nseg):
                co_ref[0, s, blk] = a_sc[(s + 1) * stride - 8:(s + 1) * stride, :]

        def conv(a_sc, blk, r0, c0):
            tap = lambda k: a_sc[r0 - k:r0 - k + CONV_ROWS, c0:c0 + LANE]
            wc = lambda k: wc_ref[blk, k:k + 1, c0:c0 + LANE]
            return b_ref[blk, :, c0:c0 + LANE] + (wc(0) * tap(2) + wc(1) * tap(1) + wc(2) * tap(0))

        for c0 in range(0, tf, LANE):
            for s in range(nseg):
                for r in range(0, seg_t, CONV_ROWS):
                    cg = conv(ag_sc, fb, s * stride + 8 + r, c0)
                    cv = conv(av_sc, nf + fb, s * stride + 8 + r, c0)
                    act_sc[fb, s * seg_t + r:s * seg_t + r + CONV_ROWS, c0:c0 + LANE] = (
                        cg * _sigmoid(cg) * cv).astype(BF16)

    @pl.when(t == 0)
    def _():
        u_sc[...] = _rms(h_ref[0], gf_ref[...]).astype(BF16)
        up_proj()

    @pl.when((t >= 1) & (t < nf))
    def _():
        conv_act()
        up_proj()

    def down(j):
        act = jnp.concatenate([act_sc[f] for f in range(nf)], axis=1)
        y_sc[j] = hj_ref[0] + jnp.dot(act, wd_ref[...], preferred_element_type=F32)

    @pl.when(t == nf)
    def _():
        conv_act()
        down(0)

    @pl.when(t > nf)
    def _():
        down(t - nf)

    @pl.when(t == nf + nd - 1)
    def _():
        full = jnp.concatenate([y_sc[k] for k in range(nd)], axis=1)
        y_ref[0] = _rms(full, gfin_ref[...])


def _ffn(h, conv_buf, g_ffn, w_up, w_conv, b_conv, w_down, g_final, *, tm, tf, td, seg, nseg):
    bt, rows, d = h.shape
    nf = D_FF // tf
    nd = d // td
    nt = rows // tm
    nb = bt * nseg
    up_blk = lambda t: jnp.minimum(t, nf - 1)
    down_blk = lambda t: jnp.clip(t - nf, 0, nd - 1)
    wc_blocks = w_conv.reshape(3, 2 * nf, tf).transpose(1, 0, 2)
    b_blocks = b_conv.reshape(2 * nf, 1, tf)
    cb_blocks = conv_buf.reshape(nb, 2, 2 * nf, tf).transpose(0, 2, 1, 3)
    whole = lambda *shape: pl.BlockSpec(shape, lambda b, i, t: (0,) * len(shape))
    in_specs = [
        pl.BlockSpec((1, tm, d), lambda b, i, t: (b, i, 0)),
        pl.BlockSpec((1, tm, td), lambda b, i, t: (b, i, down_blk(t))),
        whole(1, d),
        pl.BlockSpec((d, tf), lambda b, i, t: (0, up_blk(t))),
        pl.BlockSpec((d, tf), lambda b, i, t: (0, nf + up_blk(t))),
        whole(2 * nf, 3, tf),
        whole(2 * nf, 1, tf),
        pl.BlockSpec((D_FF, td), lambda b, i, t: (0, down_blk(t))),
        pl.BlockSpec((nseg, 2 * nf, 2, tf), lambda b, i, t: (b, 0, 0, 0)),
        whole(1, d),
    ]
    out_specs = [
        pl.BlockSpec((1, tm, d), lambda b, i, t: (b, i, 0)),
        pl.BlockSpec((1, nseg, 2 * nf, 8, tf), lambda b, i, t: (b * nt + i, 0, 0, 0, 0)),
    ]
    return pl.pallas_call(
        functools.partial(_ffn_kernel, tm=tm, tf=tf, td=td, nf=nf, seg=seg, nseg=nseg),
        grid=(bt, nt, nf + nd),
        in_specs=in_specs,
        out_specs=out_specs,
        out_shape=[jax.ShapeDtypeStruct((bt, rows, d), F32),
                   jax.ShapeDtypeStruct((bt * nt, nseg, 2 * nf, 8, tf), F32)],
        scratch_shapes=[pltpu.VMEM((tm, d), BF16), pltpu.VMEM((nf, tm, tf), BF16),
                        pltpu.VMEM((tm + 8 * nseg, tf), F32), pltpu.VMEM((tm + 8 * nseg, tf), F32),
                        pltpu.VMEM((nd, tm, td), F32),
                        pltpu.VMEM((nf, 8, tf), F32), pltpu.VMEM((nf, 8, tf), F32)],
        compiler_params=pltpu.CompilerParams(dimension_semantics=("arbitrary", "arbitrary", "arbitrary")),
        name="conv_ffn",
    )(h, h, g_ffn.reshape(1, d), w_up, w_up, wc_blocks, b_blocks, w_down, cb_blocks, g_final.reshape(1, d))


def _rope_tables(pos0, s):
    half = RET_DIM // 2
    inv_freq = 1.0 / (RET_ROPE_BASE ** jnp.linspace(0.0, 1.0, half, dtype=F32))
    ang = (pos0 + jnp.arange(s)).astype(F32)[:, None] * inv_freq[None, :]
    cos, sin = jnp.cos(ang), jnp.sin(ang)
    return jnp.concatenate([cos, cos], axis=1), jnp.concatenate([-sin, sin], axis=1)


def _prep_weights(g_mix, w_in, b_gate, sink, w_br, w_o, g_ffn, w_up, w_conv, b_conv, w_down):
    o = 0
    cuts = {}
    for name, width in (("qa", SWA_Q), ("ka", SWA_KV), ("va", SWA_KV), ("qr", RET_W), ("kr", RET_W),
                        ("vr", RET_W), ("gr", RET_W), ("qm", MEM_W), ("gl", N_BRANCH * D_MODEL)):
        cuts[name] = (o, o + width)
        o += width
    cols = lambda *names: jnp.concatenate([w_in[:, cuts[n][0]:cuts[n][1]] for n in names], axis=1).astype(BF16)
    ones = lambda n, v: jnp.full((n,), v, F32)
    return dict(
        g_mix=g_mix, b_gate=b_gate, sink=sink, g_ffn=g_ffn, w_conv=w_conv, b_conv=b_conv,
        w_plain=cols("qa", "vr", "qm"),
        s_plain=jnp.concatenate([ones(SWA_Q, SWA_HEAD_DIM ** -0.5), ones(RET_W, 1.0),
                                 ones(MEM_W, MEM_HEAD_DIM ** -0.5)]),
        w_kv=cols("ka", "va"),
        w_rot=cols("qr", "kr"),
        s_rot=jnp.concatenate([ones(RET_W, 1.0), ones(RET_W, RET_DIM ** -0.5)]),
        w_g=cols("gr"),
        w_gate=cols("gl"),
        w_br=w_br.astype(BF16), w_o=w_o.astype(BF16), w_up=w_up.astype(BF16), w_down=w_down.astype(BF16),
    )


def _run_group(x, pos0, mem_k, mem_v, swa_cache, ret_state, conv_buf, wts, g_final, log_g):
    b, s, d = x.shape
    n = b * s
    tm = min(1024, n)
    x2 = x.reshape(n, d)
    plain, u = _norm_mm(x2, wts["g_mix"], wts["w_plain"], wts["s_plain"], tm=tm, tn=1024, name="proj_plain")
    kv = _mm(u, wts["w_kv"], epilogue="plain", out_dtype=F32, tm=tm, tn=2 * SWA_KV, name="proj_kv")
    cc, ss = _rope_tables(pos0, s)
    if s < tm:
        cc, ss = jnp.tile(cc, (tm // s, 1)), jnp.tile(ss, (tm // s, 1))
    rot = _mm(u, wts["w_rot"], epilogue="rotary", out_dtype=BF16, tm=tm, tn=1024,
              colvec=wts["s_rot"], tables=(cc, ss), name="proj_rot")
    sgate = _mm(u, wts["w_g"], epilogue="silu", out_dtype=BF16, tm=tm, tn=1024, name="proj_silu")

    plain3 = plain.reshape(b, s, -1)
    kv3 = kv.reshape(b, s, 2 * SWA_KV)
    if swa_cache is None:
        o_swa = _swa(plain3, 0, kv3, 0, kv3, 1, wts["sink"], sq=s, qt=min(256, s), kv_off=0)
        new_k = kv3[:, s - WINDOW:, :SWA_KV]
        new_v = kv3[:, s - WINDOW:, SWA_KV:]
    else:
        ck = swa_cache[0].reshape(b, -1, SWA_KV)
        cv = swa_cache[1].reshape(b, -1, SWA_KV)
        n_keep = ck.shape[1]
        k_all = jnp.concatenate([ck, kv3[:, :, :SWA_KV]], axis=1)
        v_all = jnp.concatenate([cv, kv3[:, :, SWA_KV:]], axis=1)
        o_swa = _swa(plain3, 0, k_all, 0, v_all, 0, wts["sink"], sq=s, qt=s, kv_off=n_keep)
        new_k = k_all[:, -n_keep:]
        new_v = v_all[:, -n_keep:]
    state0 = jnp.zeros((b, RET_HEADS, RET_DIM, RET_DIM), F32) if ret_state is None else ret_state
    o_ret, s_new = _retention(rot.reshape(b, s, -1), 0, 1, plain3, 1, sgate.reshape(b, s, -1), state0, log_g,
                              s=s, c=min(256, s))
    o_mem = _mem_attend(plain3, 2, mem_k.reshape(b, N_MEM, MEM_W), mem_v.reshape(b, N_MEM, MEM_W),
                        s=s, tq=min(512, s))
    merged = _merge(u, o_swa.reshape(n, -1), o_ret.reshape(n, -1), o_mem.reshape(n, -1), wts["w_gate"],
                    wts["b_gate"], wts["w_br"], tm=tm, tn=512)
    h1 = _mm(merged, wts["w_o"], epilogue="residual", out_dtype=F32, tm=min(512, n), tn=d, residual=x2,
             name="proj_out")

    if s >= 512:
        y, co = _ffn(h1.reshape(b, s, d), conv_buf, wts["g_ffn"], wts["w_up"], wts["w_conv"], wts["b_conv"],
                     wts["w_down"], g_final, tm=512, tf=512, td=512, seg=s, nseg=1)
    else:
        y, co = _ffn(h1.reshape(1, n, d), conv_buf, wts["g_ffn"], wts["w_up"], wts["w_conv"], wts["b_conv"],
                     wts["w_down"], g_final, tm=n, tf=512, td=512, seg=s, nseg=b)
    nblk, tfb = co.shape[2], co.shape[4]
    last = co.reshape(b, -1, nblk, 8, tfb)[:, -1, :, 6:8, :]
    new_buf = last.transpose(0, 2, 1, 3).reshape(b, 2, nblk * tfb)
    return (y.reshape(b, s, d), new_k.reshape(b, -1, SWA_KV_HEADS, SWA_HEAD_DIM),
            new_v.reshape(b, -1, SWA_KV_HEADS, SWA_HEAD_DIM), s_new, new_buf)


def _memory_kv(mem, g_mem, w_mem_kv):
    b, m, d = mem.shape
    u = _rmsnorm(mem.reshape(b * m, d), g_mem, min(512, b * m))
    kv = _mm(u, w_mem_kv.astype(BF16), epilogue="plain", out_dtype=F32, tm=min(1024, b * m), tn=1024, name="mem_kv")
    return (kv[:, :MEM_W].reshape(b, m, MEM_HEADS, MEM_HEAD_DIM), kv[:, MEM_W:].reshape(b, m, MEM_HEADS, MEM_HEAD_DIM))


def kernel(x_prompt, x_sample, mem_prompt, cache_swa_k, cache_swa_v, state_ret, state_ffn_conv, cache_mem_k, cache_mem_v, g_mix, w_in, b_gate, sink, w_br, w_o, g_mem, w_mem_kv, g_ffn, w_up, w_conv, b_conv, w_down, g_final):
    bp = x_prompt.shape[0]
    depth = w_in.shape[0]
    log_g = jnp.log1p(-jnp.exp2(-5.0 - jnp.arange(RET_HEADS, dtype=F32)))
    hp, hs = x_prompt, x_sample
    outs_p = [[] for _ in range(6)]
    outs_s = [[] for _ in range(4)]
    for l in range(depth):
        wts = _prep_weights(g_mix[l], w_in[l], b_gate[l], sink[l], w_br[l], w_o[l],
                            g_ffn[l], w_up[l], w_conv[l], b_conv[l], w_down[l])
        mk, mv = _memory_kv(mem_prompt, g_mem[l], w_mem_kv[l])
        zero_buf = jnp.zeros((bp, 2, 2 * D_FF), F32)
        assert depth == 1
        hp, kp, vp, sp, cp = _run_group(hp, 0, mk, mv, None, None, zero_buf, wts, g_final, log_g)
        hs, ksn, vsn, ssn, csn = _run_group(hs, PAST_LEN, cache_mem_k[l], cache_mem_v[l],
                                            (cache_swa_k[l], cache_swa_v[l]), state_ret[l],
                                            state_ffn_conv[l], wts, g_final, log_g)
        for lst, val in zip(outs_p, (kp, vp, sp, cp, mk, mv)):
            lst.append(val)
        for lst, val in zip(outs_s, (ksn, vsn, ssn, csn)):
            lst.append(val)
    return (hp, hs, *[jnp.stack(v) for v in outs_p], *[jnp.stack(v) for v in outs_s])
```

```python
import functools

import jax
import jax.numpy as jnp
from jax import lax
from jax.experimental import pallas as pl
from jax.experimental.pallas import tpu as pltpu

F32 = jnp.float32
BF16 = jnp.bfloat16

D_MODEL = 2048
CHUNK = 64
WINDOW = 128
SWA_HEADS = 16
SWA_KV_HEADS = 4
SWA_GROUP = SWA_HEADS // SWA_KV_HEADS
SWA_HEAD_DIM = 64
RET_HEADS = 8
RET_DIM = 128
RET_ROPE_BASE = 10000.0
N_MEM = 256
MEM_HEADS = 4
MEM_HEAD_DIM = 256
D_FF = 5632
N_BRANCH = 3
EPS = 1e-6
NEG = -1e30
PAST_LEN = 1024

SWA_Q = SWA_HEADS * SWA_HEAD_DIM
SWA_KV = SWA_KV_HEADS * SWA_HEAD_DIM
RET_W = RET_HEADS * RET_DIM
MEM_W = MEM_HEADS * MEM_HEAD_DIM
LANE = 128
CONV_ROWS = 64

def _sigmoid(x):
    return 1.0 / (1.0 + jnp.exp(-x))


def _rms(x, g):
    return x * lax.rsqrt(jnp.mean(x * x, axis=-1, keepdims=True) + EPS) * g


def _rmsnorm_kernel(x_ref, g_ref, o_ref):
    o_ref[...] = _rms(x_ref[...], g_ref[...]).astype(o_ref.dtype)


def _rmsnorm(x, g, tm):
    n, d = x.shape
    return pl.pallas_call(
        _rmsnorm_kernel,
        grid=(n // tm,),
        in_specs=[pl.BlockSpec((tm, d), lambda i: (i, 0)),
                  pl.BlockSpec((1, d), lambda i: (0, 0))],
        out_specs=pl.BlockSpec((tm, d), lambda i: (i, 0)),
        out_shape=jax.ShapeDtypeStruct((n, d), BF16),
        name="rmsnorm",
    )(x, g.reshape(1, d))


def _mm_kernel(x_ref, w_ref, *rest, epilogue, tn):
    o_ref = rest[-1]
    acc = jnp.dot(x_ref[...], w_ref[...], preferred_element_type=F32)
    if epilogue == "plain":
        out = acc
    elif epilogue == "scale":
        out = acc * rest[0][...]
    elif epilogue == "silu":
        out = acc * _sigmoid(acc)
    elif epilogue == "sigmoid_bias":
        out = _sigmoid(acc + rest[0][...])
    elif epilogue == "residual":
        out = rest[0][...] + acc
    elif epilogue == "rotary":
        cc, ss = rest[1][...], rest[2][...]
        pieces = []
        for j in range(tn // LANE):
            xh = acc[:, j * LANE:(j + 1) * LANE]
            pieces.append(xh * cc + pltpu.roll(xh, LANE // 2, 1) * ss)
        out = jnp.concatenate(pieces, axis=1) * rest[0][...]
    else:
        raise ValueError(epilogue)
    o_ref[...] = out.astype(o_ref.dtype)


def _mm(x, w, *, epilogue, out_dtype, tm, tn, colvec=None, residual=None, tables=None, name):
    n, k = x.shape
    ncols = w.shape[1]
    in_specs = [pl.BlockSpec((tm, k), lambda i, j: (i, 0)),
                pl.BlockSpec((k, tn), lambda i, j: (0, j))]
    args = [x, w]
    if colvec is not None:
        in_specs.append(pl.BlockSpec((1, tn), lambda i, j: (0, j)))
        args.append(colvec.reshape(1, ncols).astype(F32))
    if residual is not None:
        in_specs.append(pl.BlockSpec((tm, tn), lambda i, j: (i, j)))
        args.append(residual)
    if tables is not None:
        nblk = tables[0].shape[0] // tm
        for t in tables:
            in_specs.append(pl.BlockSpec((tm, LANE), lambda i, j: (i % nblk, 0)))
            args.append(t)
    return pl.pallas_call(
        functools.partial(_mm_kernel, epilogue=epilogue, tn=tn),
        grid=(n // tm, ncols // tn),
        in_specs=in_specs,
        out_specs=pl.BlockSpec((tm, tn), lambda i, j: (i, j)),
        out_shape=jax.ShapeDtypeStruct((n, ncols), out_dtype),
        compiler_params=pltpu.CompilerParams(dimension_semantics=("parallel", "parallel")),
        name=name,
    )(*args)


def _norm_mm_kernel(x_ref, g_ref, w_ref, cv_ref, o_ref, u_ref):
    @pl.when(pl.program_id(1) == 0)
    def _():
        u_ref[...] = _rms(x_ref[...], g_ref[...]).astype(u_ref.dtype)

    acc = jnp.dot(u_ref[...], w_ref[...], preferred_element_type=F32)
    o_ref[...] = (acc * cv_ref[...]).astype(o_ref.dtype)


def _norm_mm(x, g, w, colvec, *, tm, tn, name):
    n, k = x.shape
    ncols = w.shape[1]
    return pl.pallas_call(
        _norm_mm_kernel,
        grid=(n // tm, ncols // tn),
        in_specs=[pl.BlockSpec((tm, k), lambda i, j: (i, 0)),
                  pl.BlockSpec((1, k), lambda i, j: (0, 0)),
                  pl.BlockSpec((k, tn), lambda i, j: (0, j)),
                  pl.BlockSpec((1, tn), lambda i, j: (0, j))],
        out_specs=[pl.BlockSpec((tm, tn), lambda i, j: (i, j)),
                   pl.BlockSpec((tm, k), lambda i, j: (i, 0))],
        out_shape=[jax.ShapeDtypeStruct((n, ncols), BF16), jax.ShapeDtypeStruct((n, k), BF16)],
        compiler_params=pltpu.CompilerParams(dimension_semantics=("parallel", "arbitrary")),
        name=name,
    )(x, g.reshape(1, k), w, colvec.reshape(1, ncols).astype(F32))


def _swa_kernel(sink_ref, q_ref, kp_ref, kc_ref, vp_ref, vc_ref, o_ref, *scratch, qt, kv_off):
    s_scs, p_scs = scratch[:SWA_KV_HEADS], scratch[SWA_KV_HEADS:]
    i = pl.program_id(1)
    nq = qt // CHUNK
    w = (nq + 2) * CHUNK
    wp = -(-w // LANE) * LANE
    first_ok = jnp.where(i * qt + kv_off >= WINDOW, 0, 2)

    def window(p_ref, c_ref):
        parts = [p_ref[0], c_ref[0]]
        if wp > w:
            parts.append(jnp.zeros((wp - w, SWA_KV), F32))
        return jnp.concatenate(parts, axis=0)

    kwin = window(kp_ref, kc_ref)
    vwin = window(vp_ref, vc_ref)
    lane = lax.broadcasted_iota(jnp.int32, (wp, LANE), 1)
    lane_g = lax.broadcasted_iota(jnp.int32, (wp, SWA_KV), 1) // SWA_HEAD_DIM
    out_g = lax.broadcasted_iota(jnp.int32, (CHUNK, SWA_KV), 1) // SWA_HEAD_DIM

    def block_diag(win, h):
        col, half = divmod(h, 2)
        x = win[:, col * LANE:(col + 1) * LANE]
        xr = pltpu.roll(x, SWA_HEAD_DIM, 1)
        lo = lane < SWA_HEAD_DIM
        r = jnp.where(lo, x, xr) if half == 0 else jnp.where(lo, xr, x)
        r2 = jnp.concatenate([r, r], axis=1).astype(BF16)
        zero = jnp.zeros_like(r2)
        return jnp.concatenate([jnp.where(lane_g == g, r2, zero) for g in range(SWA_GROUP)], axis=0)

    for h in range(SWA_KV_HEADS):
        bk = block_diag(kwin, h)
        bv = block_diag(vwin, h)
        qh = q_ref[0, :, h * SWA_KV:(h + 1) * SWA_KV]
        s_sc, p_sc = s_scs[h], p_scs[h]
        s_sc[...] = lax.dot_general(qh, bk, (((1,), (1,)), ((), ())), preferred_element_type=F32)
        invs = []
        for jq in range(nq):
            rows = slice(jq * CHUNK, (jq + 1) * CHUNK)
            c_lo, c_hi = jq * CHUNK // LANE, ((jq + 3) * CHUNK - 1) // LANE
            width = (c_hi - c_lo + 1) * LANE
            kc = c_lo * (LANE // CHUNK) + lax.broadcasted_iota(jnp.int32, (CHUNK, width), 1) // CHUNK
            visible = (kc >= jq) & (kc <= jq + 2) & (kc >= first_ok)
            inv = jnp.zeros((CHUNK, SWA_KV), F32)
            for g in range(SWA_GROUP):
                l0 = g * wp + c_lo * LANE
                sg = jnp.where(visible, s_sc[rows, l0:l0 + width], NEG)
                sk = sink_ref[h * SWA_GROUP + g]
                m = jnp.maximum(jnp.max(sg, axis=1, keepdims=True), sk)
                p = jnp.exp(sg - m)
                den = jnp.sum(p, axis=1, keepdims=True) + jnp.exp(sk - m)
                p_sc[rows, l0:l0 + width] = p.astype(BF16)
                for c in range(wp // LANE):
                    if not c_lo <= c <= c_hi:
                        p_sc[rows, g * wp + c * LANE:g * wp + (c + 1) * LANE] = jnp.zeros((CHUNK, LANE), BF16)
                inv = jnp.where(out_g == g, 1.0 / den, inv)
            invs.append(inv)
        o = jnp.dot(p_sc[...], bv, preferred_element_type=F32)
        o_ref[0, :, h * SWA_KV:(h + 1) * SWA_KV] = (o * jnp.concatenate(invs, axis=0)).astype(o_ref.dtype)


def _swa(q, q_col, k, k_col, v, v_col, sink, *, sq, qt, kv_off):
    b = q.shape[0]
    cur_off = kv_off // qt
    prev_off = kv_off // WINDOW - 1
    per = qt // WINDOW if qt >= WINDOW else 0
    wp = -(-(qt + WINDOW) // LANE) * LANE

    def prev_map(kcol):
        return lambda bi, i, s: (bi, jnp.maximum(i * per + prev_off, 0), kcol)

    def cur_map(kcol):
        return lambda bi, i, s: (bi, i + cur_off, kcol)

    grid_spec = pltpu.PrefetchScalarGridSpec(
        num_scalar_prefetch=1,
        grid=(b, sq // qt),
        in_specs=[pl.BlockSpec((1, qt, SWA_Q), lambda bi, i, s: (bi, i, q_col)),
                  pl.BlockSpec((1, WINDOW, SWA_KV), prev_map(k_col)),
                  pl.BlockSpec((1, qt, SWA_KV), cur_map(k_col)),
                  pl.BlockSpec((1, WINDOW, SWA_KV), prev_map(v_col)),
                  pl.BlockSpec((1, qt, SWA_KV), cur_map(v_col))],
        out_specs=pl.BlockSpec((1, qt, SWA_Q), lambda bi, i, s: (bi, i, 0)),
        scratch_shapes=([pltpu.VMEM((qt, SWA_GROUP * wp), F32)] * SWA_KV_HEADS
                        + [pltpu.VMEM((qt, SWA_GROUP * wp), BF16)] * SWA_KV_HEADS),
    )
    return pl.pallas_call(
        functools.partial(_swa_kernel, qt=qt, kv_off=kv_off),
        grid_spec=grid_spec,
        out_shape=jax.ShapeDtypeStruct((b, sq, SWA_Q), BF16),
        compiler_params=pltpu.CompilerParams(dimension_semantics=("parallel", "parallel")),
        name="swa",
    )(sink.astype(F32), q, k, k, v, v)


def _ret_kernel(lg_ref, q_ref, k_ref, v_ref, g_ref, s0_ref, o_ref, st_ref, dec_ref, xi_ref, zeta_ref, *, c):
    bi = pl.program_id(0)
    ci = pl.program_id(1)

    @pl.when((bi == 0) & (ci == 0))
    def _():
        n = lax.broadcasted_iota(jnp.int32, (c, c), 0)
        m = lax.broadcasted_iota(jnp.int32, (c, c), 1)
        diff = (n - m).astype(F32)
        nrow = lax.broadcasted_iota(jnp.int32, (c, RET_DIM), 0).astype(F32)
        for h in range(RET_HEADS):
            dec_ref[h] = jnp.where(diff >= 0, jnp.exp(lg_ref[h] * jnp.maximum(diff, 0.0)), 0.0)
            xi_ref[h] = jnp.exp(lg_ref[h] * (nrow + 1.0))
            zeta_ref[h] = jnp.exp(lg_ref[h] * (float(c) - 1.0 - nrow))

    @pl.when(ci == 0)
    def _():
        st_ref[...] = s0_ref[...]

    for h in range(RET_HEADS):
        lg = lg_ref[h]
        sl = slice(h * RET_DIM, (h + 1) * RET_DIM)
        qh = q_ref[0, :, sl]
        kh = k_ref[0, :, sl]
        vh = v_ref[0, :, sl]
        state = st_ref[0, h]
        inner = lax.dot_general(qh, kh, (((1,), (1,)), ((), ())), preferred_element_type=F32) * dec_ref[h]
        o = jnp.dot(inner.astype(BF16), vh, preferred_element_type=F32)
        cross = jnp.dot(qh, state.astype(BF16), preferred_element_type=F32)
        o = o + cross * xi_ref[h]
        kz = (kh.astype(F32) * zeta_ref[h]).astype(BF16)
        upd = lax.dot_general(kz, vh, (((0,), (0,)), ((), ())), preferred_element_type=F32)
        decay_c = jnp.exp(lg * jnp.full((1, RET_DIM), float(c), F32))
        st_ref[0, h] = decay_c * state + upd
        on = o * lax.rsqrt(jnp.mean(o * o, axis=-1, keepdims=True) + EPS)
        o_ref[0, :, sl] = (g_ref[0, :, sl].astype(F32) * on).astype(o_ref.dtype)


def _retention(qk, q_col, k_col, v, v_col, gate, state0, log_g, *, s, c):
    b = qk.shape[0]

    def seq(col):
        return pl.BlockSpec((1, c, RET_W), lambda bi, ci, lg: (bi, ci, col))

    st_spec = pl.BlockSpec((1, RET_HEADS, RET_DIM, RET_DIM), lambda bi, ci, lg: (bi, 0, 0, 0))
    grid_spec = pltpu.PrefetchScalarGridSpec(
        num_scalar_prefetch=1,
        grid=(b, s // c),
        in_specs=[seq(q_col), seq(k_col), seq(v_col), seq(0), st_spec],
        out_specs=[seq(0), st_spec],
        scratch_shapes=[pltpu.VMEM((RET_HEADS, c, c), F32), pltpu.VMEM((RET_HEADS, c, RET_DIM), F32),
                        pltpu.VMEM((RET_HEADS, c, RET_DIM), F32)],
    )
    return pl.pallas_call(
        functools.partial(_ret_kernel, c=c),
        grid_spec=grid_spec,
        out_shape=[jax.ShapeDtypeStruct((b, s, RET_W), BF16),
                   jax.ShapeDtypeStruct((b, RET_HEADS, RET_DIM, RET_DIM), F32)],
        compiler_params=pltpu.CompilerParams(dimension_semantics=("arbitrary", "arbitrary")),
        name="retention",
    )(log_g, qk, qk, v, gate, state0)


def _mem_kernel(q_ref, k_ref, v_ref, o_ref):
    for h in range(MEM_HEADS):
        sl = slice(h * MEM_HEAD_DIM, (h + 1) * MEM_HEAD_DIM)
        qh = q_ref[0, :, sl]
        kh = k_ref[0, :, sl].astype(BF16)
        vh = v_ref[0, :, sl].astype(BF16)
        s = lax.dot_general(qh, kh, (((1,), (1,)), ((), ())), preferred_element_type=F32)
        m = jnp.max(s, axis=1, keepdims=True)
        p = jnp.exp(s - m)
        den = jnp.sum(p, axis=1, keepdims=True)
        o = jnp.dot(p.astype(BF16), vh, preferred_element_type=F32)
        o_ref[0, :, sl] = (o * (1.0 / den)).astype(o_ref.dtype)


def _mem_attend(q, q_col, mk, mv, *, s, tq):
    b = q.shape[0]
    kv_spec = pl.BlockSpec((1, N_MEM, MEM_W), lambda bi, i: (bi, 0, 0))
    return pl.pallas_call(
        _mem_kernel,
        grid=(b, s // tq),
        in_specs=[pl.BlockSpec((1, tq, MEM_W), lambda bi, i: (bi, i, q_col)), kv_spec, kv_spec],
        out_specs=pl.BlockSpec((1, tq, MEM_W), lambda bi, i: (bi, i, 0)),
        out_shape=jax.ShapeDtypeStruct((b, s, MEM_W), BF16),
        compiler_params=pltpu.CompilerParams(dimension_semantics=("parallel", "parallel")),
        name="mem_attend",
    )(q, mk, mv)


def _merge_kernel(u_ref, oa_ref, ob_ref, oc_ref, wga_ref, wgb_ref, wgc_ref, ba_ref, bb_ref, bc_ref,
                  wa_ref, wb_ref, wc_ref, o_ref):
    u = u_ref[...]
    acc = None
    for o_r, wg_ref, b_ref, w_ref in ((oa_ref, wga_ref, ba_ref, wa_ref), (ob_ref, wgb_ref, bb_ref, wb_ref),
                                      (oc_ref, wgc_ref, bc_ref, wc_ref)):
        gate = _sigmoid(jnp.dot(u, wg_ref[...], preferred_element_type=F32) + b_ref[...])
        term = gate * jnp.dot(o_r[...], w_ref[...], preferred_element_type=F32)
        acc = term if acc is None else acc + term
    o_ref[...] = acc.astype(o_ref.dtype)


def _merge(u, o_swa, o_ret, o_mem, w_gate, b_gate, w_br, *, tm, tn):
    n, d = u.shape
    kb = o_swa.shape[1]
    nj = D_MODEL // tn
    o_spec = pl.BlockSpec((tm, kb), lambda i, j: (i, 0))

    def wg_spec(br):
        return pl.BlockSpec((d, tn), lambda i, j: (0, br * nj + j))

    def b_spec(br):
        return pl.BlockSpec((1, tn), lambda i, j: (0, br * nj + j))

    def w_spec(br):
        return pl.BlockSpec((kb, tn), lambda i, j: (br, j))

    bias = b_gate.reshape(1, -1).astype(F32)
    return pl.pallas_call(
        _merge_kernel,
        grid=(n // tm, nj),
        in_specs=[pl.BlockSpec((tm, d), lambda i, j: (i, 0)), o_spec, o_spec, o_spec,
                  wg_spec(0), wg_spec(1), wg_spec(2), b_spec(0), b_spec(1), b_spec(2),
                  w_spec(0), w_spec(1), w_spec(2)],
        out_specs=pl.BlockSpec((tm, tn), lambda i, j: (i, j)),
        out_shape=jax.ShapeDtypeStruct((n, D_MODEL), BF16),
        compiler_params=pltpu.CompilerParams(dimension_semantics=("parallel", "parallel")),
        name="merge",
    )(u, o_swa, o_ret, o_mem, w_gate, w_gate, w_gate, bias, bias, bias, w_br, w_br, w_br)


def _ffn_kernel(h_ref, hj_ref, gf_ref, wug_ref, wuv_ref, wc_ref, b_ref, wd_ref, cb_ref, gfin_ref, y_ref, co_ref,
                u_sc, act_sc, ag_sc, av_sc, y_sc, cg_sc, cv_sc, *, tm, tf, td, nf, seg, nseg):
    i = pl.program_id(1)
    t = pl.program_id(2)
    nd = D_MODEL // td

    seg_t = tm // nseg
    stride = seg_t + 8

    def up_proj():
        u = u_sc[...]
        for a_sc, wu_ref in ((ag_sc, wug_ref), (av_sc, wuv_ref)):
            a = jnp.dot(u, wu_ref[...], preferred_element_type=F32)
            for s in range(nseg):
                a_sc[s * stride + 8:(s + 1) * stride, :] = a[s * seg_t:(s + 1) * seg_t]

    def conv_act():
        fb = t - 1
        for a_sc, blk, carry_sc in ((ag_sc, fb, cg_sc), (av_sc, nf + fb, cv_sc)):
            if nseg == 1:
                @pl.when(i == 0)
                def _():
                    a_sc[6:8, :] = cb_ref[0, blk]

                @pl.when(i > 0)
                def _():
                    a_sc[6:8, :] = carry_sc[fb, 6:8, :]

                carry_sc[fb] = a_sc[tm:tm + 8, :]
            else:
                for s in range(nseg):
                    a_sc[s * stride + 6:s * stride + 8, :] = cb_ref[s, blk]
            for s in range(nseg):
                co_ref[0, s, blk] = a_sc[(s + 1) * stride - 8:(s + 1) * stride, :]

        def conv(a_sc, blk, r0, c0):
            tap = lambda k: a_sc[r0 - k:r0 - k + CONV_ROWS, c0:c0 + LANE]
            wc = lambda k: wc_ref[blk, k:k + 1, c0:c0 + LANE]
            return b_ref[blk, :, c0:c0 + LANE] + (wc(0) * tap(2) + wc(1) * tap(1) + wc(2) * tap(0))

        for c0 in range(0, tf, LANE):
            for s in range(nseg):
                for r in range(0, seg_t, CONV_ROWS):
                    cg = conv(ag_sc, fb, s * stride + 8 + r, c0)
                    cv = conv(av_sc, nf + fb, s * stride + 8 + r, c0)
                    act_sc[fb, s * seg_t + r:s * seg_t + r + CONV_ROWS, c0:c0 + LANE] = (
                        cg * _sigmoid(cg) * cv).astype(BF16)

    @pl.when(t == 0)
    def _():
        u_sc[...] = _rms(h_ref[0], gf_ref[...]).astype(BF16)
        up_proj()

    @pl.when((t >= 1) & (t < nf))
    def _():
        conv_act()
        up_proj()

    def down(j):
        act = jnp.concatenate([act_sc[f] for f in range(nf)], axis=1)
        y_sc[j] = hj_ref[0] + jnp.dot(act, wd_ref[...], preferred_element_type=F32)

    @pl.when(t == nf)
    def _():
        conv_act()
        down(0)

    @pl.when(t > nf)
    def _():
        down(t - nf)

    @pl.when(t == nf + nd - 1)
    def _():
        full = jnp.concatenate([y_sc[k] for k in range(nd)], axis=1)
        y_ref[0] = _rms(full, gfin_ref[...])


def _ffn(h, conv_buf, g_ffn, w_up, w_conv, b_conv, w_down, g_final, *, tm, tf, td, seg, nseg):
    bt, rows, d = h.shape
    nf = D_FF // tf
    nd = d // td
    nt = rows // tm
    nb = bt * nseg
    up_blk = lambda t: jnp.minimum(t, nf - 1)
    down_blk = lambda t: jnp.clip(t - nf, 0, nd - 1)
    wc_blocks = w_conv.reshape(3, 2 * nf, tf).transpose(1, 0, 2)
    b_blocks = b_conv.reshape(2 * nf, 1, tf)
    cb_blocks = conv_buf.reshape(nb, 2, 2 * nf, tf).transpose(0, 2, 1, 3)
    whole = lambda *shape: pl.BlockSpec(shape, lambda b, i, t: (0,) * len(shape))
    in_specs = [
        pl.BlockSpec((1, tm, d), lambda b, i, t: (b, i, 0)),
        pl.BlockSpec((1, tm, td), lambda b, i, t: (b, i, down_blk(t))),
        whole(1, d),
        pl.BlockSpec((d, tf), lambda b, i, t: (0, up_blk(t))),
        pl.BlockSpec((d, tf), lambda b, i, t: (0, nf + up_blk(t))),
        whole(2 * nf, 3, tf),
        whole(2 * nf, 1, tf),
        pl.BlockSpec((D_FF, td), lambda b, i, t: (0, down_blk(t))),
        pl.BlockSpec((nseg, 2 * nf, 2, tf), lambda b, i, t: (b, 0, 0, 0)),
        whole(1, d),
    ]
    out_specs = [
        pl.BlockSpec((1, tm, d), lambda b, i, t: (b, i, 0)),
        pl.BlockSpec((1, nseg, 2 * nf, 8, tf), lambda b, i, t: (b * nt + i, 0, 0, 0, 0)),
    ]
    return pl.pallas_call(
        functools.partial(_ffn_kernel, tm=tm, tf=tf, td=td, nf=nf, seg=seg, nseg=nseg),
        grid=(bt, nt, nf + nd),
        in_specs=in_specs,
        out_specs=out_specs,
        out_shape=[jax.ShapeDtypeStruct((bt, rows, d), F32),
                   jax.ShapeDtypeStruct((bt * nt, nseg, 2 * nf, 8, tf), F32)],
        scratch_shapes=[pltpu.VMEM((tm, d), BF16), pltpu.VMEM((nf, tm, tf), BF16),
                        pltpu.VMEM((tm + 8 * nseg, tf), F32), pltpu.VMEM((tm + 8 * nseg, tf), F32),
                        pltpu.VMEM((nd, tm, td), F32),
                        pltpu.VMEM((nf, 8, tf), F32), pltpu.VMEM((nf, 8, tf), F32)],
        compiler_params=pltpu.CompilerParams(dimension_semantics=("arbitrary", "arbitrary", "arbitrary")),
        name="conv_ffn",
    )(h, h, g_ffn.reshape(1, d), w_up, w_up, wc_blocks, b_blocks, w_down, cb_blocks, g_final.reshape(1, d))


def _ffn_up_kernel(u_ref, wug_ref, wuv_ref, wc_ref, b_ref, cb_ref, act_ref, co_ref,
                   ag_sc, av_sc, cg_sc, cv_sc, *, tm, tf, nf, nseg):
    i = pl.program_id(1)
    fb = pl.program_id(2)
    seg_t = tm // nseg
    stride = seg_t + 8
    if nseg == 1:
        @pl.when((i == 0) & (fb == 0))
        def _():
            for f in range(nf):
                cg_sc[f, 6:8, :] = cb_ref[0, f]
                cv_sc[f, 6:8, :] = cb_ref[0, nf + f]

    u = u_ref[0]
    for a_sc, wu_ref, blk, carry_sc in ((ag_sc, wug_ref, fb, cg_sc), (av_sc, wuv_ref, nf + fb, cv_sc)):
        a = jnp.dot(u, wu_ref[...], preferred_element_type=F32)
        for s in range(nseg):
            a_sc[s * stride + 8:(s + 1) * stride, :] = a[s * seg_t:(s + 1) * seg_t]
            co_ref[0, s, blk] = a[(s + 1) * seg_t - 8:(s + 1) * seg_t]
        if nseg == 1:
            a_sc[6:8, :] = carry_sc[fb, 6:8, :]
            carry_sc[fb] = a[tm - 8:tm]
        else:
            for s in range(nseg):
                a_sc[s * stride + 6:s * stride + 8, :] = cb_ref[s, blk]

    def conv(a_sc, blk, r0, c0):
        tap = lambda k: a_sc[r0 - k:r0 - k + CONV_ROWS, c0:c0 + LANE]
        wc = lambda k: wc_ref[blk, k:k + 1, c0:c0 + LANE]
        return b_ref[blk, :, c0:c0 + LANE] + (wc(0) * tap(2) + wc(1) * tap(1) + wc(2) * tap(0))

    for c0 in range(0, tf, LANE):
        for s in range(nseg):
            for r in range(0, seg_t, CONV_ROWS):
                cg = conv(ag_sc, fb, s * stride + 8 + r, c0)
                cv = conv(av_sc, nf + fb, s * stride + 8 + r, c0)
                act_ref[0, s * seg_t + r:s * seg_t + r + CONV_ROWS, c0:c0 + LANE] = (
                    cg * _sigmoid(cg) * cv).astype(act_ref.dtype)


def _ffn_up(u, conv_buf, w_up, w_conv, b_conv, *, tm, tf, nseg):
    bt, rows, d = u.shape
    nf = D_FF // tf
    nt = rows // tm
    nb = bt * nseg
    wc_blocks = w_conv.reshape(3, 2 * nf, tf).transpose(1, 0, 2)
    b_blocks = b_conv.reshape(2 * nf, 1, tf)
    cb_blocks = conv_buf.reshape(nb, 2, 2 * nf, tf).transpose(0, 2, 1, 3)
    whole = lambda *shape: pl.BlockSpec(shape, lambda b, i, f: (0,) * len(shape))
    return pl.pallas_call(
        functools.partial(_ffn_up_kernel, tm=tm, tf=tf, nf=nf, nseg=nseg),
        grid=(bt, nt, nf),
        in_specs=[pl.BlockSpec((1, tm, d), lambda b, i, f: (b, i, 0)),
                  pl.BlockSpec((d, tf), lambda b, i, f: (0, f)),
                  pl.BlockSpec((d, tf), lambda b, i, f: (0, nf + f)),
                  whole(2 * nf, 3, tf),
                  whole(2 * nf, 1, tf),
                  pl.BlockSpec((nseg, 2 * nf, 2, tf), lambda b, i, f: (b, 0, 0, 0))],
        out_specs=[pl.BlockSpec((1, tm, tf), lambda b, i, f: (b, i, f)),
                   pl.BlockSpec((1, nseg, 2 * nf, 8, tf), lambda b, i, f: (b * nt + i, 0, 0, 0, 0))],
        out_shape=[jax.ShapeDtypeStruct((bt, rows, D_FF), BF16),
                   jax.ShapeDtypeStruct((bt * nt, nseg, 2 * nf, 8, tf), F32)],
        scratch_shapes=[pltpu.VMEM((tm + 8 * nseg, tf), F32), pltpu.VMEM((tm + 8 * nseg, tf), F32),
                        pltpu.VMEM((nf, 8, tf), F32), pltpu.VMEM((nf, 8, tf), F32)],
        compiler_params=pltpu.CompilerParams(dimension_semantics=("arbitrary", "arbitrary", "arbitrary")),
        name="ffn_up",
    )(u, w_up, w_up, wc_blocks, b_blocks, cb_blocks)


def _ffn_down_kernel(act_ref, wd_ref, h_ref, g_ref, y_ref, y_sc, *, nd):
    j = pl.program_id(1)
    y_sc[j] = h_ref[...] + jnp.dot(act_ref[...], wd_ref[...], preferred_element_type=F32)

    @pl.when(j == nd - 1)
    def _():
        full = jnp.concatenate([y_sc[k] for k in range(nd)], axis=1)
        y_ref[...] = _rms(full, g_ref[...])


def _ffn_down(act, w_down, h, g_final, *, tm, td):
    n, f = act.shape
    d = h.shape[1]
    nd = d // td
    return pl.pallas_call(
        functools.partial(_ffn_down_kernel, nd=nd),
        grid=(n // tm, nd),
        in_specs=[pl.BlockSpec((tm, f), lambda i, j: (i, 0)),
                  pl.BlockSpec((f, td), lambda i, j: (0, j)),
                  pl.BlockSpec((tm, td), lambda i, j: (i, j)),
                  pl.BlockSpec((1, d), lambda i, j: (0, 0))],
        out_specs=pl.BlockSpec((tm, d), lambda i, j: (i, 0)),
        out_shape=jax.ShapeDtypeStruct((n, d), F32),
        scratch_shapes=[pltpu.VMEM((nd, tm, td), F32)],
        compiler_params=pltpu.CompilerParams(dimension_semantics=("parallel", "arbitrary")),
        name="ffn_down",
    )(act, w_down, h, g_final.reshape(1, d))


def _out_proj_kernel(x_ref, w_ref, res_ref, g_ref, h_ref, u_ref):
    h = res_ref[...] + jnp.dot(x_ref[...], w_ref[...], preferred_element_type=F32)
    h_ref[...] = h
    u_ref[...] = _rms(h, g_ref[...]).astype(u_ref.dtype)


def _out_proj(x, w, residual, g, *, tm):
    n, k = x.shape
    d = w.shape[1]
    row = lambda width: pl.BlockSpec((tm, width), lambda i: (i, 0))
    return pl.pallas_call(
        _out_proj_kernel,
        grid=(n // tm,),
        in_specs=[row(k), pl.BlockSpec((k, d), lambda i: (0, 0)), row(d), pl.BlockSpec((1, d), lambda i: (0, 0))],
        out_specs=[row(d), row(d)],
        out_shape=[jax.ShapeDtypeStruct((n, d), F32), jax.ShapeDtypeStruct((n, d), BF16)],
        compiler_params=pltpu.CompilerParams(dimension_semantics=("parallel",)),
        name="proj_out",
    )(x, w, residual, g.reshape(1, d))


def _rope_tables(pos0, s):
    half = RET_DIM // 2
    inv_freq = 1.0 / (RET_ROPE_BASE ** jnp.linspace(0.0, 1.0, half, dtype=F32))
    ang = (pos0 + jnp.arange(s)).astype(F32)[:, None] * inv_freq[None, :]
    cos, sin = jnp.cos(ang), jnp.sin(ang)
    return jnp.concatenate([cos, cos], axis=1), jnp.concatenate([-sin, sin], axis=1)


def _prep_weights(g_mix, w_in, b_gate, sink, w_br, w_o, g_ffn, w_up, w_conv, b_conv, w_down):
    o = 0
    cuts = {}
    for name, width in (("qa", SWA_Q), ("ka", SWA_KV), ("va", SWA_KV), ("qr", RET_W), ("kr", RET_W),
                        ("vr", RET_W), ("gr", RET_W), ("qm", MEM_W), ("gl", N_BRANCH * D_MODEL)):
        cuts[name] = (o, o + width)
        o += width
    cols = lambda *names: jnp.concatenate([w_in[:, cuts[n][0]:cuts[n][1]] for n in names], axis=1).astype(BF16)
    ones = lambda n, v: jnp.full((n,), v, F32)
    return dict(
        g_mix=g_mix, b_gate=b_gate, sink=sink, g_ffn=g_ffn, w_conv=w_conv, b_conv=b_conv,
        w_plain=cols("qa", "vr", "qm"),
        s_plain=jnp.concatenate([ones(SWA_Q, SWA_HEAD_DIM ** -0.5), ones(RET_W, 1.0),
                                 ones(MEM_W, MEM_HEAD_DIM ** -0.5)]),
        w_kv=cols("ka", "va"),
        w_rot=cols("qr", "kr"),
        s_rot=jnp.concatenate([ones(RET_W, 1.0), ones(RET_W, RET_DIM ** -0.5)]),
        w_g=cols("gr"),
        w_gate=cols("gl"),
        w_br=w_br.astype(BF16), w_o=w_o.astype(BF16), w_up=w_up.astype(BF16), w_down=w_down.astype(BF16),
    )


def _run_group(x, pos0, mem_k, mem_v, swa_cache, ret_state, conv_buf, wts, g_final, log_g):
    b, s, d = x.shape
    n = b * s
    tm = min(1024, n)
    x2 = x.reshape(n, d)
    plain, u = _norm_mm(x2, wts["g_mix"], wts["w_plain"], wts["s_plain"], tm=tm, tn=1024, name="proj_plain")
    kv = _mm(u, wts["w_kv"], epilogue="plain", out_dtype=F32, tm=tm, tn=2 * SWA_KV, name="proj_kv")
    cc, ss = _rope_tables(pos0, s)
    if s < tm:
        cc, ss = jnp.tile(cc, (tm // s, 1)), jnp.tile(ss, (tm // s, 1))
    rot = _mm(u, wts["w_rot"], epilogue="rotary", out_dtype=BF16, tm=tm, tn=1024,
              colvec=wts["s_rot"], tables=(cc, ss), name="proj_rot")
    sgate = _mm(u, wts["w_g"], epilogue="silu", out_dtype=BF16, tm=tm, tn=1024, name="proj_silu")

    plain3 = plain.reshape(b, s, -1)
    kv3 = kv.reshape(b, s, 2 * SWA_KV)
    if swa_cache is None:
        o_swa = _swa(plain3, 0, kv3, 0, kv3, 1, wts["sink"], sq=s, qt=min(256, s), kv_off=0)
        new_k = kv3[:, s - WINDOW:, :SWA_KV]
        new_v = kv3[:, s - WINDOW:, SWA_KV:]
    else:
        ck = swa_cache[0].reshape(b, -1, SWA_KV)
        cv = swa_cache[1].reshape(b, -1, SWA_KV)
        n_keep = ck.shape[1]
        k_all = jnp.concatenate([ck, kv3[:, :, :SWA_KV]], axis=1)
        v_all = jnp.concatenate([cv, kv3[:, :, SWA_KV:]], axis=1)
        o_swa = _swa(plain3, 0, k_all, 0, v_all, 0, wts["sink"], sq=s, qt=s, kv_off=n_keep)
        new_k = k_all[:, -n_keep:]
        new_v = v_all[:, -n_keep:]
    state0 = jnp.zeros((b, RET_HEADS, RET_DIM, RET_DIM), F32) if ret_state is None else ret_state
    o_ret, s_new = _retention(rot.reshape(b, s, -1), 0, 1, plain3, 1, sgate.reshape(b, s, -1), state0, log_g,
                              s=s, c=min(256, s))
    o_mem = _mem_attend(plain3, 2, mem_k.reshape(b, N_MEM, MEM_W), mem_v.reshape(b, N_MEM, MEM_W),
                        s=s, tq=min(512, s))
    merged = _merge(u, o_swa.reshape(n, -1), o_ret.reshape(n, -1), o_mem.reshape(n, -1), wts["w_gate"],
                    wts["b_gate"], wts["w_br"], tm=tm, tn=512)
    h1, u2 = _out_proj(merged, wts["w_o"], x2, wts["g_ffn"], tm=min(512, n))
    if s >= 1024:
        act, co = _ffn_up(u2.reshape(b, s, d), conv_buf, wts["w_up"], wts["w_conv"], wts["b_conv"],
                          tm=1024, tf=512, nseg=1)
    else:
        act, co = _ffn_up(u2.reshape(1, n, d), conv_buf, wts["w_up"], wts["w_conv"], wts["b_conv"],
                          tm=n, tf=512, nseg=b)
    y = _ffn_down(act.reshape(n, D_FF), wts["w_down"], h1, g_final, tm=min(512, n), td=512)
    nblk, tfb = co.shape[2], co.shape[4]
    last = co.reshape(b, -1, nblk, 8, tfb)[:, -1, :, 6:8, :]
    new_buf = last.transpose(0, 2, 1, 3).reshape(b, 2, nblk * tfb)
    return (y.reshape(b, s, d), new_k.reshape(b, -1, SWA_KV_HEADS, SWA_HEAD_DIM),
            new_v.reshape(b, -1, SWA_KV_HEADS, SWA_HEAD_DIM), s_new, new_buf)


def _memory_kv(mem, g_mem, w_mem_kv):
    b, m, d = mem.shape
    u = _rmsnorm(mem.reshape(b * m, d), g_mem, min(512, b * m))
    kv = _mm(u, w_mem_kv.astype(BF16), epilogue="plain", out_dtype=F32, tm=min(1024, b * m), tn=1024, name="mem_kv")
    return (kv[:, :MEM_W].reshape(b, m, MEM_HEADS, MEM_HEAD_DIM), kv[:, MEM_W:].reshape(b, m, MEM_HEADS, MEM_HEAD_DIM))


def kernel(x_prompt, x_sample, mem_prompt, cache_swa_k, cache_swa_v, state_ret, state_ffn_conv, cache_mem_k, cache_mem_v, g_mix, w_in, b_gate, sink, w_br, w_o, g_mem, w_mem_kv, g_ffn, w_up, w_conv, b_conv, w_down, g_final):
    bp = x_prompt.shape[0]
    depth = w_in.shape[0]
    log_g = jnp.log1p(-jnp.exp2(-5.0 - jnp.arange(RET_HEADS, dtype=F32)))
    hp, hs = x_prompt, x_sample
    outs_p = [[] for _ in range(6)]
    outs_s = [[] for _ in range(4)]
    for l in range(depth):
        wts = _prep_weights(g_mix[l], w_in[l], b_gate[l], sink[l], w_br[l], w_o[l],
                            g_ffn[l], w_up[l], w_conv[l], b_conv[l], w_down[l])
        mk, mv = _memory_kv(mem_prompt, g_mem[l], w_mem_kv[l])
        zero_buf = jnp.zeros((bp, 2, 2 * D_FF), F32)
        assert depth == 1
        hp, kp, vp, sp, cp = _run_group(hp, 0, mk, mv, None, None, zero_buf, wts, g_final, log_g)
        hs, ksn, vsn, ssn, csn = _run_group(hs, PAST_LEN, cache_mem_k[l], cache_mem_v[l],
                                            (cache_swa_k[l], cache_swa_v[l]), state_ret[l],
                                            state_ffn_conv[l], wts, g_final, log_g)
        for lst, val in zip(outs_p, (kp, vp, sp, cp, mk, mv)):
            lst.append(val)
        for lst, val in zip(outs_s, (ksn, vsn, ssn, csn)):
            lst.append(val)
    return (hp, hs, *[jnp.stack(v) for v in outs_p], *[jnp.stack(v) for v in outs_s])
```

```python
import functools

import jax
import jax.numpy as jnp
from jax import lax
from jax.experimental import pallas as pl
from jax.experimental.pallas import tpu as pltpu

F32 = jnp.float32
BF16 = jnp.bfloat16

D_MODEL = 2048
CHUNK = 64
WINDOW = 128
SWA_HEADS = 16
SWA_KV_HEADS = 4
SWA_GROUP = SWA_HEADS // SWA_KV_HEADS
SWA_HEAD_DIM = 64
RET_HEADS = 8
RET_DIM = 128
RET_ROPE_BASE = 10000.0
N_MEM = 256
MEM_HEADS = 4
MEM_HEAD_DIM = 256
D_FF = 5632
N_BRANCH = 3
EPS = 1e-6
NEG = -1e30
PAST_LEN = 1024

SWA_Q = SWA_HEADS * SWA_HEAD_DIM
SWA_KV = SWA_KV_HEADS * SWA_HEAD_DIM
RET_W = RET_HEADS * RET_DIM
MEM_W = MEM_HEADS * MEM_HEAD_DIM
LANE = 128
CONV_ROWS = 64

def _sigmoid(x):
    return 1.0 / (1.0 + jnp.exp(-x))


def _rms(x, g):
    return x * lax.rsqrt(jnp.mean(x * x, axis=-1, keepdims=True) + EPS) * g


def _rmsnorm_kernel(x_ref, g_ref, o_ref):
    o_ref[...] = _rms(x_ref[...], g_ref[...]).astype(o_ref.dtype)


def _rmsnorm(x, g, tm):
    n, d = x.shape
    return pl.pallas_call(
        _rmsnorm_kernel,
        grid=(n // tm,),
        in_specs=[pl.BlockSpec((tm, d), lambda i: (i, 0)),
                  pl.BlockSpec((1, d), lambda i: (0, 0))],
        out_specs=pl.BlockSpec((tm, d), lambda i: (i, 0)),
        out_shape=jax.ShapeDtypeStruct((n, d), BF16),
        name="rmsnorm",
    )(x, g.reshape(1, d))


def _mm_kernel(x_ref, w_ref, *rest, epilogue, tn):
    o_ref = rest[-1]
    acc = jnp.dot(x_ref[...], w_ref[...], preferred_element_type=F32)
    if epilogue == "plain":
        out = acc
    elif epilogue == "scale":
        out = acc * rest[0][...]
    elif epilogue == "silu":
        out = acc * _sigmoid(acc)
    elif epilogue == "sigmoid_bias":
        out = _sigmoid(acc + rest[0][...])
    elif epilogue == "residual":
        out = rest[0][...] + acc
    elif epilogue == "rotary":
        cc, ss = rest[1][...], rest[2][...]
        pieces = []
        for j in range(tn // LANE):
            xh = acc[:, j * LANE:(j + 1) * LANE]
            pieces.append(xh * cc + pltpu.roll(xh, LANE // 2, 1) * ss)
        out = jnp.concatenate(pieces, axis=1) * rest[0][...]
    else:
        raise ValueError(epilogue)
    o_ref[...] = out.astype(o_ref.dtype)


def _mm(x, w, *, epilogue, out_dtype, tm, tn, colvec=None, residual=None, tables=None, name):
    n, k = x.shape
    ncols = w.shape[1]
    in_specs = [pl.BlockSpec((tm, k), lambda i, j: (i, 0)),
                pl.BlockSpec((k, tn), lambda i, j: (0, j))]
    args = [x, w]
    if colvec is not None:
        in_specs.append(pl.BlockSpec((1, tn), lambda i, j: (0, j)))
        args.append(colvec.reshape(1, ncols).astype(F32))
    if residual is not None:
        in_specs.append(pl.BlockSpec((tm, tn), lambda i, j: (i, j)))
        args.append(residual)
    if tables is not None:
        nblk = tables[0].shape[0] // tm
        for t in tables:
            in_specs.append(pl.BlockSpec((tm, LANE), lambda i, j: (i % nblk, 0)))
            args.append(t)
    return pl.pallas_call(
        functools.partial(_mm_kernel, epilogue=epilogue, tn=tn),
        grid=(n // tm, ncols // tn),
        in_specs=in_specs,
        out_specs=pl.BlockSpec((tm, tn), lambda i, j: (i, j)),
        out_shape=jax.ShapeDtypeStruct((n, ncols), out_dtype),
        compiler_params=pltpu.CompilerParams(dimension_semantics=("parallel", "parallel")),
        name=name,
    )(*args)


def _norm_mm_kernel(x_ref, g_ref, w_ref, cv_ref, o_ref, u_ref):
    @pl.when(pl.program_id(1) == 0)
    def _():
        u_ref[...] = _rms(x_ref[...], g_ref[...]).astype(u_ref.dtype)

    acc = jnp.dot(u_ref[...], w_ref[...], preferred_element_type=F32)
    o_ref[...] = (acc * cv_ref[...]).astype(o_ref.dtype)


def _norm_mm(x, g, w, colvec, *, tm, tn, name):
    n, k = x.shape
    ncols = w.shape[1]
    return pl.pallas_call(
        _norm_mm_kernel,
        grid=(n // tm, ncols // tn),
        in_specs=[pl.BlockSpec((tm, k), lambda i, j: (i, 0)),
                  pl.BlockSpec((1, k), lambda i, j: (0, 0)),
                  pl.BlockSpec((k, tn), lambda i, j: (0, j)),
                  pl.BlockSpec((1, tn), lambda i, j: (0, j))],
        out_specs=[pl.BlockSpec((tm, tn), lambda i, j: (i, j)),
                   pl.BlockSpec((tm, k), lambda i, j: (i, 0))],
        out_shape=[jax.ShapeDtypeStruct((n, ncols), BF16), jax.ShapeDtypeStruct((n, k), BF16)],
        compiler_params=pltpu.CompilerParams(dimension_semantics=("parallel", "arbitrary")),
        name=name,
    )(x, g.reshape(1, k), w, colvec.reshape(1, ncols).astype(F32))


def _swa_kernel(sink_ref, q_ref, kp_ref, kc_ref, vp_ref, vc_ref, o_ref, *scratch, qt, kv_off):
    s_scs, p_scs = scratch[:SWA_KV_HEADS], scratch[SWA_KV_HEADS:]
    i = pl.program_id(1)
    nq = qt // CHUNK
    w = (nq + 2) * CHUNK
    wp = -(-w // LANE) * LANE
    first_ok = jnp.where(i * qt + kv_off >= WINDOW, 0, 2)

    def window(p_ref, c_ref):
        parts = [p_ref[0], c_ref[0]]
        if wp > w:
            parts.append(jnp.zeros((wp - w, SWA_KV), F32))
        return jnp.concatenate(parts, axis=0)

    kwin = window(kp_ref, kc_ref)
    vwin = window(vp_ref, vc_ref)
    lane = lax.broadcasted_iota(jnp.int32, (wp, LANE), 1)
    lane_g = lax.broadcasted_iota(jnp.int32, (wp, SWA_KV), 1) // SWA_HEAD_DIM
    out_g = lax.broadcasted_iota(jnp.int32, (CHUNK, SWA_KV), 1) // SWA_HEAD_DIM

    def block_diag(win, h):
        col, half = divmod(h, 2)
        x = win[:, col * LANE:(col + 1) * LANE]
        xr = pltpu.roll(x, SWA_HEAD_DIM, 1)
        lo = lane < SWA_HEAD_DIM
        r = jnp.where(lo, x, xr) if half == 0 else jnp.where(lo, xr, x)
        r2 = jnp.concatenate([r, r], axis=1).astype(BF16)
        zero = jnp.zeros_like(r2)
        return jnp.concatenate([jnp.where(lane_g == g, r2, zero) for g in range(SWA_GROUP)], axis=0)

    for h in range(SWA_KV_HEADS):
        bk = block_diag(kwin, h)
        bv = block_diag(vwin, h)
        qh = q_ref[0, :, h * SWA_KV:(h + 1) * SWA_KV]
        s_sc, p_sc = s_scs[h], p_scs[h]
        s_sc[...] = lax.dot_general(qh, bk, (((1,), (1,)), ((), ())), preferred_element_type=F32)
        invs = []
        for jq in range(nq):
            rows = slice(jq * CHUNK, (jq + 1) * CHUNK)
            c_lo, c_hi = jq * CHUNK // LANE, ((jq + 3) * CHUNK - 1) // LANE
            width = (c_hi - c_lo + 1) * LANE
            kc = c_lo * (LANE // CHUNK) + lax.broadcasted_iota(jnp.int32, (CHUNK, width), 1) // CHUNK
            visible = (kc >= jq) & (kc <= jq + 2) & (kc >= first_ok)
            inv = jnp.zeros((CHUNK, SWA_KV), F32)
            for g in range(SWA_GROUP):
                l0 = g * wp + c_lo * LANE
                sg = jnp.where(visible, s_sc[rows, l0:l0 + width], NEG)
                sk = sink_ref[h * SWA_GROUP + g]
                m = jnp.maximum(jnp.max(sg, axis=1, keepdims=True), sk)
                p = jnp.exp(sg - m)
                den = jnp.sum(p, axis=1, keepdims=True) + jnp.exp(sk - m)
                p_sc[rows, l0:l0 + width] = p.astype(BF16)
                for c in range(wp // LANE):
                    if not c_lo <= c <= c_hi:
                        p_sc[rows, g * wp + c * LANE:g * wp + (c + 1) * LANE] = jnp.zeros((CHUNK, LANE), BF16)
                inv = jnp.where(out_g == g, 1.0 / den, inv)
            invs.append(inv)
        o = jnp.dot(p_sc[...], bv, preferred_element_type=F32)
        o_ref[0, :, h * SWA_KV:(h + 1) * SWA_KV] = (o * jnp.concatenate(invs, axis=0)).astype(o_ref.dtype)


def _swa(q, q_col, k, k_col, v, v_col, sink, *, sq, qt, kv_off):
    b = q.shape[0]
    cur_off = kv_off // qt
    prev_off = kv_off // WINDOW - 1
    per = qt // WINDOW if qt >= WINDOW else 0
    wp = -(-(qt + WINDOW) // LANE) * LANE

    def prev_map(kcol):
        return lambda bi, i, s: (bi, jnp.maximum(i * per + prev_off, 0), kcol)

    def cur_map(kcol):
        return lambda bi, i, s: (bi, i + cur_off, kcol)

    grid_spec = pltpu.PrefetchScalarGridSpec(
        num_scalar_prefetch=1,
        grid=(b, sq // qt),
        in_specs=[pl.BlockSpec((1, qt, SWA_Q), lambda bi, i, s: (bi, i, q_col)),
                  pl.BlockSpec((1, WINDOW, SWA_KV), prev_map(k_col)),
                  pl.BlockSpec((1, qt, SWA_KV), cur_map(k_col)),
                  pl.BlockSpec((1, WINDOW, SWA_KV), prev_map(v_col)),
                  pl.BlockSpec((1, qt, SWA_KV), cur_map(v_col))],
        out_specs=pl.BlockSpec((1, qt, SWA_Q), lambda bi, i, s: (bi, i, 0)),
        scratch_shapes=([pltpu.VMEM((qt, SWA_GROUP * wp), F32)] * SWA_KV_HEADS
                        + [pltpu.VMEM((qt, SWA_GROUP * wp), BF16)] * SWA_KV_HEADS),
    )
    return pl.pallas_call(
        functools.partial(_swa_kernel, qt=qt, kv_off=kv_off),
        grid_spec=grid_spec,
        out_shape=jax.ShapeDtypeStruct((b, sq, SWA_Q), BF16),
        compiler_params=pltpu.CompilerParams(dimension_semantics=("parallel", "parallel")),
        name="swa",
    )(sink.astype(F32), q, k, k, v, v)


def _ret_kernel(lg_ref, q_ref, k_ref, v_ref, g_ref, s0_ref, o_ref, st_ref, dec_ref, xi_ref, zeta_ref, *, c):
    bi = pl.program_id(0)
    ci = pl.program_id(1)

    @pl.when((bi == 0) & (ci == 0))
    def _():
        n = lax.broadcasted_iota(jnp.int32, (c, c), 0)
        m = lax.broadcasted_iota(jnp.int32, (c, c), 1)
        diff = (n - m).astype(F32)
        nrow = lax.broadcasted_iota(jnp.int32, (c, RET_DIM), 0).astype(F32)
        for h in range(RET_HEADS):
            dec_ref[h] = jnp.where(diff >= 0, jnp.exp(lg_ref[h] * jnp.maximum(diff, 0.0)), 0.0)
            xi_ref[h] = jnp.exp(lg_ref[h] * (nrow + 1.0))
            zeta_ref[h] = jnp.exp(lg_ref[h] * (float(c) - 1.0 - nrow))

    @pl.when(ci == 0)
    def _():
        st_ref[...] = s0_ref[...]

    for h in range(RET_HEADS):
        lg = lg_ref[h]
        sl = slice(h * RET_DIM, (h + 1) * RET_DIM)
        qh = q_ref[0, :, sl]
        kh = k_ref[0, :, sl]
        vh = v_ref[0, :, sl]
        state = st_ref[0, h]
        inner = lax.dot_general(qh, kh, (((1,), (1,)), ((), ())), preferred_element_type=F32) * dec_ref[h]
        o = jnp.dot(inner.astype(BF16), vh, preferred_element_type=F32)
        cross = jnp.dot(qh, state.astype(BF16), preferred_element_type=F32)
        o = o + cross * xi_ref[h]
        kz = (kh.astype(F32) * zeta_ref[h]).astype(BF16)
        upd = lax.dot_general(kz, vh, (((0,), (0,)), ((), ())), preferred_element_type=F32)
        decay_c = jnp.exp(lg * jnp.full((1, RET_DIM), float(c), F32))
        st_ref[0, h] = decay_c * state + upd
        on = o * lax.rsqrt(jnp.mean(o * o, axis=-1, keepdims=True) + EPS)
        o_ref[0, :, sl] = (g_ref[0, :, sl].astype(F32) * on).astype(o_ref.dtype)


def _retention(qk, q_col, k_col, v, v_col, gate, state0, log_g, *, s, c):
    b = qk.shape[0]

    def seq(col):
        return pl.BlockSpec((1, c, RET_W), lambda bi, ci, lg: (bi, ci, col))

    st_spec = pl.BlockSpec((1, RET_HEADS, RET_DIM, RET_DIM), lambda bi, ci, lg: (bi, 0, 0, 0))
    grid_spec = pltpu.PrefetchScalarGridSpec(
        num_scalar_prefetch=1,
        grid=(b, s // c),
        in_specs=[seq(q_col), seq(k_col), seq(v_col), seq(0), st_spec],
        out_specs=[seq(0), st_spec],
        scratch_shapes=[pltpu.VMEM((RET_HEADS, c, c), F32), pltpu.VMEM((RET_HEADS, c, RET_DIM), F32),
                        pltpu.VMEM((RET_HEADS, c, RET_DIM), F32)],
    )
    return pl.pallas_call(
        functools.partial(_ret_kernel, c=c),
        grid_spec=grid_spec,
        out_shape=[jax.ShapeDtypeStruct((b, s, RET_W), BF16),
                   jax.ShapeDtypeStruct((b, RET_HEADS, RET_DIM, RET_DIM), F32)],
        compiler_params=pltpu.CompilerParams(dimension_semantics=("arbitrary", "arbitrary")),
        name="retention",
    )(log_g, qk, qk, v, gate, state0)


def _mem_kernel(q_ref, k_ref, v_ref, o_ref):
    for h in range(MEM_HEADS):
        sl = slice(h * MEM_HEAD_DIM, (h + 1) * MEM_HEAD_DIM)
        qh = q_ref[0, :, sl]
        kh = k_ref[0, :, sl].astype(BF16)
        vh = v_ref[0, :, sl].astype(BF16)
        s = lax.dot_general(qh, kh, (((1,), (1,)), ((), ())), preferred_element_type=F32)
        m = jnp.max(s, axis=1, keepdims=True)
        p = jnp.exp(s - m)
        den = jnp.sum(p, axis=1, keepdims=True)
        o = jnp.dot(p.astype(BF16), vh, preferred_element_type=F32)
        o_ref[0, :, sl] = (o * (1.0 / den)).astype(o_ref.dtype)


def _mem_attend(q, q_col, mk, mv, *, s, tq):
    b = q.shape[0]
    kv_spec = pl.BlockSpec((1, N_MEM, MEM_W), lambda bi, i: (bi, 0, 0))
    return pl.pallas_call(
        _mem_kernel,
        grid=(b, s // tq),
        in_specs=[pl.BlockSpec((1, tq, MEM_W), lambda bi, i: (bi, i, q_col)), kv_spec, kv_spec],
        out_specs=pl.BlockSpec((1, tq, MEM_W), lambda bi, i: (bi, i, 0)),
        out_shape=jax.ShapeDtypeStruct((b, s, MEM_W), BF16),
        compiler_params=pltpu.CompilerParams(dimension_semantics=("parallel", "parallel")),
        name="mem_attend",
    )(q, mk, mv)


def _merge_kernel(u_ref, oa_ref, ob_ref, oc_ref, wga_ref, wgb_ref, wgc_ref, ba_ref, bb_ref, bc_ref,
                  wa_ref, wb_ref, wc_ref, o_ref):
    u = u_ref[...]
    acc = None
    for o_r, wg_ref, b_ref, w_ref in ((oa_ref, wga_ref, ba_ref, wa_ref), (ob_ref, wgb_ref, bb_ref, wb_ref),
                                      (oc_ref, wgc_ref, bc_ref, wc_ref)):
        gate = _sigmoid(jnp.dot(u, wg_ref[...], preferred_element_type=F32) + b_ref[...])
        term = gate * jnp.dot(o_r[...], w_ref[...], preferred_element_type=F32)
        acc = term if acc is None else acc + term
    o_ref[...] = acc.astype(o_ref.dtype)


def _merge(u, o_swa, o_ret, o_mem, w_gate, b_gate, w_br, *, tm, tn):
    n, d = u.shape
    kb = o_swa.shape[1]
    nj = D_MODEL // tn
    o_spec = pl.BlockSpec((tm, kb), lambda i, j: (i, 0))

    def wg_spec(br):
        return pl.BlockSpec((d, tn), lambda i, j: (0, br * nj + j))

    def b_spec(br):
        return pl.BlockSpec((1, tn), lambda i, j: (0, br * nj + j))

    def w_spec(br):
        return pl.BlockSpec((kb, tn), lambda i, j: (br, j))

    bias = b_gate.reshape(1, -1).astype(F32)
    return pl.pallas_call(
        _merge_kernel,
        grid=(n // tm, nj),
        in_specs=[pl.BlockSpec((tm, d), lambda i, j: (i, 0)), o_spec, o_spec, o_spec,
                  wg_spec(0), wg_spec(1), wg_spec(2), b_spec(0), b_spec(1), b_spec(2),
                  w_spec(0), w_spec(1), w_spec(2)],
        out_specs=pl.BlockSpec((tm, tn), lambda i, j: (i, j)),
        out_shape=jax.ShapeDtypeStruct((n, D_MODEL), BF16),
        compiler_params=pltpu.CompilerParams(dimension_semantics=("parallel", "parallel")),
        name="merge",
    )(u, o_swa, o_ret, o_mem, w_gate, w_gate, w_gate, bias, bias, bias, w_br, w_br, w_br)


def _ffn_kernel(h_ref, hj_ref, gf_ref, wug_ref, wuv_ref, wc_ref, b_ref, wd_ref, cb_ref, gfin_ref, y_ref, co_ref,
                u_sc, act_sc, ag_sc, av_sc, y_sc, cg_sc, cv_sc, *, tm, tf, td, nf, seg, nseg):
    i = pl.program_id(1)
    t = pl.program_id(2)
    nd = D_MODEL // td

    seg_t = tm // nseg
    stride = seg_t + 8

    def up_proj():
        u = u_sc[...]
        for a_sc, wu_ref in ((ag_sc, wug_ref), (av_sc, wuv_ref)):
            a = jnp.dot(u, wu_ref[...], preferred_element_type=F32)
            for s in range(nseg):
                a_sc[s * stride + 8:(s + 1) * stride, :] = a[s * seg_t:(s + 1) * seg_t]

    def conv_act():
        fb = t - 1
        for a_sc, blk, carry_sc in ((ag_sc, fb, cg_sc), (av_sc, nf + fb, cv_sc)):
            if nseg == 1:
                @pl.when(i == 0)
                def _():
                    a_sc[6:8, :] = cb_ref[0, blk]

                @pl.when(i > 0)
                def _():
                    a_sc[6:8, :] = carry_sc[fb, 6:8, :]

                carry_sc[fb] = a_sc[tm:tm + 8, :]
            else:
                for s in range(nseg):
                    a_sc[s * stride + 6:s * stride + 8, :] = cb_ref[s, blk]
            for s in range(nseg):
                co_ref[0, s, blk] = a_sc[(s + 1) * stride - 8:(s + 1) * stride, :]

        def conv(a_sc, blk, r0, c0):
            tap = lambda k: a_sc[r0 - k:r0 - k + CONV_ROWS, c0:c0 + LANE]
            wc = lambda k: wc_ref[blk, k:k + 1, c0:c0 + LANE]
            return b_ref[blk, :, c0:c0 + LANE] + (wc(0) * tap(2) + wc(1) * tap(1) + wc(2) * tap(0))

        for c0 in range(0, tf, LANE):
            for s in range(nseg):
                for r in range(0, seg_t, CONV_ROWS):
                    cg = conv(ag_sc, fb, s * stride + 8 + r, c0)
                    cv = conv(av_sc, nf + fb, s * stride + 8 + r, c0)
                    act_sc[fb, s * seg_t + r:s * seg_t + r + CONV_ROWS, c0:c0 + LANE] = (
                        cg * _sigmoid(cg) * cv).astype(BF16)

    @pl.when(t == 0)
    def _():
        u_sc[...] = _rms(h_ref[0], gf_ref[...]).astype(BF16)
        up_proj()

    @pl.when((t >= 1) & (t < nf))
    def _():
        conv_act()
        up_proj()

    def down(j):
        act = jnp.concatenate([act_sc[f] for f in range(nf)], axis=1)
        y_sc[j] = hj_ref[0] + jnp.dot(act, wd_ref[...], preferred_element_type=F32)

    @pl.when(t == nf)
    def _():
        conv_act()
        down(0)

    @pl.when(t > nf)
    def _():
        down(t - nf)

    @pl.when(t == nf + nd - 1)
    def _():
        full = jnp.concatenate([y_sc[k] for k in range(nd)], axis=1)
        y_ref[0] = _rms(full, gfin_ref[...])


def _ffn(h, conv_buf, g_ffn, w_up, w_conv, b_conv, w_down, g_final, *, tm, tf, td, seg, nseg):
    bt, rows, d = h.shape
    nf = D_FF // tf
    nd = d // td
    nt = rows // tm
    nb = bt * nseg
    up_blk = lambda t: jnp.minimum(t, nf - 1)
    down_blk = lambda t: jnp.clip(t - nf, 0, nd - 1)
    wc_blocks = w_conv.reshape(3, 2 * nf, tf).transpose(1, 0, 2)
    b_blocks = b_conv.reshape(2 * nf, 1, tf)
    cb_blocks = conv_buf.reshape(nb, 2, 2 * nf, tf).transpose(0, 2, 1, 3)
    whole = lambda *shape: pl.BlockSpec(shape, lambda b, i, t: (0,) * len(shape))
    in_specs = [
        pl.BlockSpec((1, tm, d), lambda b, i, t: (b, i, 0)),
        pl.BlockSpec((1, tm, td), lambda b, i, t: (b, i, down_blk(t))),
        whole(1, d),
        pl.BlockSpec((d, tf), lambda b, i, t: (0, up_blk(t))),
        pl.BlockSpec((d, tf), lambda b, i, t: (0, nf + up_blk(t))),
        whole(2 * nf, 3, tf),
        whole(2 * nf, 1, tf),
        pl.BlockSpec((D_FF, td), lambda b, i, t: (0, down_blk(t))),
        pl.BlockSpec((nseg, 2 * nf, 2, tf), lambda b, i, t: (b, 0, 0, 0)),
        whole(1, d),
    ]
    out_specs = [
        pl.BlockSpec((1, tm, d), lambda b, i, t: (b, i, 0)),
        pl.BlockSpec((1, nseg, 2 * nf, 8, tf), lambda b, i, t: (b * nt + i, 0, 0, 0, 0)),
    ]
    return pl.pallas_call(
        functools.partial(_ffn_kernel, tm=tm, tf=tf, td=td, nf=nf, seg=seg, nseg=nseg),
        grid=(bt, nt, nf + nd),
        in_specs=in_specs,
        out_specs=out_specs,
        out_shape=[jax.ShapeDtypeStruct((bt, rows, d), F32),
                   jax.ShapeDtypeStruct((bt * nt, nseg, 2 * nf, 8, tf), F32)],
        scratch_shapes=[pltpu.VMEM((tm, d), BF16), pltpu.VMEM((nf, tm, tf), BF16),
                        pltpu.VMEM((tm + 8 * nseg, tf), F32), pltpu.VMEM((tm + 8 * nseg, tf), F32),
                        pltpu.VMEM((nd, tm, td), F32),
                        pltpu.VMEM((nf, 8, tf), F32), pltpu.VMEM((nf, 8, tf), F32)],
        compiler_params=pltpu.CompilerParams(dimension_semantics=("arbitrary", "arbitrary", "arbitrary")),
        name="conv_ffn",
    )(h, h, g_ffn.reshape(1, d), w_up, w_up, wc_blocks, b_blocks, w_down, cb_blocks, g_final.reshape(1, d))


def _ffn_up_kernel(u_ref, wug_ref, wuv_ref, wc_ref, b_ref, cb_ref, act_ref, co_ref,
                   ag_sc, av_sc, cg_sc, cv_sc, *, tm, tf, nf, nseg):
    i = pl.program_id(1)
    fb = pl.program_id(2)
    seg_t = tm // nseg
    stride = seg_t + 8
    if nseg == 1:
        @pl.when((i == 0) & (fb == 0))
        def _():
            for f in range(nf):
                cg_sc[f, 6:8, :] = cb_ref[0, f]
                cv_sc[f, 6:8, :] = cb_ref[0, nf + f]

    u = u_ref[0]
    for a_sc, wu_ref, blk, carry_sc in ((ag_sc, wug_ref, fb, cg_sc), (av_sc, wuv_ref, nf + fb, cv_sc)):
        a = jnp.dot(u, wu_ref[...], preferred_element_type=F32)
        for s in range(nseg):
            a_sc[s * stride + 8:(s + 1) * stride, :] = a[s * seg_t:(s + 1) * seg_t]
            co_ref[0, s, blk] = a[(s + 1) * seg_t - 8:(s + 1) * seg_t]
        if nseg == 1:
            a_sc[6:8, :] = carry_sc[fb, 6:8, :]
            carry_sc[fb] = a[tm - 8:tm]
        else:
            for s in range(nseg):
                a_sc[s * stride + 6:s * stride + 8, :] = cb_ref[s, blk]

    def conv(a_sc, blk, r0, c0):
        tap = lambda k: a_sc[r0 - k:r0 - k + CONV_ROWS, c0:c0 + LANE]
        wc = lambda k: wc_ref[blk, k:k + 1, c0:c0 + LANE]
        return b_ref[blk, :, c0:c0 + LANE] + (wc(0) * tap(2) + wc(1) * tap(1) + wc(2) * tap(0))

    for c0 in range(0, tf, LANE):
        for s in range(nseg):
            for r in range(0, seg_t, CONV_ROWS):
                cg = conv(ag_sc, fb, s * stride + 8 + r, c0)
                cv = conv(av_sc, nf + fb, s * stride + 8 + r, c0)
                act_ref[0, s * seg_t + r:s * seg_t + r + CONV_ROWS, c0:c0 + LANE] = (
                    cg * _sigmoid(cg) * cv).astype(act_ref.dtype)


def _ffn_up(u, conv_buf, w_up, w_conv, b_conv, *, tm, tf, nseg):
    bt, rows, d = u.shape
    nf = D_FF // tf
    nt = rows // tm
    nb = bt * nseg
    wc_blocks = w_conv.reshape(3, 2 * nf, tf).transpose(1, 0, 2)
    b_blocks = b_conv.reshape(2 * nf, 1, tf)
    cb_blocks = conv_buf.reshape(nb, 2, 2 * nf, tf).transpose(0, 2, 1, 3)
    whole = lambda *shape: pl.BlockSpec(shape, lambda b, i, f: (0,) * len(shape))
    return pl.pallas_call(
        functools.partial(_ffn_up_kernel, tm=tm, tf=tf, nf=nf, nseg=nseg),
        grid=(bt, nt, nf),
        in_specs=[pl.BlockSpec((1, tm, d), lambda b, i, f: (b, i, 0)),
                  pl.BlockSpec((d, tf), lambda b, i, f: (0, f)),
                  pl.BlockSpec((d, tf), lambda b, i, f: (0, nf + f)),
                  whole(2 * nf, 3, tf),
                  whole(2 * nf, 1, tf),
                  pl.BlockSpec((nseg, 2 * nf, 2, tf), lambda b, i, f: (b, 0, 0, 0))],
        out_specs=[pl.BlockSpec((1, tm, tf), lambda b, i, f: (b, i, f)),
                   pl.BlockSpec((1, nseg, 2 * nf, 8, tf), lambda b, i, f: (b * nt + i, 0, 0, 0, 0))],
        out_shape=[jax.ShapeDtypeStruct((bt, rows, D_FF), BF16),
                   jax.ShapeDtypeStruct((bt * nt, nseg, 2 * nf, 8, tf), F32)],
        scratch_shapes=[pltpu.VMEM((tm + 8 * nseg, tf), F32), pltpu.VMEM((tm + 8 * nseg, tf), F32),
                        pltpu.VMEM((nf, 8, tf), F32), pltpu.VMEM((nf, 8, tf), F32)],
        compiler_params=pltpu.CompilerParams(dimension_semantics=("arbitrary", "arbitrary", "arbitrary")),
        name="ffn_up",
    )(u, w_up, w_up, wc_blocks, b_blocks, cb_blocks)


def _ffn_down_kernel(act_ref, wd_ref, h_ref, g_ref, y_ref, y_sc, *, nd):
    j = pl.program_id(1)
    y_sc[j] = h_ref[...] + jnp.dot(act_ref[...], wd_ref[...], preferred_element_type=F32)

    @pl.when(j == nd - 1)
    def _():
        full = jnp.concatenate([y_sc[k] for k in range(nd)], axis=1)
        y_ref[...] = _rms(full, g_ref[...])


def _ffn_down(act, w_down, h, g_final, *, tm, td):
    n, f = act.shape
    d = h.shape[1]
    nd = d // td
    return pl.pallas_call(
        functools.partial(_ffn_down_kernel, nd=nd),
        grid=(n // tm, nd),
        in_specs=[pl.BlockSpec((tm, f), lambda i, j: (i, 0)),
                  pl.BlockSpec((f, td), lambda i, j: (0, j)),
                  pl.BlockSpec((tm, td), lambda i, j: (i, j)),
                  pl.BlockSpec((1, d), lambda i, j: (0, 0))],
        out_specs=pl.BlockSpec((tm, d), lambda i, j: (i, 0)),
        out_shape=jax.ShapeDtypeStruct((n, d), F32),
        scratch_shapes=[pltpu.VMEM((nd, tm, td), F32)],
        compiler_params=pltpu.CompilerParams(dimension_semantics=("parallel", "arbitrary")),
        name="ffn_down",
    )(act, w_down, h, g_final.reshape(1, d))


def _out_proj_kernel(x_ref, w_ref, res_ref, g_ref, h_ref, u_ref):
    h = res_ref[...] + jnp.dot(x_ref[...], w_ref[...], preferred_element_type=F32)
    h_ref[...] = h
    u_ref[...] = _rms(h, g_ref[...]).astype(u_ref.dtype)


def _out_proj(x, w, residual, g, *, tm):
    n, k = x.shape
    d = w.shape[1]
    row = lambda width: pl.BlockSpec((tm, width), lambda i: (i, 0))
    return pl.pallas_call(
        _out_proj_kernel,
        grid=(n // tm,),
        in_specs=[row(k), pl.BlockSpec((k, d), lambda i: (0, 0)), row(d), pl.BlockSpec((1, d), lambda i: (0, 0))],
        out_specs=[row(d), row(d)],
        out_shape=[jax.ShapeDtypeStruct((n, d), F32), jax.ShapeDtypeStruct((n, d), BF16)],
        compiler_params=pltpu.CompilerParams(dimension_semantics=("parallel",)),
        name="proj_out",
    )(x, w, residual, g.reshape(1, d))


def _rope_tables(pos0, s):
    half = RET_DIM // 2
    inv_freq = 1.0 / (RET_ROPE_BASE ** jnp.linspace(0.0, 1.0, half, dtype=F32))
    ang = (pos0 + jnp.arange(s)).astype(F32)[:, None] * inv_freq[None, :]
    cos, sin = jnp.cos(ang), jnp.sin(ang)
    return jnp.concatenate([cos, cos], axis=1), jnp.concatenate([-sin, sin], axis=1)


def _prep_weights(g_mix, w_in, b_gate, sink, w_br, w_o, g_ffn, w_up, w_conv, b_conv, w_down):
    o = 0
    cuts = {}
    for name, width in (("qa", SWA_Q), ("ka", SWA_KV), ("va", SWA_KV), ("qr", RET_W), ("kr", RET_W),
                        ("vr", RET_W), ("gr", RET_W), ("qm", MEM_W), ("gl", N_BRANCH * D_MODEL)):
        cuts[name] = (o, o + width)
        o += width
    cols = lambda *names: jnp.concatenate([w_in[:, cuts[n][0]:cuts[n][1]] for n in names], axis=1).astype(BF16)
    ones = lambda n, v: jnp.full((n,), v, F32)
    return dict(
        g_mix=g_mix, b_gate=b_gate, sink=sink, g_ffn=g_ffn, w_conv=w_conv, b_conv=b_conv,
        w_plain=cols("qa", "vr", "qm"),
        s_plain=jnp.concatenate([ones(SWA_Q, SWA_HEAD_DIM ** -0.5), ones(RET_W, 1.0),
                                 ones(MEM_W, MEM_HEAD_DIM ** -0.5)]),
        w_kv=cols("ka", "va"),
        w_rot=cols("qr", "kr"),
        s_rot=jnp.concatenate([ones(RET_W, 1.0), ones(RET_W, RET_DIM ** -0.5)]),
        w_g=cols("gr"),
        w_gate=cols("gl"),
        w_br=w_br.astype(BF16), w_o=w_o.astype(BF16), w_up=w_up.astype(BF16), w_down=w_down.astype(BF16),
    )


def _run_group(x, pos0, mem_k, mem_v, swa_cache, ret_state, conv_buf, wts, g_final, log_g):
    b, s, d = x.shape
    n = b * s
    tm = min(1024, n)
    x2 = x.reshape(n, d)
    plain, u = _norm_mm(x2, wts["g_mix"], wts["w_plain"], wts["s_plain"], tm=tm, tn=1024, name="proj_plain")
    kv = _mm(u, wts["w_kv"], epilogue="plain", out_dtype=F32, tm=tm, tn=2 * SWA_KV, name="proj_kv")
    cc, ss = _rope_tables(pos0, s)
    if s < tm:
        cc, ss = jnp.tile(cc, (tm // s, 1)), jnp.tile(ss, (tm // s, 1))
    rot = _mm(u, wts["w_rot"], epilogue="rotary", out_dtype=BF16, tm=tm, tn=1024,
              colvec=wts["s_rot"], tables=(cc, ss), name="proj_rot")
    sgate = _mm(u, wts["w_g"], epilogue="silu", out_dtype=BF16, tm=tm, tn=1024, name="proj_silu")

    plain3 = plain.reshape(b, s, -1)
    kv3 = kv.reshape(b, s, 2 * SWA_KV)
    if swa_cache is None:
        o_swa = _swa(plain3, 0, kv3, 0, kv3, 1, wts["sink"], sq=s, qt=min(256, s), kv_off=0)
        new_k = kv3[:, s - WINDOW:, :SWA_KV]
        new_v = kv3[:, s - WINDOW:, SWA_KV:]
    else:
        ck = swa_cache[0].reshape(b, -1, SWA_KV)
        cv = swa_cache[1].reshape(b, -1, SWA_KV)
        n_keep = ck.shape[1]
        k_all = jnp.concatenate([ck, kv3[:, :, :SWA_KV]], axis=1)
        v_all = jnp.concatenate([cv, kv3[:, :, SWA_KV:]], axis=1)
        o_swa = _swa(plain3, 0, k_all, 0, v_all, 0, wts["sink"], sq=s, qt=s, kv_off=n_keep)
        new_k = k_all[:, -n_keep:]
        new_v = v_all[:, -n_keep:]
    state0 = jnp.zeros((b, RET_HEADS, RET_DIM, RET_DIM), F32) if ret_state is None else ret_state
    o_ret, s_new = _retention(rot.reshape(b, s, -1), 0, 1, plain3, 1, sgate.reshape(b, s, -1), state0, log_g,
                              s=s, c=min(256, s))
    o_mem = _mem_attend(plain3, 2, mem_k.reshape(b, N_MEM, MEM_W), mem_v.reshape(b, N_MEM, MEM_W),
                        s=s, tq=min(512, s))
    merged = _merge(u, o_swa.reshape(n, -1), o_ret.reshape(n, -1), o_mem.reshape(n, -1), wts["w_gate"],
                    wts["b_gate"], wts["w_br"], tm=tm, tn=512)
    h1, u2 = _out_proj(merged, wts["w_o"], x2, wts["g_ffn"], tm=min(512, n))
    if s >= 1024:
        act, co = _ffn_up(u2.reshape(b, s, d), conv_buf, wts["w_up"], wts["w_conv"], wts["b_conv"],
                          tm=1024, tf=512, nseg=1)
    else:
        act, co = _ffn_up(u2.reshape(1, n, d), conv_buf, wts["w_up"], wts["w_conv"], wts["b_conv"],
                          tm=n, tf=512, nseg=b)
    y = _ffn_down(act.reshape(n, D_FF), wts["w_down"], h1, g_final, tm=min(1024, n), td=256)
    nblk, tfb = co.shape[2], co.shape[4]
    last = co.reshape(b, -1, nblk, 8, tfb)[:, -1, :, 6:8, :]
    new_buf = last.transpose(0, 2, 1, 3).reshape(b, 2, nblk * tfb)
    return (y.reshape(b, s, d), new_k.reshape(b, -1, SWA_KV_HEADS, SWA_HEAD_DIM),
            new_v.reshape(b, -1, SWA_KV_HEADS, SWA_HEAD_DIM), s_new, new_buf)


def _memory_kv(mem, g_mem, w_mem_kv):
    b, m, d = mem.shape
    u = _rmsnorm(mem.reshape(b * m, d), g_mem, min(512, b * m))
    kv = _mm(u, w_mem_kv.astype(BF16), epilogue="plain", out_dtype=F32, tm=min(1024, b * m), tn=1024, name="mem_kv")
    return (kv[:, :MEM_W].reshape(b, m, MEM_HEADS, MEM_HEAD_DIM), kv[:, MEM_W:].reshape(b, m, MEM_HEADS, MEM_HEAD_DIM))


def kernel(x_prompt, x_sample, mem_prompt, cache_swa_k, cache_swa_v, state_ret, state_ffn_conv, cache_mem_k, cache_mem_v, g_mix, w_in, b_gate, sink, w_br, w_o, g_mem, w_mem_kv, g_ffn, w_up, w_conv, b_conv, w_down, g_final):
    bp = x_prompt.shape[0]
    depth = w_in.shape[0]
    log_g = jnp.log1p(-jnp.exp2(-5.0 - jnp.arange(RET_HEADS, dtype=F32)))
    hp, hs = x_prompt, x_sample
    outs_p = [[] for _ in range(6)]
    outs_s = [[] for _ in range(4)]
    for l in range(depth):
        wts = _prep_weights(g_mix[l], w_in[l], b_gate[l], sink[l], w_br[l], w_o[l],
                            g_ffn[l], w_up[l], w_conv[l], b_conv[l], w_down[l])
        mk, mv = _memory_kv(mem_prompt, g_mem[l], w_mem_kv[l])
        zero_buf = jnp.zeros((bp, 2, 2 * D_FF), F32)
        assert depth == 1
        hp, kp, vp, sp, cp = _run_group(hp, 0, mk, mv, None, None, zero_buf, wts, g_final, log_g)
        hs, ksn, vsn, ssn, csn = _run_group(hs, PAST_LEN, cache_mem_k[l], cache_mem_v[l],
                                            (cache_swa_k[l], cache_swa_v[l]), state_ret[l],
                                            state_ffn_conv[l], wts, g_final, log_g)
        for lst, val in zip(outs_p, (kp, vp, sp, cp, mk, mv)):
            lst.append(val)
        for lst, val in zip(outs_s, (ksn, vsn, ssn, csn)):
            lst.append(val)
    return (hp, hs, *[jnp.stack(v) for v in outs_p], *[jnp.stack(v) for v in outs_s])
```

```python
import functools

import jax
import jax.numpy as jnp
from jax import lax
from jax.experimental import pallas as pl
from jax.experimental.pallas import tpu as pltpu

F32 = jnp.float32
BF16 = jnp.bfloat16

D_MODEL = 2048
CHUNK = 64
WINDOW = 128
SWA_HEADS = 16
SWA_KV_HEADS = 4
SWA_GROUP = SWA_HEADS // SWA_KV_HEADS
SWA_HEAD_DIM = 64
RET_HEADS = 8
RET_DIM = 128
RET_ROPE_BASE = 10000.0
N_MEM = 256
MEM_HEADS = 4
MEM_HEAD_DIM = 256
D_FF = 5632
N_BRANCH = 3
EPS = 1e-6
NEG = -1e30
PAST_LEN = 1024

SWA_Q = SWA_HEADS * SWA_HEAD_DIM
SWA_KV = SWA_KV_HEADS * SWA_HEAD_DIM
RET_W = RET_HEADS * RET_DIM
MEM_W = MEM_HEADS * MEM_HEAD_DIM
LANE = 128
CONV_ROWS = 64
def _sigmoid(x):
    return 1.0 / (1.0 + jnp.exp(-x))


def _rms(x, g):
    return x * lax.rsqrt(jnp.mean(x * x, axis=-1, keepdims=True) + EPS) * g


def _rmsnorm_kernel(x_ref, g_ref, o_ref):
    o_ref[...] = _rms(x_ref[...], g_ref[...]).astype(o_ref.dtype)


def _rmsnorm(x, g, tm):
    n, d = x.shape
    return pl.pallas_call(
        _rmsnorm_kernel,
        grid=(n // tm,),
        in_specs=[pl.BlockSpec((tm, d), lambda i: (i, 0)),
                  pl.BlockSpec((1, d), lambda i: (0, 0))],
        out_specs=pl.BlockSpec((tm, d), lambda i: (i, 0)),
        out_shape=jax.ShapeDtypeStruct((n, d), BF16),
        name="rmsnorm",
    )(x, g.reshape(1, d))


def _mm_kernel(x_ref, w_ref, *rest, epilogue, tn):
    o_ref = rest[-1]
    acc = jnp.dot(x_ref[...], w_ref[...], preferred_element_type=F32)
    if epilogue == "plain":
        out = acc
    elif epilogue == "scale":
        out = acc * rest[0][...]
    elif epilogue == "silu":
        out = acc * _sigmoid(acc)
    elif epilogue == "sigmoid_bias":
        out = _sigmoid(acc + rest[0][...])
    elif epilogue == "residual":
        out = rest[0][...] + acc
    elif epilogue == "rotary":
        cc, ss = rest[1][...], rest[2][...]
        pieces = []
        for j in range(tn // LANE):
            xh = acc[:, j * LANE:(j + 1) * LANE]
            pieces.append(xh * cc + pltpu.roll(xh, LANE // 2, 1) * ss)
        out = jnp.concatenate(pieces, axis=1) * rest[0][...]
    else:
        raise ValueError(epilogue)
    o_ref[...] = out.astype(o_ref.dtype)


def _mm(x, w, *, epilogue, out_dtype, tm, tn, colvec=None, residual=None, tables=None, name):
    n, k = x.shape
    ncols = w.shape[1]
    in_specs = [pl.BlockSpec((tm, k), lambda i, j: (i, 0)),
                pl.BlockSpec((k, tn), lambda i, j: (0, j))]
    args = [x, w]
    if colvec is not None:
        in_specs.append(pl.BlockSpec((1, tn), lambda i, j: (0, j)))
        args.append(colvec.reshape(1, ncols).astype(F32))
    if residual is not None:
        in_specs.append(pl.BlockSpec((tm, tn), lambda i, j: (i, j)))
        args.append(residual)
    if tables is not None:
        nblk = tables[0].shape[0] // tm
        for t in tables:
            in_specs.append(pl.BlockSpec((tm, LANE), lambda i, j: (i % nblk, 0)))
            args.append(t)
    return pl.pallas_call(
        functools.partial(_mm_kernel, epilogue=epilogue, tn=tn),
        grid=(n // tm, ncols // tn),
        in_specs=in_specs,
        out_specs=pl.BlockSpec((tm, tn), lambda i, j: (i, j)),
        out_shape=jax.ShapeDtypeStruct((n, ncols), out_dtype),
        compiler_params=pltpu.CompilerParams(dimension_semantics=("parallel", "parallel")),
        name=name,
    )(*args)


def _norm_mm_kernel(x_ref, g_ref, w_ref, cv_ref, o_ref, u_ref):
    @pl.when(pl.program_id(1) == 0)
    def _():
        u_ref[...] = _rms(x_ref[...], g_ref[...]).astype(u_ref.dtype)

    acc = jnp.dot(u_ref[...], w_ref[...], preferred_element_type=F32)
    o_ref[...] = (acc * cv_ref[...]).astype(o_ref.dtype)


def _norm_mm(x, g, w, colvec, *, tm, tn, name):
    n, k = x.shape
    ncols = w.shape[1]
    return pl.pallas_call(
        _norm_mm_kernel,
        grid=(n // tm, ncols // tn),
        in_specs=[pl.BlockSpec((tm, k), lambda i, j: (i, 0)),
                  pl.BlockSpec((1, k), lambda i, j: (0, 0)),
                  pl.BlockSpec((k, tn), lambda i, j: (0, j)),
                  pl.BlockSpec((1, tn), lambda i, j: (0, j))],
        out_specs=[pl.BlockSpec((tm, tn), lambda i, j: (i, j)),
                   pl.BlockSpec((tm, k), lambda i, j: (i, 0))],
        out_shape=[jax.ShapeDtypeStruct((n, ncols), BF16), jax.ShapeDtypeStruct((n, k), BF16)],
        compiler_params=pltpu.CompilerParams(dimension_semantics=("parallel", "arbitrary")),
        name=name,
    )(x, g.reshape(1, k), w, colvec.reshape(1, ncols).astype(F32))


def _swa_kernel(sink_ref, q_ref, kp_ref, kc_ref, vp_ref, vc_ref, o_ref, *scratch, qt, kv_off):
    s_scs, p_scs = scratch[:SWA_KV_HEADS], scratch[SWA_KV_HEADS:]
    i = pl.program_id(1)
    nq = qt // CHUNK
    w = (nq + 2) * CHUNK
    wp = -(-w // LANE) * LANE
    first_ok = jnp.where(i * qt + kv_off >= WINDOW, 0, 2)

    def window(p_ref, c_ref):
        parts = [p_ref[0], c_ref[0]]
        if wp > w:
            parts.append(jnp.zeros((wp - w, SWA_KV), F32))
        return jnp.concatenate(parts, axis=0)

    kwin = window(kp_ref, kc_ref)
    vwin = window(vp_ref, vc_ref)
    lane = lax.broadcasted_iota(jnp.int32, (wp, LANE), 1)
    lane_g = lax.broadcasted_iota(jnp.int32, (wp, SWA_KV), 1) // SWA_HEAD_DIM
    out_g = lax.broadcasted_iota(jnp.int32, (CHUNK, SWA_KV), 1) // SWA_HEAD_DIM

    def block_diag(win, h):
        col, half = divmod(h, 2)
        x = win[:, col * LANE:(col + 1) * LANE]
        xr = pltpu.roll(x, SWA_HEAD_DIM, 1)
        lo = lane < SWA_HEAD_DIM
        r = jnp.where(lo, x, xr) if half == 0 else jnp.where(lo, xr, x)
        r2 = jnp.concatenate([r, r], axis=1).astype(BF16)
        zero = jnp.zeros_like(r2)
        return jnp.concatenate([jnp.where(lane_g == g, r2, zero) for g in range(SWA_GROUP)], axis=0)

    for h in range(SWA_KV_HEADS):
        bk = block_diag(kwin, h)
        bv = block_diag(vwin, h)
        qh = q_ref[0, :, h * SWA_KV:(h + 1) * SWA_KV]
        s_sc, p_sc = s_scs[h], p_scs[h]
        s_sc[...] = lax.dot_general(qh, bk, (((1,), (1,)), ((), ())), preferred_element_type=F32)
        invs = []
        for jq in range(nq):
            rows = slice(jq * CHUNK, (jq + 1) * CHUNK)
            c_lo, c_hi = jq * CHUNK // LANE, ((jq + 3) * CHUNK - 1) // LANE
            width = (c_hi - c_lo + 1) * LANE
            kc = c_lo * (LANE // CHUNK) + lax.broadcasted_iota(jnp.int32, (CHUNK, width), 1) // CHUNK
            visible = (kc >= jq) & (kc <= jq + 2) & (kc >= first_ok)
            inv = jnp.zeros((CHUNK, SWA_KV), F32)
            for g in range(SWA_GROUP):
                l0 = g * wp + c_lo * LANE
                sg = jnp.where(visible, s_sc[rows, l0:l0 + width], NEG)
                sk = sink_ref[h * SWA_GROUP + g]
                m = jnp.maximum(jnp.max(sg, axis=1, keepdims=True), sk)
                p = jnp.exp(sg - m)
                den = jnp.sum(p, axis=1, keepdims=True) + jnp.exp(sk - m)
                p_sc[rows, l0:l0 + width] = p.astype(BF16)
                for c in range(wp // LANE):
                    if not c_lo <= c <= c_hi:
                        p_sc[rows, g * wp + c * LANE:g * wp + (c + 1) * LANE] = jnp.zeros((CHUNK, LANE), BF16)
                inv = jnp.where(out_g == g, 1.0 / den, inv)
            invs.append(inv)
        o = jnp.dot(p_sc[...], bv, preferred_element_type=F32)
        o_ref[0, :, h * SWA_KV:(h + 1) * SWA_KV] = (o * jnp.concatenate(invs, axis=0)).astype(o_ref.dtype)


def _swa(q, q_col, k, k_col, v, v_col, sink, *, sq, qt, kv_off):
    b = q.shape[0]
    cur_off = kv_off // qt
    prev_off = kv_off // WINDOW - 1
    per = qt // WINDOW if qt >= WINDOW else 0
    wp = -(-(qt + WINDOW) // LANE) * LANE

    def prev_map(kcol):
        return lambda bi, i, s: (bi, jnp.maximum(i * per + prev_off, 0), kcol)

    def cur_map(kcol):
        return lambda bi, i, s: (bi, i + cur_off, kcol)

    grid_spec = pltpu.PrefetchScalarGridSpec(
        num_scalar_prefetch=1,
        grid=(b, sq // qt),
        in_specs=[pl.BlockSpec((1, qt, SWA_Q), lambda bi, i, s: (bi, i, q_col)),
                  pl.BlockSpec((1, WINDOW, SWA_KV), prev_map(k_col)),
                  pl.BlockSpec((1, qt, SWA_KV), cur_map(k_col)),
                  pl.BlockSpec((1, WINDOW, SWA_KV), prev_map(v_col)),
                  pl.BlockSpec((1, qt, SWA_KV), cur_map(v_col))],
        out_specs=pl.BlockSpec((1, qt, SWA_Q), lambda bi, i, s: (bi, i, 0)),
        scratch_shapes=([pltpu.VMEM((qt, SWA_GROUP * wp), F32)] * SWA_KV_HEADS
                        + [pltpu.VMEM((qt, SWA_GROUP * wp), BF16)] * SWA_KV_HEADS),
    )
    return pl.pallas_call(
        functools.partial(_swa_kernel, qt=qt, kv_off=kv_off),
        grid_spec=grid_spec,
        out_shape=jax.ShapeDtypeStruct((b, sq, SWA_Q), BF16),
        compiler_params=pltpu.CompilerParams(dimension_semantics=("parallel", "parallel")),
        name="swa",
    )(sink.astype(F32), q, k, k, v, v)


def _ret_kernel(lg_ref, q_ref, k_ref, v_ref, g_ref, s0_ref, o_ref, st_ref, dec_ref, xi_ref, zeta_ref, *, c):
    bi = pl.program_id(0)
    ci = pl.program_id(1)

    @pl.when((bi == 0) & (ci == 0))
    def _():
        n = lax.broadcasted_iota(jnp.int32, (c, c), 0)
        m = lax.broadcasted_iota(jnp.int32, (c, c), 1)
        diff = (n - m).astype(F32)
        nrow = lax.broadcasted_iota(jnp.int32, (c, RET_DIM), 0).astype(F32)
        for h in range(RET_HEADS):
            dec_ref[h] = jnp.where(diff >= 0, jnp.exp(lg_ref[h] * jnp.maximum(diff, 0.0)), 0.0)
            xi_ref[h] = jnp.exp(lg_ref[h] * (nrow + 1.0))
            zeta_ref[h] = jnp.exp(lg_ref[h] * (float(c) - 1.0 - nrow))

    @pl.when(ci == 0)
    def _():
        st_ref[...] = s0_ref[...]

    for h in range(RET_HEADS):
        lg = lg_ref[h]
        sl = slice(h * RET_DIM, (h + 1) * RET_DIM)
        qh = q_ref[0, :, sl]
        kh = k_ref[0, :, sl]
        vh = v_ref[0, :, sl]
        state = st_ref[0, h]
        inner = lax.dot_general(qh, kh, (((1,), (1,)), ((), ())), preferred_element_type=F32) * dec_ref[h]
        o = jnp.dot(inner.astype(BF16), vh, preferred_element_type=F32)
        cross = jnp.dot(qh, state.astype(BF16), preferred_element_type=F32)
        o = o + cross * xi_ref[h]
        kz = (kh.astype(F32) * zeta_ref[h]).astype(BF16)
        upd = lax.dot_general(kz, vh, (((0,), (0,)), ((), ())), preferred_element_type=F32)
        decay_c = jnp.exp(lg * jnp.full((1, RET_DIM), float(c), F32))
        st_ref[0, h] = decay_c * state + upd
        on = o * lax.rsqrt(jnp.mean(o * o, axis=-1, keepdims=True) + EPS)
        o_ref[0, :, sl] = (g_ref[0, :, sl].astype(F32) * on).astype(o_ref.dtype)


def _retention(qk, q_col, k_col, v, v_col, gate, state0, log_g, *, s, c):
    b = qk.shape[0]

    def seq(col):
        return pl.BlockSpec((1, c, RET_W), lambda bi, ci, lg: (bi, ci, col))

    st_spec = pl.BlockSpec((1, RET_HEADS, RET_DIM, RET_DIM), lambda bi, ci, lg: (bi, 0, 0, 0))
    grid_spec = pltpu.PrefetchScalarGridSpec(
        num_scalar_prefetch=1,
        grid=(b, s // c),
        in_specs=[seq(q_col), seq(k_col), seq(v_col), seq(0), st_spec],
        out_specs=[seq(0), st_spec],
        scratch_shapes=[pltpu.VMEM((RET_HEADS, c, c), F32), pltpu.VMEM((RET_HEADS, c, RET_DIM), F32),
                        pltpu.VMEM((RET_HEADS, c, RET_DIM), F32)],
    )
    return pl.pallas_call(
        functools.partial(_ret_kernel, c=c),
        grid_spec=grid_spec,
        out_shape=[jax.ShapeDtypeStruct((b, s, RET_W), BF16),
                   jax.ShapeDtypeStruct((b, RET_HEADS, RET_DIM, RET_DIM), F32)],
        compiler_params=pltpu.CompilerParams(dimension_semantics=("arbitrary", "arbitrary")),
        name="retention",
    )(log_g, qk, qk, v, gate, state0)


def _mem_kernel(q_ref, k_ref, v_ref, o_ref):
    for h in range(MEM_HEADS):
        sl = slice(h * MEM_HEAD_DIM, (h + 1) * MEM_HEAD_DIM)
        qh = q_ref[0, :, sl]
        kh = k_ref[0, :, sl].astype(BF16)
        vh = v_ref[0, :, sl].astype(BF16)
        s = lax.dot_general(qh, kh, (((1,), (1,)), ((), ())), preferred_element_type=F32)
        m = jnp.max(s, axis=1, keepdims=True)
        p = jnp.exp(s - m)
        den = jnp.sum(p, axis=1, keepdims=True)
        o = jnp.dot(p.astype(BF16), vh, preferred_element_type=F32)
        o_ref[0, :, sl] = (o * (1.0 / den)).astype(o_ref.dtype)


def _mem_attend(q, q_col, mk, mv, *, s, tq):
    b = q.shape[0]
    kv_spec = pl.BlockSpec((1, N_MEM, MEM_W), lambda bi, i: (bi, 0, 0))
    return pl.pallas_call(
        _mem_kernel,
        grid=(b, s // tq),
        in_specs=[pl.BlockSpec((1, tq, MEM_W), lambda bi, i: (bi, i, q_col)), kv_spec, kv_spec],
        out_specs=pl.BlockSpec((1, tq, MEM_W), lambda bi, i: (bi, i, 0)),
        out_shape=jax.ShapeDtypeStruct((b, s, MEM_W), BF16),
        compiler_params=pltpu.CompilerParams(dimension_semantics=("parallel", "parallel")),
        name="mem_attend",
    )(q, mk, mv)


def _merge_kernel(u_ref, oa_ref, ob_ref, oc_ref, wga_ref, wgb_ref, wgc_ref, ba_ref, bb_ref, bc_ref,
                  wa_ref, wb_ref, wc_ref, o_ref):
    u = u_ref[...]
    acc = None
    for o_r, wg_ref, b_ref, w_ref in ((oa_ref, wga_ref, ba_ref, wa_ref), (ob_ref, wgb_ref, bb_ref, wb_ref),
                                      (oc_ref, wgc_ref, bc_ref, wc_ref)):
        gate = _sigmoid(jnp.dot(u, wg_ref[...], preferred_element_type=F32) + b_ref[...])
        term = gate * jnp.dot(o_r[...], w_ref[...], preferred_element_type=F32)
        acc = term if acc is None else acc + term
    o_ref[...] = acc.astype(o_ref.dtype)


def _merge(u, o_swa, o_ret, o_mem, w_gate, b_gate, w_br, *, tm, tn):
    n, d = u.shape
    kb = o_swa.shape[1]
    nj = D_MODEL // tn
    o_spec = pl.BlockSpec((tm, kb), lambda i, j: (i, 0))

    def wg_spec(br):
        return pl.BlockSpec((d, tn), lambda i, j: (0, br * nj + j))

    def b_spec(br):
        return pl.BlockSpec((1, tn), lambda i, j: (0, br * nj + j))

    def w_spec(br):
        return pl.BlockSpec((kb, tn), lambda i, j: (br, j))

    bias = b_gate.reshape(1, -1).astype(F32)
    return pl.pallas_call(
        _merge_kernel,
        grid=(n // tm, nj),
        in_specs=[pl.BlockSpec((tm, d), lambda i, j: (i, 0)), o_spec, o_spec, o_spec,
                  wg_spec(0), wg_spec(1), wg_spec(2), b_spec(0), b_spec(1), b_spec(2),
                  w_spec(0), w_spec(1), w_spec(2)],
        out_specs=pl.BlockSpec((tm, tn), lambda i, j: (i, j)),
        out_shape=jax.ShapeDtypeStruct((n, D_MODEL), BF16),
        compiler_params=pltpu.CompilerParams(dimension_semantics=("parallel", "parallel")),
        name="merge",
    )(u, o_swa, o_ret, o_mem, w_gate, w_gate, w_gate, bias, bias, bias, w_br, w_br, w_br)


def _ffn_up_kernel(u_ref, wug_ref, wuv_ref, wc_ref, b_ref, cb_ref, act_ref, co_ref,
                   ag_sc, av_sc, cg_sc, cv_sc, *, tm, tf, nf, nseg):
    i = pl.program_id(1)
    fb = pl.program_id(2)
    seg_t = tm // nseg
    stride = seg_t + 8
    if nseg == 1:
        @pl.when((i == 0) & (fb == 0))
        def _():
            for f in range(nf):
                cg_sc[f, 6:8, :] = cb_ref[0, f]
                cv_sc[f, 6:8, :] = cb_ref[0, nf + f]

    u = u_ref[0]
    for a_sc, wu_ref, blk, carry_sc in ((ag_sc, wug_ref, fb, cg_sc), (av_sc, wuv_ref, nf + fb, cv_sc)):
        a = jnp.dot(u, wu_ref[...], preferred_element_type=F32)
        for s in range(nseg):
            a_sc[s * stride + 8:(s + 1) * stride, :] = a[s * seg_t:(s + 1) * seg_t]
            co_ref[0, s, blk] = a[(s + 1) * seg_t - 8:(s + 1) * seg_t]
        if nseg == 1:
            a_sc[6:8, :] = carry_sc[fb, 6:8, :]
            carry_sc[fb] = a[tm - 8:tm]
        else:
            for s in range(nseg):
                a_sc[s * stride + 6:s * stride + 8, :] = cb_ref[s, blk]

    def conv(a_sc, blk, r0, c0):
        tap = lambda k: a_sc[r0 - k:r0 - k + CONV_ROWS, c0:c0 + LANE]
        wc = lambda k: wc_ref[blk, k:k + 1, c0:c0 + LANE]
        return b_ref[blk, :, c0:c0 + LANE] + (wc(0) * tap(2) + wc(1) * tap(1) + wc(2) * tap(0))

    for c0 in range(0, tf, LANE):
        for s in range(nseg):
            for r in range(0, seg_t, CONV_ROWS):
                cg = conv(ag_sc, fb, s * stride + 8 + r, c0)
                cv = conv(av_sc, nf + fb, s * stride + 8 + r, c0)
                act_ref[0, s * seg_t + r:s * seg_t + r + CONV_ROWS, c0:c0 + LANE] = (
                    cg * _sigmoid(cg) * cv).astype(act_ref.dtype)


def _ffn_up(u, conv_buf, w_up, w_conv, b_conv, *, tm, tf, nseg):
    bt, rows, d = u.shape
    nf = D_FF // tf
    nt = rows // tm
    nb = bt * nseg
    wc_blocks = w_conv.reshape(3, 2 * nf, tf).transpose(1, 0, 2)
    b_blocks = b_conv.reshape(2 * nf, 1, tf)
    cb_blocks = conv_buf.reshape(nb, 2, 2 * nf, tf).transpose(0, 2, 1, 3)
    whole = lambda *shape: pl.BlockSpec(shape, lambda b, i, f: (0,) * len(shape))
    return pl.pallas_call(
        functools.partial(_ffn_up_kernel, tm=tm, tf=tf, nf=nf, nseg=nseg),
        grid=(bt, nt, nf),
        in_specs=[pl.BlockSpec((1, tm, d), lambda b, i, f: (b, i, 0)),
                  pl.BlockSpec((d, tf), lambda b, i, f: (0, f)),
                  pl.BlockSpec((d, tf), lambda b, i, f: (0, nf + f)),
                  whole(2 * nf, 3, tf),
                  whole(2 * nf, 1, tf),
                  pl.BlockSpec((nseg, 2 * nf, 2, tf), lambda b, i, f: (b, 0, 0, 0))],
        out_specs=[pl.BlockSpec((1, tm, tf), lambda b, i, f: (b, i, f)),
                   pl.BlockSpec((1, nseg, 2 * nf, 8, tf), lambda b, i, f: (b * nt + i, 0, 0, 0, 0))],
        out_shape=[jax.ShapeDtypeStruct((bt, rows, D_FF), BF16),
                   jax.ShapeDtypeStruct((bt * nt, nseg, 2 * nf, 8, tf), F32)],
        scratch_shapes=[pltpu.VMEM((tm + 8 * nseg, tf), F32), pltpu.VMEM((tm + 8 * nseg, tf), F32),
                        pltpu.VMEM((nf, 8, tf), F32), pltpu.VMEM((nf, 8, tf), F32)],
        compiler_params=pltpu.CompilerParams(dimension_semantics=("arbitrary", "arbitrary", "arbitrary")),
        name="ffn_up",
    )(u, w_up, w_up, wc_blocks, b_blocks, cb_blocks)


def _ffn_down_kernel(act_ref, wd_ref, h_ref, g_ref, y_ref, *, nd, td):
    j = pl.program_id(1)
    for jj in range(nd):
        @pl.when(j == jj)
        def _():
            y_ref[:, jj * td:(jj + 1) * td] = h_ref[...] + jnp.dot(act_ref[...], wd_ref[...],
                                                                   preferred_element_type=F32)

    @pl.when(j == nd - 1)
    def _():
        y_ref[...] = _rms(y_ref[...], g_ref[...])


def _ffn_down(act, w_down, h, g_final, *, tm, td):
    n, f = act.shape
    d = h.shape[1]
    nd = d // td
    return pl.pallas_call(
        functools.partial(_ffn_down_kernel, nd=nd, td=td),
        grid=(n // tm, nd),
        in_specs=[pl.BlockSpec((tm, f), lambda i, j: (i, 0)),
                  pl.BlockSpec((f, td), lambda i, j: (0, j)),
                  pl.BlockSpec((tm, td), lambda i, j: (i, j)),
                  pl.BlockSpec((1, d), lambda i, j: (0, 0))],
        out_specs=pl.BlockSpec((tm, d), lambda i, j: (i, 0)),
        out_shape=jax.ShapeDtypeStruct((n, d), F32),
        compiler_params=pltpu.CompilerParams(dimension_semantics=("parallel", "arbitrary")),
        name="ffn_down",
    )(act, w_down, h, g_final.reshape(1, d))


def _out_proj_kernel(x_ref, w_ref, res_ref, g_ref, h_ref, u_ref):
    h = res_ref[...] + jnp.dot(x_ref[...], w_ref[...], preferred_element_type=F32)
    h_ref[...] = h
    u_ref[...] = _rms(h, g_ref[...]).astype(u_ref.dtype)


def _out_proj(x, w, residual, g, *, tm):
    n, k = x.shape
    d = w.shape[1]
    row = lambda width: pl.BlockSpec((tm, width), lambda i: (i, 0))
    return pl.pallas_call(
        _out_proj_kernel,
        grid=(n // tm,),
        in_specs=[row(k), pl.BlockSpec((k, d), lambda i: (0, 0)), row(d), pl.BlockSpec((1, d), lambda i: (0, 0))],
        out_specs=[row(d), row(d)],
        out_shape=[jax.ShapeDtypeStruct((n, d), F32), jax.ShapeDtypeStruct((n, d), BF16)],
        compiler_params=pltpu.CompilerParams(dimension_semantics=("parallel",)),
        name="proj_out",
    )(x, w, residual, g.reshape(1, d))


def _rope_tables(pos0, s):
    half = RET_DIM // 2
    inv_freq = 1.0 / (RET_ROPE_BASE ** jnp.linspace(0.0, 1.0, half, dtype=F32))
    ang = (pos0 + jnp.arange(s)).astype(F32)[:, None] * inv_freq[None, :]
    cos, sin = jnp.cos(ang), jnp.sin(ang)
    return jnp.concatenate([cos, cos], axis=1), jnp.concatenate([-sin, sin], axis=1)


def _prep_weights(g_mix, w_in, b_gate, sink, w_br, w_o, g_ffn, w_up, w_conv, b_conv, w_down):
    o = 0
    cuts = {}
    for name, width in (("qa", SWA_Q), ("ka", SWA_KV), ("va", SWA_KV), ("qr", RET_W), ("kr", RET_W),
                        ("vr", RET_W), ("gr", RET_W), ("qm", MEM_W), ("gl", N_BRANCH * D_MODEL)):
        cuts[name] = (o, o + width)
        o += width
    cols = lambda *names: jnp.concatenate([w_in[:, cuts[n][0]:cuts[n][1]] for n in names], axis=1).astype(BF16)
    ones = lambda n, v: jnp.full((n,), v, F32)
    return dict(
        g_mix=g_mix, b_gate=b_gate, sink=sink, g_ffn=g_ffn, w_conv=w_conv, b_conv=b_conv,
        w_plain=cols("qa", "vr", "qm"),
        s_plain=jnp.concatenate([ones(SWA_Q, SWA_HEAD_DIM ** -0.5), ones(RET_W, 1.0),
                                 ones(MEM_W, MEM_HEAD_DIM ** -0.5)]),
        w_kv=cols("ka", "va"),
        w_rot=cols("qr", "kr"),
        s_rot=jnp.concatenate([ones(RET_W, 1.0), ones(RET_W, RET_DIM ** -0.5)]),
        w_g=cols("gr"),
        w_gate=cols("gl"),
        w_br=w_br.astype(BF16), w_o=w_o.astype(BF16), w_up=w_up.astype(BF16), w_down=w_down.astype(BF16),
    )


def _run_group(x, pos0, mem_k, mem_v, swa_cache, ret_state, conv_buf, wts, g_final, log_g):
    b, s, d = x.shape
    n = b * s
    tm = min(1024, n)
    x2 = x.reshape(n, d)
    plain, u = _norm_mm(x2, wts["g_mix"], wts["w_plain"], wts["s_plain"], tm=tm, tn=1024, name="proj_plain")
    kv = _mm(u, wts["w_kv"], epilogue="plain", out_dtype=F32, tm=tm, tn=2 * SWA_KV, name="proj_kv")
    cc, ss = _rope_tables(pos0, s)
    if s < tm:
        cc, ss = jnp.tile(cc, (tm // s, 1)), jnp.tile(ss, (tm // s, 1))
    rot = _mm(u, wts["w_rot"], epilogue="rotary", out_dtype=BF16, tm=tm, tn=1024,
              colvec=wts["s_rot"], tables=(cc, ss), name="proj_rot")
    sgate = _mm(u, wts["w_g"], epilogue="silu", out_dtype=BF16, tm=tm, tn=1024, name="proj_silu")

    plain3 = plain.reshape(b, s, -1)
    kv3 = kv.reshape(b, s, 2 * SWA_KV)
    if swa_cache is None:
        o_swa = _swa(plain3, 0, kv3, 0, kv3, 1, wts["sink"], sq=s, qt=min(256, s), kv_off=0)
        new_k = kv3[:, s - WINDOW:, :SWA_KV]
        new_v = kv3[:, s - WINDOW:, SWA_KV:]
    else:
        ck = swa_cache[0].reshape(b, -1, SWA_KV)
        cv = swa_cache[1].reshape(b, -1, SWA_KV)
        n_keep = ck.shape[1]
        k_all = jnp.concatenate([ck, kv3[:, :, :SWA_KV]], axis=1)
        v_all = jnp.concatenate([cv, kv3[:, :, SWA_KV:]], axis=1)
        o_swa = _swa(plain3, 0, k_all, 0, v_all, 0, wts["sink"], sq=s, qt=s, kv_off=n_keep)
        new_k = k_all[:, -n_keep:]
        new_v = v_all[:, -n_keep:]
    state0 = jnp.zeros((b, RET_HEADS, RET_DIM, RET_DIM), F32) if ret_state is None else ret_state
    o_ret, s_new = _retention(rot.reshape(b, s, -1), 0, 1, plain3, 1, sgate.reshape(b, s, -1), state0, log_g,
                              s=s, c=min(256, s))
    o_mem = _mem_attend(plain3, 2, mem_k.reshape(b, N_MEM, MEM_W), mem_v.reshape(b, N_MEM, MEM_W),
                        s=s, tq=min(512, s))
    merged = _merge(u, o_swa.reshape(n, -1), o_ret.reshape(n, -1), o_mem.reshape(n, -1), wts["w_gate"],
                    wts["b_gate"], wts["w_br"], tm=tm, tn=512)
    h1, u2 = _out_proj(merged, wts["w_o"], x2, wts["g_ffn"], tm=min(512, n))
    if s >= 1024:
        act, co = _ffn_up(u2.reshape(b, s, d), conv_buf, wts["w_up"], wts["w_conv"], wts["b_conv"],
                          tm=1024, tf=512, nseg=1)
    else:
        act, co = _ffn_up(u2.reshape(1, n, d), conv_buf, wts["w_up"], wts["w_conv"], wts["b_conv"],
                          tm=n, tf=512, nseg=b)
    y = _ffn_down(act.reshape(n, D_FF), wts["w_down"], h1, g_final, tm=min(1024, n), td=512)
    nblk, tfb = co.shape[2], co.shape[4]
    last = co.reshape(b, -1, nblk, 8, tfb)[:, -1, :, 6:8, :]
    new_buf = last.transpose(0, 2, 1, 3).reshape(b, 2, nblk * tfb)
    return (y.reshape(b, s, d), new_k.reshape(b, -1, SWA_KV_HEADS, SWA_HEAD_DIM),
            new_v.reshape(b, -1, SWA_KV_HEADS, SWA_HEAD_DIM), s_new, new_buf)


def _memory_kv(mem, g_mem, w_mem_kv):
    b, m, d = mem.shape
    u = _rmsnorm(mem.reshape(b * m, d), g_mem, min(512, b * m))
    kv = _mm(u, w_mem_kv.astype(BF16), epilogue="plain", out_dtype=F32, tm=min(1024, b * m), tn=1024, name="mem_kv")
    return (kv[:, :MEM_W].reshape(b, m, MEM_HEADS, MEM_HEAD_DIM), kv[:, MEM_W:].reshape(b, m, MEM_HEADS, MEM_HEAD_DIM))


def kernel(x_prompt, x_sample, mem_prompt, cache_swa_k, cache_swa_v, state_ret, state_ffn_conv, cache_mem_k, cache_mem_v, g_mix, w_in, b_gate, sink, w_br, w_o, g_mem, w_mem_kv, g_ffn, w_up, w_conv, b_conv, w_down, g_final):
    bp = x_prompt.shape[0]
    depth = w_in.shape[0]
    log_g = jnp.log1p(-jnp.exp2(-5.0 - jnp.arange(RET_HEADS, dtype=F32)))
    hp, hs = x_prompt, x_sample
    outs_p = [[] for _ in range(6)]
    outs_s = [[] for _ in range(4)]
    for l in range(depth):
        wts = _prep_weights(g_mix[l], w_in[l], b_gate[l], sink[l], w_br[l], w_o[l],
                            g_ffn[l], w_up[l], w_conv[l], b_conv[l], w_down[l])
        mk, mv = _memory_kv(mem_prompt, g_mem[l], w_mem_kv[l])
        zero_buf = jnp.zeros((bp, 2, 2 * D_FF), F32)
        assert depth == 1
        hp, kp, vp, sp, cp = _run_group(hp, 0, mk, mv, None, None, zero_buf, wts, g_final, log_g)
        hs, ksn, vsn, ssn, csn = _run_group(hs, PAST_LEN, cache_mem_k[l], cache_mem_v[l],
                                            (cache_swa_k[l], cache_swa_v[l]), state_ret[l],
                                            state_ffn_conv[l], wts, g_final, log_g)
        for lst, val in zip(outs_p, (kp, vp, sp, cp, mk, mv)):
            lst.append(val)
        for lst, val in zip(outs_s, (ksn, vsn, ssn, csn)):
            lst.append(val)
    return (hp, hs, *[jnp.stack(v) for v in outs_p], *[jnp.stack(v) for v in outs_s])
```

```python
import functools

import jax
import jax.numpy as jnp
import numpy as np
from jax import lax
from jax.experimental import pallas as pl
from jax.experimental.pallas import tpu as pltpu

F32 = jnp.float32
BF16 = jnp.bfloat16

D_MODEL = 2048
CHUNK = 64
WINDOW = 128
SWA_HEADS = 16
SWA_KV_HEADS = 4
SWA_GROUP = SWA_HEADS // SWA_KV_HEADS
SWA_HEAD_DIM = 64
RET_HEADS = 8
RET_DIM = 128
RET_ROPE_BASE = 10000.0
N_MEM = 256
MEM_HEADS = 4
MEM_HEAD_DIM = 256
D_FF = 5632
N_BRANCH = 3
EPS = 1e-6
NEG = -1e30
PAST_LEN = 1024

SWA_Q = SWA_HEADS * SWA_HEAD_DIM
SWA_KV = SWA_KV_HEADS * SWA_HEAD_DIM
RET_W = RET_HEADS * RET_DIM
MEM_W = MEM_HEADS * MEM_HEAD_DIM
LANE = 128
CONV_ROWS = 64
def _sigmoid(x):
    return 1.0 / (1.0 + jnp.exp(-x))


def _rms(x, g):
    return x * lax.rsqrt(jnp.mean(x * x, axis=-1, keepdims=True) + EPS) * g


def _rmsnorm_kernel(x_ref, g_ref, o_ref):
    o_ref[...] = _rms(x_ref[...], g_ref[...]).astype(o_ref.dtype)


def _rmsnorm(x, g, tm):
    n, d = x.shape
    return pl.pallas_call(
        _rmsnorm_kernel,
        grid=(n // tm,),
        in_specs=[pl.BlockSpec((tm, d), lambda i: (i, 0)),
                  pl.BlockSpec((1, d), lambda i: (0, 0))],
        out_specs=pl.BlockSpec((tm, d), lambda i: (i, 0)),
        out_shape=jax.ShapeDtypeStruct((n, d), BF16),
        name="rmsnorm",
    )(x, g.reshape(1, d))


def _mm_kernel(x_ref, w_ref, *rest, epilogue, tn):
    o_ref = rest[-1]
    acc = jnp.dot(x_ref[...], w_ref[...], preferred_element_type=F32)
    if epilogue == "plain":
        out = acc
    elif epilogue == "scale":
        out = acc * rest[0][...]
    elif epilogue == "silu":
        out = acc * _sigmoid(acc)
    elif epilogue == "sigmoid_bias":
        out = _sigmoid(acc + rest[0][...])
    elif epilogue == "residual":
        out = rest[0][...] + acc
    elif epilogue == "rotary":
        cc, ss = rest[1][...], rest[2][...]
        pieces = []
        for j in range(tn // LANE):
            xh = acc[:, j * LANE:(j + 1) * LANE]
            pieces.append(xh * cc + pltpu.roll(xh, LANE // 2, 1) * ss)
        out = jnp.concatenate(pieces, axis=1) * rest[0][...]
    else:
        raise ValueError(epilogue)
    o_ref[...] = out.astype(o_ref.dtype)


def _mm(x, w, *, epilogue, out_dtype, tm, tn, colvec=None, residual=None, tables=None, name):
    n, k = x.shape
    ncols = w.shape[1]
    in_specs = [pl.BlockSpec((tm, k), lambda i, j: (i, 0)),
                pl.BlockSpec((k, tn), lambda i, j: (0, j))]
    args = [x, w]
    if colvec is not None:
        in_specs.append(pl.BlockSpec((1, tn), lambda i, j: (0, j)))
        args.append(colvec.reshape(1, ncols).astype(F32))
    if residual is not None:
        in_specs.append(pl.BlockSpec((tm, tn), lambda i, j: (i, j)))
        args.append(residual)
    if tables is not None:
        nblk = tables[0].shape[0] // tm
        for t in tables:
            in_specs.append(pl.BlockSpec((tm, LANE), lambda i, j: (i % nblk, 0)))
            args.append(t)
    return pl.pallas_call(
        functools.partial(_mm_kernel, epilogue=epilogue, tn=tn),
        grid=(n // tm, ncols // tn),
        in_specs=in_specs,
        out_specs=pl.BlockSpec((tm, tn), lambda i, j: (i, j)),
        out_shape=jax.ShapeDtypeStruct((n, ncols), out_dtype),
        compiler_params=pltpu.CompilerParams(dimension_semantics=("parallel", "parallel")),
        name=name,
    )(*args)


def _norm_mm_kernel(x_ref, g_ref, w_ref, cv_ref, o_ref, u_ref):
    @pl.when(pl.program_id(1) == 0)
    def _():
        u_ref[...] = _rms(x_ref[...], g_ref[...]).astype(u_ref.dtype)

    acc = jnp.dot(u_ref[...], w_ref[...], preferred_element_type=F32)
    o_ref[...] = (acc * cv_ref[...]).astype(o_ref.dtype)


def _norm_mm(x, g, w, colvec, *, tm, tn, name):
    n, k = x.shape
    ncols = w.shape[1]
    return pl.pallas_call(
        _norm_mm_kernel,
        grid=(n // tm, ncols // tn),
        in_specs=[pl.BlockSpec((tm, k), lambda i, j: (i, 0)),
                  pl.BlockSpec((1, k), lambda i, j: (0, 0)),
                  pl.BlockSpec((k, tn), lambda i, j: (0, j)),
                  pl.BlockSpec((1, tn), lambda i, j: (0, j))],
        out_specs=[pl.BlockSpec((tm, tn), lambda i, j: (i, j)),
                   pl.BlockSpec((tm, k), lambda i, j: (i, 0))],
        out_shape=[jax.ShapeDtypeStruct((n, ncols), BF16), jax.ShapeDtypeStruct((n, k), BF16)],
        compiler_params=pltpu.CompilerParams(dimension_semantics=("parallel", "arbitrary")),
        name=name,
    )(x, g.reshape(1, k), w, colvec.reshape(1, ncols).astype(F32))


def _swa_kernel(sink_ref, q_ref, kp_ref, kc_ref, vp_ref, vc_ref, o_ref, *scratch, qt, kv_off):
    s_scs, p_scs = scratch[:SWA_KV_HEADS], scratch[SWA_KV_HEADS:]
    i = pl.program_id(1)
    nq = qt // CHUNK
    w = (nq + 2) * CHUNK
    wp = -(-w // LANE) * LANE
    first_ok = jnp.where(i * qt + kv_off >= WINDOW, 0, 2)

    def window(p_ref, c_ref):
        parts = [p_ref[0], c_ref[0]]
        if wp > w:
            parts.append(jnp.zeros((wp - w, SWA_KV), F32))
        return jnp.concatenate(parts, axis=0)

    kwin = window(kp_ref, kc_ref)
    vwin = window(vp_ref, vc_ref)
    lane = lax.broadcasted_iota(jnp.int32, (wp, LANE), 1)
    lane_g = lax.broadcasted_iota(jnp.int32, (wp, SWA_KV), 1) // SWA_HEAD_DIM
    out_g = lax.broadcasted_iota(jnp.int32, (CHUNK, SWA_KV), 1) // SWA_HEAD_DIM

    def block_diag(win, h):
        col, half = divmod(h, 2)
        x = win[:, col * LANE:(col + 1) * LANE]
        xr = pltpu.roll(x, SWA_HEAD_DIM, 1)
        lo = lane < SWA_HEAD_DIM
        r = jnp.where(lo, x, xr) if half == 0 else jnp.where(lo, xr, x)
        r2 = jnp.concatenate([r, r], axis=1).astype(BF16)
        zero = jnp.zeros_like(r2)
        return jnp.concatenate([jnp.where(lane_g == g, r2, zero) for g in range(SWA_GROUP)], axis=0)

    for h in range(SWA_KV_HEADS):
        bk = block_diag(kwin, h)
        bv = block_diag(vwin, h)
        qh = q_ref[0, :, h * SWA_KV:(h + 1) * SWA_KV]
        s_sc, p_sc = s_scs[h], p_scs[h]
        s_sc[...] = lax.dot_general(qh, bk, (((1,), (1,)), ((), ())), preferred_element_type=F32)
        invs = []
        for jq in range(nq):
            rows = slice(jq * CHUNK, (jq + 1) * CHUNK)
            c_lo, c_hi = jq * CHUNK // LANE, ((jq + 3) * CHUNK - 1) // LANE
            width = (c_hi - c_lo + 1) * LANE
            kc = c_lo * (LANE // CHUNK) + lax.broadcasted_iota(jnp.int32, (CHUNK, width), 1) // CHUNK
            visible = (kc >= jq) & (kc <= jq + 2) & (kc >= first_ok)
            inv = jnp.zeros((CHUNK, SWA_KV), F32)
            for g in range(SWA_GROUP):
                l0 = g * wp + c_lo * LANE
                sg = jnp.where(visible, s_sc[rows, l0:l0 + width], NEG)
                sk = sink_ref[h * SWA_GROUP + g]
                m = jnp.maximum(jnp.max(sg, axis=1, keepdims=True), sk)
                p = jnp.exp(sg - m)
                den = jnp.sum(p, axis=1, keepdims=True) + jnp.exp(sk - m)
                p_sc[rows, l0:l0 + width] = p.astype(BF16)
                for c in range(wp // LANE):
                    if not c_lo <= c <= c_hi:
                        p_sc[rows, g * wp + c * LANE:g * wp + (c + 1) * LANE] = jnp.zeros((CHUNK, LANE), BF16)
                inv = jnp.where(out_g == g, 1.0 / den, inv)
            invs.append(inv)
        o = jnp.dot(p_sc[...], bv, preferred_element_type=F32)
        o_ref[0, :, h * SWA_KV:(h + 1) * SWA_KV] = (o * jnp.concatenate(invs, axis=0)).astype(o_ref.dtype)


def _swa(q, q_col, k, k_col, v, v_col, sink, *, sq, qt, kv_off):
    b = q.shape[0]
    cur_off = kv_off // qt
    prev_off = kv_off // WINDOW - 1
    per = qt // WINDOW if qt >= WINDOW else 0
    wp = -(-(qt + WINDOW) // LANE) * LANE

    def prev_map(kcol):
        return lambda bi, i, s: (bi, jnp.maximum(i * per + prev_off, 0), kcol)

    def cur_map(kcol):
        return lambda bi, i, s: (bi, i + cur_off, kcol)

    grid_spec = pltpu.PrefetchScalarGridSpec(
        num_scalar_prefetch=1,
        grid=(b, sq // qt),
        in_specs=[pl.BlockSpec((1, qt, SWA_Q), lambda bi, i, s: (bi, i, q_col)),
                  pl.BlockSpec((1, WINDOW, SWA_KV), prev_map(k_col)),
                  pl.BlockSpec((1, qt, SWA_KV), cur_map(k_col)),
                  pl.BlockSpec((1, WINDOW, SWA_KV), prev_map(v_col)),
                  pl.BlockSpec((1, qt, SWA_KV), cur_map(v_col))],
        out_specs=pl.BlockSpec((1, qt, SWA_Q), lambda bi, i, s: (bi, i, 0)),
        scratch_shapes=([pltpu.VMEM((qt, SWA_GROUP * wp), F32)] * SWA_KV_HEADS
                        + [pltpu.VMEM((qt, SWA_GROUP * wp), BF16)] * SWA_KV_HEADS),
    )
    return pl.pallas_call(
        functools.partial(_swa_kernel, qt=qt, kv_off=kv_off),
        grid_spec=grid_spec,
        out_shape=jax.ShapeDtypeStruct((b, sq, SWA_Q), BF16),
        compiler_params=pltpu.CompilerParams(dimension_semantics=("parallel", "parallel")),
        name="swa",
    )(sink.astype(F32), q, k, k, v, v)


def _ret_kernel(lg_ref, q_ref, k_ref, v_ref, g_ref, s0_ref, o_ref, st_ref, dec_ref, xi_ref, zeta_ref, *, c):
    bi = pl.program_id(0)
    ci = pl.program_id(1)

    @pl.when((bi == 0) & (ci == 0))
    def _():
        n = lax.broadcasted_iota(jnp.int32, (c, c), 0)
        m = lax.broadcasted_iota(jnp.int32, (c, c), 1)
        diff = (n - m).astype(F32)
        nrow = lax.broadcasted_iota(jnp.int32, (c, RET_DIM), 0).astype(F32)
        for h in range(RET_HEADS):
            dec_ref[h] = jnp.where(diff >= 0, jnp.exp(lg_ref[h] * jnp.maximum(diff, 0.0)), 0.0)
            xi_ref[h] = jnp.exp(lg_ref[h] * (nrow + 1.0))
            zeta_ref[h] = jnp.exp(lg_ref[h] * (float(c) - 1.0 - nrow))

    @pl.when(ci == 0)
    def _():
        st_ref[...] = s0_ref[...]

    for h in range(RET_HEADS):
        lg = lg_ref[h]
        sl = slice(h * RET_DIM, (h + 1) * RET_DIM)
        qh = q_ref[0, :, sl]
        kh = k_ref[0, :, sl]
        vh = v_ref[0, :, sl]
        state = st_ref[0, h]
        inner = lax.dot_general(qh, kh, (((1,), (1,)), ((), ())), preferred_element_type=F32) * dec_ref[h]
        o = jnp.dot(inner.astype(BF16), vh, preferred_element_type=F32)
        cross = jnp.dot(qh, state.astype(BF16), preferred_element_type=F32)
        o = o + cross * xi_ref[h]
        kz = (kh.astype(F32) * zeta_ref[h]).astype(BF16)
        upd = lax.dot_general(kz, vh, (((0,), (0,)), ((), ())), preferred_element_type=F32)
        decay_c = jnp.exp(lg * jnp.full((1, RET_DIM), float(c), F32))
        st_ref[0, h] = decay_c * state + upd
        on = o * lax.rsqrt(jnp.mean(o * o, axis=-1, keepdims=True) + EPS)
        o_ref[0, :, sl] = (g_ref[0, :, sl].astype(F32) * on).astype(o_ref.dtype)


def _retention(qk, q_col, k_col, v, v_col, gate, state0, log_g, *, s, c):
    b = qk.shape[0]

    def seq(col):
        return pl.BlockSpec((1, c, RET_W), lambda bi, ci, lg: (bi, ci, col))

    st_spec = pl.BlockSpec((1, RET_HEADS, RET_DIM, RET_DIM), lambda bi, ci, lg: (bi, 0, 0, 0))
    grid_spec = pltpu.PrefetchScalarGridSpec(
        num_scalar_prefetch=1,
        grid=(b, s // c),
        in_specs=[seq(q_col), seq(k_col), seq(v_col), seq(0), st_spec],
        out_specs=[seq(0), st_spec],
        scratch_shapes=[pltpu.VMEM((RET_HEADS, c, c), F32), pltpu.VMEM((RET_HEADS, c, RET_DIM), F32),
                        pltpu.VMEM((RET_HEADS, c, RET_DIM), F32)],
    )
    return pl.pallas_call(
        functools.partial(_ret_kernel, c=c),
        grid_spec=grid_spec,
        out_shape=[jax.ShapeDtypeStruct((b, s, RET_W), BF16),
                   jax.ShapeDtypeStruct((b, RET_HEADS, RET_DIM, RET_DIM), F32)],
        compiler_params=pltpu.CompilerParams(dimension_semantics=("arbitrary", "arbitrary")),
        name="retention",
    )(log_g, qk, qk, v, gate, state0)


def _mem_kernel(q_ref, k_ref, v_ref, o_ref):
    for h in range(MEM_HEADS):
        sl = slice(h * MEM_HEAD_DIM, (h + 1) * MEM_HEAD_DIM)
        qh = q_ref[0, :, sl]
        kh = k_ref[0, :, sl].astype(BF16)
        vh = v_ref[0, :, sl].astype(BF16)
        s = lax.dot_general(qh, kh, (((1,), (1,)), ((), ())), preferred_element_type=F32)
        m = jnp.max(s, axis=1, keepdims=True)
        p = jnp.exp(s - m)
        den = jnp.sum(p, axis=1, keepdims=True)
        o = jnp.dot(p.astype(BF16), vh, preferred_element_type=F32)
        o_ref[0, :, sl] = (o * (1.0 / den)).astype(o_ref.dtype)


def _mem_attend(q, q_col, mk, mv, *, s, tq):
    b = q.shape[0]
    kv_spec = pl.BlockSpec((1, N_MEM, MEM_W), lambda bi, i: (bi, 0, 0))
    return pl.pallas_call(
        _mem_kernel,
        grid=(b, s // tq),
        in_specs=[pl.BlockSpec((1, tq, MEM_W), lambda bi, i: (bi, i, q_col)), kv_spec, kv_spec],
        out_specs=pl.BlockSpec((1, tq, MEM_W), lambda bi, i: (bi, i, 0)),
        out_shape=jax.ShapeDtypeStruct((b, s, MEM_W), BF16),
        compiler_params=pltpu.CompilerParams(dimension_semantics=("parallel", "parallel")),
        name="mem_attend",
    )(q, mk, mv)


def _merge_kernel(u_ref, oa_ref, ob_ref, oc_ref, wga_ref, wgb_ref, wgc_ref, ba_ref, bb_ref, bc_ref,
                  wa_ref, wb_ref, wc_ref, o_ref):
    u = u_ref[...]
    acc = None
    for o_r, wg_ref, b_ref, w_ref in ((oa_ref, wga_ref, ba_ref, wa_ref), (ob_ref, wgb_ref, bb_ref, wb_ref),
                                      (oc_ref, wgc_ref, bc_ref, wc_ref)):
        gate = _sigmoid(jnp.dot(u, wg_ref[...], preferred_element_type=F32) + b_ref[...])
        term = gate * jnp.dot(o_r[...], w_ref[...], preferred_element_type=F32)
        acc = term if acc is None else acc + term
    o_ref[...] = acc.astype(o_ref.dtype)


def _merge(u, o_swa, o_ret, o_mem, w_gate, gate_col0, b_gate, w_br, *, tm, tn):
    n, d = u.shape
    kb = o_swa.shape[1]
    nj = D_MODEL // tn
    g0 = gate_col0 // tn
    o_spec = pl.BlockSpec((tm, kb), lambda i, j: (i, 0))

    def wg_spec(br):
        return pl.BlockSpec((d, tn), lambda i, j: (0, g0 + br * nj + j))

    def b_spec(br):
        return pl.BlockSpec((1, tn), lambda i, j: (0, br * nj + j))

    def w_spec(br):
        return pl.BlockSpec((kb, tn), lambda i, j: (br, j))

    bias = b_gate.reshape(1, -1).astype(F32)
    return pl.pallas_call(
        _merge_kernel,
        grid=(n // tm, nj),
        in_specs=[pl.BlockSpec((tm, d), lambda i, j: (i, 0)), o_spec, o_spec, o_spec,
                  wg_spec(0), wg_spec(1), wg_spec(2), b_spec(0), b_spec(1), b_spec(2),
                  w_spec(0), w_spec(1), w_spec(2)],
        out_specs=pl.BlockSpec((tm, tn), lambda i, j: (i, j)),
        out_shape=jax.ShapeDtypeStruct((n, D_MODEL), BF16),
        compiler_params=pltpu.CompilerParams(dimension_semantics=("parallel", "parallel")),
        name="merge",
    )(u, o_swa, o_ret, o_mem, w_gate, w_gate, w_gate, bias, bias, bias, w_br, w_br, w_br)


def _ffn_up_kernel(u_ref, wug_ref, wuv_ref, wc_ref, b_ref, cb_ref, act_ref, co_ref,
                   ag_sc, av_sc, cg_sc, cv_sc, *, tm, tf, nf, nseg):
    i = pl.program_id(1)
    fb = pl.program_id(2)
    seg_t = tm // nseg
    stride = seg_t + 8
    if nseg == 1:
        @pl.when((i == 0) & (fb == 0))
        def _():
            for f in range(nf):
                cg_sc[f, 6:8, :] = cb_ref[0, f]
                cv_sc[f, 6:8, :] = cb_ref[0, nf + f]

    u = u_ref[0]
    for a_sc, wu_ref, blk, carry_sc in ((ag_sc, wug_ref, fb, cg_sc), (av_sc, wuv_ref, nf + fb, cv_sc)):
        a = jnp.dot(u, wu_ref[...], preferred_element_type=F32)
        for s in range(nseg):
            a_sc[s * stride + 8:(s + 1) * stride, :] = a[s * seg_t:(s + 1) * seg_t]
            co_ref[0, s, blk] = a[(s + 1) * seg_t - 8:(s + 1) * seg_t]
        if nseg == 1:
            a_sc[6:8, :] = carry_sc[fb, 6:8, :]
            carry_sc[fb] = a[tm - 8:tm]
        else:
            for s in range(nseg):
                a_sc[s * stride + 6:s * stride + 8, :] = cb_ref[s, blk]

    def conv(a_sc, blk, r0, c0):
        tap = lambda k: a_sc[r0 - k:r0 - k + CONV_ROWS, c0:c0 + LANE]
        wc = lambda k: wc_ref[blk, k:k + 1, c0:c0 + LANE]
        return b_ref[blk, :, c0:c0 + LANE] + (wc(0) * tap(2) + wc(1) * tap(1) + wc(2) * tap(0))

    for c0 in range(0, tf, LANE):
        for s in range(nseg):
            for r in range(0, seg_t, CONV_ROWS):
                cg = conv(ag_sc, fb, s * stride + 8 + r, c0)
                cv = conv(av_sc, nf + fb, s * stride + 8 + r, c0)
                act_ref[0, s * seg_t + r:s * seg_t + r + CONV_ROWS, c0:c0 + LANE] = (
                    cg * _sigmoid(cg) * cv).astype(act_ref.dtype)


def _ffn_up(u, conv_buf, w_up, w_conv, b_conv, *, tm, tf, nseg):
    bt, rows, d = u.shape
    nf = D_FF // tf
    nt = rows // tm
    nb = bt * nseg
    wc_blocks = w_conv.reshape(3, 2 * nf, tf).transpose(1, 0, 2)
    b_blocks = b_conv.reshape(2 * nf, 1, tf)
    cb_blocks = conv_buf.reshape(nb, 2, 2 * nf, tf).transpose(0, 2, 1, 3)
    whole = lambda *shape: pl.BlockSpec(shape, lambda b, i, f: (0,) * len(shape))
    return pl.pallas_call(
        functools.partial(_ffn_up_kernel, tm=tm, tf=tf, nf=nf, nseg=nseg),
        grid=(bt, nt, nf),
        in_specs=[pl.BlockSpec((1, tm, d), lambda b, i, f: (b, i, 0)),
                  pl.BlockSpec((d, tf), lambda b, i, f: (0, f)),
                  pl.BlockSpec((d, tf), lambda b, i, f: (0, nf + f)),
                  whole(2 * nf, 3, tf),
                  whole(2 * nf, 1, tf),
                  pl.BlockSpec((nseg, 2 * nf, 2, tf), lambda b, i, f: (b, 0, 0, 0))],
        out_specs=[pl.BlockSpec((1, tm, tf), lambda b, i, f: (b, i, f)),
                   pl.BlockSpec((1, nseg, 2 * nf, 8, tf), lambda b, i, f: (b * nt + i, 0, 0, 0, 0))],
        out_shape=[jax.ShapeDtypeStruct((bt, rows, D_FF), BF16),
                   jax.ShapeDtypeStruct((bt * nt, nseg, 2 * nf, 8, tf), F32)],
        scratch_shapes=[pltpu.VMEM((tm + 8 * nseg, tf), F32), pltpu.VMEM((tm + 8 * nseg, tf), F32),
                        pltpu.VMEM((nf, 8, tf), F32), pltpu.VMEM((nf, 8, tf), F32)],
        compiler_params=pltpu.CompilerParams(dimension_semantics=("arbitrary", "arbitrary", "arbitrary")),
        name="ffn_up",
    )(u, w_up, w_up, wc_blocks, b_blocks, cb_blocks)


def _ffn_down_kernel(act_ref, wd_ref, h_ref, g_ref, y_ref, *, nd, td):
    j = pl.program_id(1)
    for jj in range(nd):
        @pl.when(j == jj)
        def _():
            y_ref[:, jj * td:(jj + 1) * td] = h_ref[...] + jnp.dot(act_ref[...], wd_ref[...],
                                                                   preferred_element_type=F32)

    @pl.when(j == nd - 1)
    def _():
        y_ref[...] = _rms(y_ref[...], g_ref[...])


def _ffn_down(act, w_down, h, g_final, *, tm, td):
    n, f = act.shape
    d = h.shape[1]
    nd = d // td
    return pl.pallas_call(
        functools.partial(_ffn_down_kernel, nd=nd, td=td),
        grid=(n // tm, nd),
        in_specs=[pl.BlockSpec((tm, f), lambda i, j: (i, 0)),
                  pl.BlockSpec((f, td), lambda i, j: (0, j)),
                  pl.BlockSpec((tm, td), lambda i, j: (i, j)),
                  pl.BlockSpec((1, d), lambda i, j: (0, 0))],
        out_specs=pl.BlockSpec((tm, d), lambda i, j: (i, 0)),
        out_shape=jax.ShapeDtypeStruct((n, d), F32),
        compiler_params=pltpu.CompilerParams(dimension_semantics=("parallel", "arbitrary")),
        name="ffn_down",
    )(act, w_down, h, g_final.reshape(1, d))


def _out_proj_kernel(x_ref, w_ref, res_ref, g_ref, h_ref, u_ref):
    h = res_ref[...] + jnp.dot(x_ref[...], w_ref[...], preferred_element_type=F32)
    h_ref[...] = h
    u_ref[...] = _rms(h, g_ref[...]).astype(u_ref.dtype)


def _out_proj(x, w, residual, g, *, tm):
    n, k = x.shape
    d = w.shape[1]
    row = lambda width: pl.BlockSpec((tm, width), lambda i: (i, 0))
    return pl.pallas_call(
        _out_proj_kernel,
        grid=(n // tm,),
        in_specs=[row(k), pl.BlockSpec((k, d), lambda i: (0, 0)), row(d), pl.BlockSpec((1, d), lambda i: (0, 0))],
        out_specs=[row(d), row(d)],
        out_shape=[jax.ShapeDtypeStruct((n, d), F32), jax.ShapeDtypeStruct((n, d), BF16)],
        compiler_params=pltpu.CompilerParams(dimension_semantics=("parallel",)),
        name="proj_out",
    )(x, w, residual, g.reshape(1, d))


def _rope_tables(pos0, s):
    half = RET_DIM // 2
    inv_freq = 1.0 / (RET_ROPE_BASE ** np.linspace(0.0, 1.0, half))
    ang = (pos0 + np.arange(s))[:, None] * inv_freq[None, :]
    cos, sin = np.cos(ang), np.sin(ang)
    return (jnp.asarray(np.concatenate([cos, cos], axis=1), F32),
            jnp.asarray(np.concatenate([-sin, sin], axis=1), F32))


def _prep_weights(g_mix, w_in, b_gate, sink, w_br, w_o, g_ffn, w_up, w_conv, b_conv, w_down):
    o = 0
    cuts = {}
    for name, width in (("qa", SWA_Q), ("ka", SWA_KV), ("va", SWA_KV), ("qr", RET_W), ("kr", RET_W),
                        ("vr", RET_W), ("gr", RET_W), ("qm", MEM_W), ("gl", N_BRANCH * D_MODEL)):
        cuts[name] = (o, o + width)
        o += width
    cols = lambda *names: jnp.concatenate([w_in[:, cuts[n][0]:cuts[n][1]] for n in names], axis=1).astype(BF16)
    ones = lambda n, v: jnp.full((n,), v, F32)
    return dict(
        g_mix=g_mix, b_gate=b_gate, sink=sink, g_ffn=g_ffn, w_conv=w_conv, b_conv=b_conv,
        w_plain=cols("qa", "vr", "qm"),
        s_plain=jnp.concatenate([ones(SWA_Q, SWA_HEAD_DIM ** -0.5), ones(RET_W, 1.0),
                                 ones(MEM_W, MEM_HEAD_DIM ** -0.5)]),
        w_kv=cols("ka", "va"),
        w_rot=cols("qr", "kr"),
        s_rot=jnp.concatenate([ones(RET_W, 1.0), ones(RET_W, RET_DIM ** -0.5)]),
        w_g=cols("gr"),
        w_in16=w_in.astype(BF16), gate_col0=cuts["gl"][0],
        w_br=w_br.astype(BF16), w_o=w_o.astype(BF16), w_up=w_up.astype(BF16), w_down=w_down.astype(BF16),
    )


def _run_group(x, pos0, mem_k, mem_v, swa_cache, ret_state, conv_buf, wts, g_final, log_g):
    b, s, d = x.shape
    n = b * s
    tm = min(1024, n)
    x2 = x.reshape(n, d)
    plain, u = _norm_mm(x2, wts["g_mix"], wts["w_plain"], wts["s_plain"], tm=tm, tn=1024, name="proj_plain")
    kv = _mm(u, wts["w_kv"], epilogue="plain", out_dtype=F32, tm=tm, tn=2 * SWA_KV, name="proj_kv")
    cc, ss = _rope_tables(pos0, s)
    if s < tm:
        cc, ss = jnp.tile(cc, (tm // s, 1)), jnp.tile(ss, (tm // s, 1))
    rot = _mm(u, wts["w_rot"], epilogue="rotary", out_dtype=BF16, tm=tm, tn=1024,
              colvec=wts["s_rot"], tables=(cc, ss), name="proj_rot")
    sgate = _mm(u, wts["w_g"], epilogue="silu", out_dtype=BF16, tm=tm, tn=1024, name="proj_silu")

    plain3 = plain.reshape(b, s, -1)
    kv3 = kv.reshape(b, s, 2 * SWA_KV)
    if swa_cache is None:
        o_swa = _swa(plain3, 0, kv3, 0, kv3, 1, wts["sink"], sq=s, qt=min(256, s), kv_off=0)
        new_k = kv3[:, s - WINDOW:, :SWA_KV]
        new_v = kv3[:, s - WINDOW:, SWA_KV:]
    else:
        ck = swa_cache[0].reshape(b, -1, SWA_KV)
        cv = swa_cache[1].reshape(b, -1, SWA_KV)
        n_keep = ck.shape[1]
        k_all = jnp.concatenate([ck, kv3[:, :, :SWA_KV]], axis=1)
        v_all = jnp.concatenate([cv, kv3[:, :, SWA_KV:]], axis=1)
        o_swa = _swa(plain3, 0, k_all, 0, v_all, 0, wts["sink"], sq=s, qt=s, kv_off=n_keep)
        new_k = k_all[:, -n_keep:]
        new_v = v_all[:, -n_keep:]
    state0 = jnp.zeros((b, RET_HEADS, RET_DIM, RET_DIM), F32) if ret_state is None else ret_state
    o_ret, s_new = _retention(rot.reshape(b, s, -1), 0, 1, plain3, 1, sgate.reshape(b, s, -1), state0, log_g,
                              s=s, c=min(256, s))
    o_mem = _mem_attend(plain3, 2, mem_k.reshape(b, N_MEM, MEM_W), mem_v.reshape(b, N_MEM, MEM_W),
                        s=s, tq=min(512, s))
    merged = _merge(u, o_swa.reshape(n, -1), o_ret.reshape(n, -1), o_mem.reshape(n, -1), wts["w_in16"],
                    wts["gate_col0"], wts["b_gate"], wts["w_br"], tm=tm, tn=512)
    h1, u2 = _out_proj(merged, wts["w_o"], x2, wts["g_ffn"], tm=min(512, n))
    if s >= 1024:
        act, co = _ffn_up(u2.reshape(b, s, d), conv_buf, wts["w_up"], wts["w_conv"], wts["b_conv"],
                          tm=1024, tf=512, nseg=1)
    else:
        act, co = _ffn_up(u2.reshape(1, n, d), conv_buf, wts["w_up"], wts["w_conv"], wts["b_conv"],
                          tm=n, tf=512, nseg=b)
    y = _ffn_down(act.reshape(n, D_FF), wts["w_down"], h1, g_final, tm=min(1024, n), td=512)
    nblk, tfb = co.shape[2], co.shape[4]
    last = co.reshape(b, -1, nblk, 8, tfb)[:, -1, :, 6:8, :]
    new_buf = last.transpose(0, 2, 1, 3).reshape(b, 2, nblk * tfb)
    return (y.reshape(b, s, d), new_k.reshape(b, -1, SWA_KV_HEADS, SWA_HEAD_DIM),
            new_v.reshape(b, -1, SWA_KV_HEADS, SWA_HEAD_DIM), s_new, new_buf)


def _memory_kv(mem, g_mem, w_mem_kv):
    b, m, d = mem.shape
    u = _rmsnorm(mem.reshape(b * m, d), g_mem, min(512, b * m))
    kv = _mm(u, w_mem_kv.astype(BF16), epilogue="plain", out_dtype=F32, tm=min(1024, b * m), tn=1024, name="mem_kv")
    return (kv[:, :MEM_W].reshape(b, m, MEM_HEADS, MEM_HEAD_DIM), kv[:, MEM_W:].reshape(b, m, MEM_HEADS, MEM_HEAD_DIM))


def kernel(x_prompt, x_sample, mem_prompt, cache_swa_k, cache_swa_v, state_ret, state_ffn_conv, cache_mem_k, cache_mem_v, g_mix, w_in, b_gate, sink, w_br, w_o, g_mem, w_mem_kv, g_ffn, w_up, w_conv, b_conv, w_down, g_final):
    bp = x_prompt.shape[0]
    depth = w_in.shape[0]
    log_g = jnp.log1p(-jnp.exp2(-5.0 - jnp.arange(RET_HEADS, dtype=F32)))
    hp, hs = x_prompt, x_sample
    outs_p = [[] for _ in range(6)]
    outs_s = [[] for _ in range(4)]
    for l in range(depth):
        wts = _prep_weights(g_mix[l], w_in[l], b_gate[l], sink[l], w_br[l], w_o[l],
                            g_ffn[l], w_up[l], w_conv[l], b_conv[l], w_down[l])
        mk, mv = _memory_kv(mem_prompt, g_mem[l], w_mem_kv[l])
        zero_buf = jnp.zeros((bp, 2, 2 * D_FF), F32)
        assert depth == 1
        hp, kp, vp, sp, cp = _run_group(hp, 0, mk, mv, None, None, zero_buf, wts, g_final, log_g)
        hs, ksn, vsn, ssn, csn = _run_group(hs, PAST_LEN, cache_mem_k[l], cache_mem_v[l],
                                            (cache_swa_k[l], cache_swa_v[l]), state_ret[l],
                                            state_ffn_conv[l], wts, g_final, log_g)
        for lst, val in zip(outs_p, (kp, vp, sp, cp, mk, mv)):
            lst.append(val)
        for lst, val in zip(outs_s, (ksn, vsn, ssn, csn)):
            lst.append(val)
    return (hp, hs, *[jnp.stack(v) for v in outs_p], *[jnp.stack(v) for v in outs_s])
```

```python
import functools

import jax
import jax.numpy as jnp
import numpy as np
from jax import lax
from jax.experimental import pallas as pl
from jax.experimental.pallas import tpu as pltpu

F32 = jnp.float32
BF16 = jnp.bfloat16

D_MODEL = 2048
CHUNK = 64
WINDOW = 128
SWA_HEADS = 16
SWA_KV_HEADS = 4
SWA_GROUP = SWA_HEADS // SWA_KV_HEADS
SWA_HEAD_DIM = 64
RET_HEADS = 8
RET_DIM = 128
RET_ROPE_BASE = 10000.0
N_MEM = 256
MEM_HEADS = 4
MEM_HEAD_DIM = 256
D_FF = 5632
N_BRANCH = 3
EPS = 1e-6
NEG = -1e30
PAST_LEN = 1024

SWA_Q = SWA_HEADS * SWA_HEAD_DIM
SWA_KV = SWA_KV_HEADS * SWA_HEAD_DIM
RET_W = RET_HEADS * RET_DIM
MEM_W = MEM_HEADS * MEM_HEAD_DIM
LANE = 128
CONV_ROWS = 64
ROW_TILE = 1024
COL_TILE = 1024
MERGE_COLS = 512
OUT_PROJ_ROWS = 512
FF_TILE = 512
SWA_ROWS = 256
RET_ROWS = 256
MEM_ROWS = 1024


def _sigmoid(x):
    return 1.0 / (1.0 + jnp.exp(-x))


def _rms(x, g):
    return x * lax.rsqrt(jnp.mean(x * x, axis=-1, keepdims=True) + EPS) * g


def _rmsnorm_kernel(x_ref, g_ref, o_ref):
    o_ref[...] = _rms(x_ref[...], g_ref[...]).astype(o_ref.dtype)


def _rmsnorm(x, g, tm):
    n, d = x.shape
    return pl.pallas_call(
        _rmsnorm_kernel,
        grid=(n // tm,),
        in_specs=[pl.BlockSpec((tm, d), lambda i: (i, 0)),
                  pl.BlockSpec((1, d), lambda i: (0, 0))],
        out_specs=pl.BlockSpec((tm, d), lambda i: (i, 0)),
        out_shape=jax.ShapeDtypeStruct((n, d), BF16),
        name="rmsnorm",
    )(x, g.reshape(1, d))


def _mm_kernel(x_ref, w_ref, *rest, epilogue, tn):
    o_ref = rest[-1]
    acc = jnp.dot(x_ref[...], w_ref[...], preferred_element_type=F32)
    if epilogue == "plain":
        out = acc
    elif epilogue == "silu":
        out = acc * _sigmoid(acc)
    elif epilogue == "rotary":
        cc, ss = rest[1][...], rest[2][...]
        pieces = []
        for j in range(tn // LANE):
            xh = acc[:, j * LANE:(j + 1) * LANE]
            pieces.append(xh * cc + pltpu.roll(xh, LANE // 2, 1) * ss)
        out = jnp.concatenate(pieces, axis=1) * rest[0][...]
    else:
        raise ValueError(epilogue)
    o_ref[...] = out.astype(o_ref.dtype)


def _mm(x, w, *, epilogue, out_dtype, tm, tn, colvec=None, tables=None, name):
    n, k = x.shape
    ncols = w.shape[1]
    in_specs = [pl.BlockSpec((tm, k), lambda i, j: (i, 0)),
                pl.BlockSpec((k, tn), lambda i, j: (0, j))]
    args = [x, w]
    if colvec is not None:
        in_specs.append(pl.BlockSpec((1, tn), lambda i, j: (0, j)))
        args.append(colvec.reshape(1, ncols).astype(F32))
    if tables is not None:
        nblk = tables[0].shape[0] // tm
        for t in tables:
            in_specs.append(pl.BlockSpec((tm, LANE), lambda i, j: (i % nblk, 0)))
            args.append(t)
    return pl.pallas_call(
        functools.partial(_mm_kernel, epilogue=epilogue, tn=tn),
        grid=(n // tm, ncols // tn),
        in_specs=in_specs,
        out_specs=pl.BlockSpec((tm, tn), lambda i, j: (i, j)),
        out_shape=jax.ShapeDtypeStruct((n, ncols), out_dtype),
        compiler_params=pltpu.CompilerParams(dimension_semantics=("parallel", "parallel")),
        name=name,
    )(*args)


def _norm_mm_kernel(x_ref, g_ref, w_ref, cv_ref, o_ref, u_ref):
    @pl.when(pl.program_id(1) == 0)
    def _():
        u_ref[...] = _rms(x_ref[...], g_ref[...]).astype(u_ref.dtype)

    acc = jnp.dot(u_ref[...], w_ref[...], preferred_element_type=F32)
    o_ref[...] = (acc * cv_ref[...]).astype(o_ref.dtype)


def _norm_mm(x, g, w, colvec, *, tm, tn, name):
    n, k = x.shape
    ncols = w.shape[1]
    return pl.pallas_call(
        _norm_mm_kernel,
        grid=(n // tm, ncols // tn),
        in_specs=[pl.BlockSpec((tm, k), lambda i, j: (i, 0)),
                  pl.BlockSpec((1, k), lambda i, j: (0, 0)),
                  pl.BlockSpec((k, tn), lambda i, j: (0, j)),
                  pl.BlockSpec((1, tn), lambda i, j: (0, j))],
        out_specs=[pl.BlockSpec((tm, tn), lambda i, j: (i, j)),
                   pl.BlockSpec((tm, k), lambda i, j: (i, 0))],
        out_shape=[jax.ShapeDtypeStruct((n, ncols), BF16), jax.ShapeDtypeStruct((n, k), BF16)],
        compiler_params=pltpu.CompilerParams(dimension_semantics=("parallel", "arbitrary")),
        name=name,
    )(x, g.reshape(1, k), w, colvec.reshape(1, ncols).astype(F32))


def _swa_kernel(sink_ref, q_ref, kp_ref, kc_ref, vp_ref, vc_ref, o_ref, *scratch, qt, kv_off):
    s_scs, p_scs = scratch[:SWA_KV_HEADS], scratch[SWA_KV_HEADS:]
    i = pl.program_id(1)
    nq = qt // CHUNK
    w = (nq + 2) * CHUNK
    wp = -(-w // LANE) * LANE
    first_ok = jnp.where(i * qt + kv_off >= WINDOW, 0, 2)

    def window(p_ref, c_ref):
        parts = [p_ref[0], c_ref[0]]
        if wp > w:
            parts.append(jnp.zeros((wp - w, SWA_KV), F32))
        return jnp.concatenate(parts, axis=0)

    kwin = window(kp_ref, kc_ref)
    vwin = window(vp_ref, vc_ref)
    lane = lax.broadcasted_iota(jnp.int32, (wp, LANE), 1)
    lane_g = lax.broadcasted_iota(jnp.int32, (wp, SWA_KV), 1) // SWA_HEAD_DIM
    out_g = lax.broadcasted_iota(jnp.int32, (CHUNK, SWA_KV), 1) // SWA_HEAD_DIM

    def block_diag(win, h):
        col, half = divmod(h, 2)
        x = win[:, col * LANE:(col + 1) * LANE]
        xr = pltpu.roll(x, SWA_HEAD_DIM, 1)
        lo = lane < SWA_HEAD_DIM
        r = jnp.where(lo, x, xr) if half == 0 else jnp.where(lo, xr, x)
        r2 = jnp.concatenate([r, r], axis=1).astype(BF16)
        zero = jnp.zeros_like(r2)
        return jnp.concatenate([jnp.where(lane_g == g, r2, zero) for g in range(SWA_GROUP)], axis=0)

    for h in range(SWA_KV_HEADS):
        bk = block_diag(kwin, h)
        bv = block_diag(vwin, h)
        qh = q_ref[0, :, h * SWA_KV:(h + 1) * SWA_KV]
        s_sc, p_sc = s_scs[h], p_scs[h]
        s_sc[...] = lax.dot_general(qh, bk, (((1,), (1,)), ((), ())), preferred_element_type=F32)
        invs = []
        for jq in range(nq):
            rows = slice(jq * CHUNK, (jq + 1) * CHUNK)
            c_lo, c_hi = jq * CHUNK // LANE, ((jq + 3) * CHUNK - 1) // LANE
            width = (c_hi - c_lo + 1) * LANE
            kc = c_lo * (LANE // CHUNK) + lax.broadcasted_iota(jnp.int32, (CHUNK, width), 1) // CHUNK
            visible = (kc >= jq) & (kc <= jq + 2) & (kc >= first_ok)
            inv = jnp.zeros((CHUNK, SWA_KV), F32)
            for g in range(SWA_GROUP):
                l0 = g * wp + c_lo * LANE
                sg = jnp.where(visible, s_sc[rows, l0:l0 + width], NEG)
                sk = sink_ref[h * SWA_GROUP + g]
                m = jnp.maximum(jnp.max(sg, axis=1, keepdims=True), sk)
                p = jnp.exp(sg - m)
                den = jnp.sum(p, axis=1, keepdims=True) + jnp.exp(sk - m)
                p_sc[rows, l0:l0 + width] = p.astype(BF16)
                for c in range(wp // LANE):
                    if not c_lo <= c <= c_hi:
                        p_sc[rows, g * wp + c * LANE:g * wp + (c + 1) * LANE] = jnp.zeros((CHUNK, LANE), BF16)
                inv = jnp.where(out_g == g, 1.0 / den, inv)
            invs.append(inv)
        o = jnp.dot(p_sc[...], bv, preferred_element_type=F32)
        o_ref[0, :, h * SWA_KV:(h + 1) * SWA_KV] = (o * jnp.concatenate(invs, axis=0)).astype(o_ref.dtype)


def _swa(q, q_col, k, k_col, v, v_col, sink, *, sq, qt, kv_off):
    b = q.shape[0]
    cur_off = kv_off // qt
    prev_off = kv_off // WINDOW - 1
    per = qt // WINDOW if qt >= WINDOW else 0
    wp = -(-(qt + WINDOW) // LANE) * LANE

    def prev_map(kcol):
        return lambda bi, i, s: (bi, jnp.maximum(i * per + prev_off, 0), kcol)

    def cur_map(kcol):
        return lambda bi, i, s: (bi, i + cur_off, kcol)

    grid_spec = pltpu.PrefetchScalarGridSpec(
        num_scalar_prefetch=1,
        grid=(b, sq // qt),
        in_specs=[pl.BlockSpec((1, qt, SWA_Q), lambda bi, i, s: (bi, i, q_col)),
                  pl.BlockSpec((1, WINDOW, SWA_KV), prev_map(k_col)),
                  pl.BlockSpec((1, qt, SWA_KV), cur_map(k_col)),
                  pl.BlockSpec((1, WINDOW, SWA_KV), prev_map(v_col)),
                  pl.BlockSpec((1, qt, SWA_KV), cur_map(v_col))],
        out_specs=pl.BlockSpec((1, qt, SWA_Q), lambda bi, i, s: (bi, i, 0)),
        scratch_shapes=([pltpu.VMEM((qt, SWA_GROUP * wp), F32)] * SWA_KV_HEADS
                        + [pltpu.VMEM((qt, SWA_GROUP * wp), BF16)] * SWA_KV_HEADS),
    )
    return pl.pallas_call(
        functools.partial(_swa_kernel, qt=qt, kv_off=kv_off),
        grid_spec=grid_spec,
        out_shape=jax.ShapeDtypeStruct((b, sq, SWA_Q), BF16),
        compiler_params=pltpu.CompilerParams(dimension_semantics=("parallel", "parallel")),
        name="swa",
    )(sink.astype(F32), q, k, k, v, v)


def _ret_kernel(lg_ref, q_ref, k_ref, v_ref, g_ref, s0_ref, o_ref, st_ref, dec_ref, xi_ref, zeta_ref, *, c):
    bi = pl.program_id(0)
    ci = pl.program_id(1)

    @pl.when((bi == 0) & (ci == 0))
    def _():
        n = lax.broadcasted_iota(jnp.int32, (c, c), 0)
        m = lax.broadcasted_iota(jnp.int32, (c, c), 1)
        diff = (n - m).astype(F32)
        nrow = lax.broadcasted_iota(jnp.int32, (c, RET_DIM), 0).astype(F32)
        for h in range(RET_HEADS):
            dec_ref[h] = jnp.where(diff >= 0, jnp.exp(lg_ref[h] * jnp.maximum(diff, 0.0)), 0.0)
            xi_ref[h] = jnp.exp(lg_ref[h] * (nrow + 1.0))
            zeta_ref[h] = jnp.exp(lg_ref[h] * (float(c) - 1.0 - nrow))

    @pl.when(ci == 0)
    def _():
        st_ref[...] = s0_ref[...]

    for h in range(RET_HEADS):
        lg = lg_ref[h]
        sl = slice(h * RET_DIM, (h + 1) * RET_DIM)
        qh = q_ref[0, :, sl]
        kh = k_ref[0, :, sl]
        vh = v_ref[0, :, sl]
        state = st_ref[0, h]
        inner = lax.dot_general(qh, kh, (((1,), (1,)), ((), ())), preferred_element_type=F32) * dec_ref[h]
        o = jnp.dot(inner.astype(BF16), vh, preferred_element_type=F32)
        cross = jnp.dot(qh, state.astype(BF16), preferred_element_type=F32)
        o = o + cross * xi_ref[h]
        kz = (kh.astype(F32) * zeta_ref[h]).astype(BF16)
        upd = lax.dot_general(kz, vh, (((0,), (0,)), ((), ())), preferred_element_type=F32)
        decay_c = jnp.exp(lg * jnp.full((1, RET_DIM), float(c), F32))
        st_ref[0, h] = decay_c * state + upd
        on = o * lax.rsqrt(jnp.mean(o * o, axis=-1, keepdims=True) + EPS)
        o_ref[0, :, sl] = (g_ref[0, :, sl].astype(F32) * on).astype(o_ref.dtype)


def _retention(qk, q_col, k_col, v, v_col, gate, state0, log_g, *, s, c):
    b = qk.shape[0]

    def seq(col):
        return pl.BlockSpec((1, c, RET_W), lambda bi, ci, lg: (bi, ci, col))

    st_spec = pl.BlockSpec((1, RET_HEADS, RET_DIM, RET_DIM), lambda bi, ci, lg: (bi, 0, 0, 0))
    grid_spec = pltpu.PrefetchScalarGridSpec(
        num_scalar_prefetch=1,
        grid=(b, s // c),
        in_specs=[seq(q_col), seq(k_col), seq(v_col), seq(0), st_spec],
        out_specs=[seq(0), st_spec],
        scratch_shapes=[pltpu.VMEM((RET_HEADS, c, c), F32), pltpu.VMEM((RET_HEADS, c, RET_DIM), F32),
                        pltpu.VMEM((RET_HEADS, c, RET_DIM), F32)],
    )
    return pl.pallas_call(
        functools.partial(_ret_kernel, c=c),
        grid_spec=grid_spec,
        out_shape=[jax.ShapeDtypeStruct((b, s, RET_W), BF16),
                   jax.ShapeDtypeStruct((b, RET_HEADS, RET_DIM, RET_DIM), F32)],
        compiler_params=pltpu.CompilerParams(dimension_semantics=("arbitrary", "arbitrary")),
        name="retention",
    )(log_g, qk, qk, v, gate, state0)


def _mem_kernel(q_ref, k_ref, v_ref, o_ref):
    for h in range(MEM_HEADS):
        sl = slice(h * MEM_HEAD_DIM, (h + 1) * MEM_HEAD_DIM)
        qh = q_ref[0, :, sl]
        kh = k_ref[0, :, sl].astype(BF16)
        vh = v_ref[0, :, sl].astype(BF16)
        s = lax.dot_general(qh, kh, (((1,), (1,)), ((), ())), preferred_element_type=F32)
        m = jnp.max(s, axis=1, keepdims=True)
        p = jnp.exp(s - m)
        den = jnp.sum(p, axis=1, keepdims=True)
        o = jnp.dot(p.astype(BF16), vh, preferred_element_type=F32)
        o_ref[0, :, sl] = (o * (1.0 / den)).astype(o_ref.dtype)


def _mem_attend(q, q_col, mk, mv, *, s, tq):
    b = q.shape[0]
    kv_spec = pl.BlockSpec((1, N_MEM, MEM_W), lambda bi, i: (bi, 0, 0))
    return pl.pallas_call(
        _mem_kernel,
        grid=(b, s // tq),
        in_specs=[pl.BlockSpec((1, tq, MEM_W), lambda bi, i: (bi, i, q_col)), kv_spec, kv_spec],
        out_specs=pl.BlockSpec((1, tq, MEM_W), lambda bi, i: (bi, i, 0)),
        out_shape=jax.ShapeDtypeStruct((b, s, MEM_W), BF16),
        compiler_params=pltpu.CompilerParams(dimension_semantics=("parallel", "parallel")),
        name="mem_attend",
    )(q, mk, mv)


def _merge_kernel(u_ref, oa_ref, ob_ref, oc_ref, wga_ref, wgb_ref, wgc_ref, ba_ref, bb_ref, bc_ref,
                  wa_ref, wb_ref, wc_ref, o_ref):
    u = u_ref[...]
    acc = None
    for o_r, wg_ref, b_ref, w_ref in ((oa_ref, wga_ref, ba_ref, wa_ref), (ob_ref, wgb_ref, bb_ref, wb_ref),
                                      (oc_ref, wgc_ref, bc_ref, wc_ref)):
        gate = _sigmoid(jnp.dot(u, wg_ref[...], preferred_element_type=F32) + b_ref[...])
        term = gate * jnp.dot(o_r[...], w_ref[...], preferred_element_type=F32)
        acc = term if acc is None else acc + term
    o_ref[...] = acc.astype(o_ref.dtype)


def _merge(u, o_swa, o_ret, o_mem, w_gate, gate_col0, b_gate, w_br, *, tm, tn):
    n, d = u.shape
    kb = o_swa.shape[1]
    nj = D_MODEL // tn
    g0 = gate_col0 // tn
    o_spec = pl.BlockSpec((tm, kb), lambda i, j: (i, 0))

    def wg_spec(br):
        return pl.BlockSpec((d, tn), lambda i, j: (0, g0 + br * nj + j))

    def b_spec(br):
        return pl.BlockSpec((1, tn), lambda i, j: (0, br * nj + j))

    def w_spec(br):
        return pl.BlockSpec((kb, tn), lambda i, j: (br, j))

    bias = b_gate.reshape(1, -1).astype(F32)
    return pl.pallas_call(
        _merge_kernel,
        grid=(n // tm, nj),
        in_specs=[pl.BlockSpec((tm, d), lambda i, j: (i, 0)), o_spec, o_spec, o_spec,
                  wg_spec(0), wg_spec(1), wg_spec(2), b_spec(0), b_spec(1), b_spec(2),
                  w_spec(0), w_spec(1), w_spec(2)],
        out_specs=pl.BlockSpec((tm, tn), lambda i, j: (i, j)),
        out_shape=jax.ShapeDtypeStruct((n, D_MODEL), BF16),
        compiler_params=pltpu.CompilerParams(dimension_semantics=("parallel", "parallel")),
        name="merge",
    )(u, o_swa, o_ret, o_mem, w_gate, w_gate, w_gate, bias, bias, bias, w_br, w_br, w_br)


def _ffn_up_kernel(u_ref, wug_ref, wuv_ref, wc_ref, b_ref, cb_ref, act_ref, co_ref,
                   ag_sc, av_sc, cg_sc, cv_sc, *, tm, tf, nf, nseg):
    i = pl.program_id(1)
    fb = pl.program_id(2)
    seg_t = tm // nseg
    stride = seg_t + 8
    if nseg == 1:
        @pl.when((i == 0) & (fb == 0))
        def _():
            for f in range(nf):
                cg_sc[f, 6:8, :] = cb_ref[0, f]
                cv_sc[f, 6:8, :] = cb_ref[0, nf + f]

    u = u_ref[0]
    for a_sc, wu_ref, blk, carry_sc in ((ag_sc, wug_ref, fb, cg_sc), (av_sc, wuv_ref, nf + fb, cv_sc)):
        a = jnp.dot(u, wu_ref[...], preferred_element_type=F32)
        for s in range(nseg):
            a_sc[s * stride + 8:(s + 1) * stride, :] = a[s * seg_t:(s + 1) * seg_t]
            co_ref[0, s, blk] = a[(s + 1) * seg_t - 8:(s + 1) * seg_t]
        if nseg == 1:
            a_sc[6:8, :] = carry_sc[fb, 6:8, :]
            carry_sc[fb] = a[tm - 8:tm]
        else:
            for s in range(nseg):
                a_sc[s * stride + 6:s * stride + 8, :] = cb_ref[s, blk]

    def conv(a_sc, blk, r0, c0):
        tap = lambda k: a_sc[r0 - k:r0 - k + CONV_ROWS, c0:c0 + LANE]
        wc = lambda k: wc_ref[blk, k:k + 1, c0:c0 + LANE]
        return b_ref[blk, :, c0:c0 + LANE] + (wc(0) * tap(2) + wc(1) * tap(1) + wc(2) * tap(0))

    for c0 in range(0, tf, LANE):
        for s in range(nseg):
            for r in range(0, seg_t, CONV_ROWS):
                cg = conv(ag_sc, fb, s * stride + 8 + r, c0)
                cv = conv(av_sc, nf + fb, s * stride + 8 + r, c0)
                act_ref[0, s * seg_t + r:s * seg_t + r + CONV_ROWS, c0:c0 + LANE] = (
                    cg * _sigmoid(cg) * cv).astype(act_ref.dtype)


def _ffn_up(u, conv_buf, w_up, w_conv, b_conv, *, tm, tf, nseg):
    bt, rows, d = u.shape
    nf = D_FF // tf
    nt = rows // tm
    nb = bt * nseg
    wc_blocks = w_conv.reshape(3, 2 * nf, tf).transpose(1, 0, 2)
    b_blocks = b_conv.reshape(2 * nf, 1, tf)
    cb_blocks = conv_buf.reshape(nb, 2, 2 * nf, tf).transpose(0, 2, 1, 3)
    whole = lambda *shape: pl.BlockSpec(shape, lambda b, i, f: (0,) * len(shape))
    return pl.pallas_call(
        functools.partial(_ffn_up_kernel, tm=tm, tf=tf, nf=nf, nseg=nseg),
        grid=(bt, nt, nf),
        in_specs=[pl.BlockSpec((1, tm, d), lambda b, i, f: (b, i, 0)),
                  pl.BlockSpec((d, tf), lambda b, i, f: (0, f)),
                  pl.BlockSpec((d, tf), lambda b, i, f: (0, nf + f)),
                  whole(2 * nf, 3, tf),
                  whole(2 * nf, 1, tf),
                  pl.BlockSpec((nseg, 2 * nf, 2, tf), lambda b, i, f: (b, 0, 0, 0))],
        out_specs=[pl.BlockSpec((1, tm, tf), lambda b, i, f: (b, i, f)),
                   pl.BlockSpec((1, nseg, 2 * nf, 8, tf), lambda b, i, f: (b * nt + i, 0, 0, 0, 0))],
        out_shape=[jax.ShapeDtypeStruct((bt, rows, D_FF), BF16),
                   jax.ShapeDtypeStruct((bt * nt, nseg, 2 * nf, 8, tf), F32)],
        scratch_shapes=[pltpu.VMEM((tm + 8 * nseg, tf), F32), pltpu.VMEM((tm + 8 * nseg, tf), F32),
                        pltpu.VMEM((nf, 8, tf), F32), pltpu.VMEM((nf, 8, tf), F32)],
        compiler_params=pltpu.CompilerParams(dimension_semantics=("arbitrary", "arbitrary", "arbitrary")),
        name="ffn_up",
    )(u, w_up, w_up, wc_blocks, b_blocks, cb_blocks)


def _ffn_down_kernel(act_ref, wd_ref, h_ref, g_ref, y_ref, *, nd, td):
    j = pl.program_id(1)
    for jj in range(nd):
        @pl.when(j == jj)
        def _():
            y_ref[:, jj * td:(jj + 1) * td] = h_ref[...] + jnp.dot(act_ref[...], wd_ref[...],
                                                                   preferred_element_type=F32)

    @pl.when(j == nd - 1)
    def _():
        y_ref[...] = _rms(y_ref[...], g_ref[...])


def _ffn_down(act, w_down, h, g_final, *, tm, td):
    n, f = act.shape
    d = h.shape[1]
    nd = d // td
    return pl.pallas_call(
        functools.partial(_ffn_down_kernel, nd=nd, td=td),
        grid=(n // tm, nd),
        in_specs=[pl.BlockSpec((tm, f), lambda i, j: (i, 0)),
                  pl.BlockSpec((f, td), lambda i, j: (0, j)),
                  pl.BlockSpec((tm, td), lambda i, j: (i, j)),
                  pl.BlockSpec((1, d), lambda i, j: (0, 0))],
        out_specs=pl.BlockSpec((tm, d), lambda i, j: (i, 0)),
        out_shape=jax.ShapeDtypeStruct((n, d), F32),
        compiler_params=pltpu.CompilerParams(dimension_semantics=("parallel", "arbitrary")),
        name="ffn_down",
    )(act, w_down, h, g_final.reshape(1, d))


def _out_proj_kernel(x_ref, w_ref, res_ref, g_ref, h_ref, u_ref):
    h = res_ref[...] + jnp.dot(x_ref[...], w_ref[...], preferred_element_type=F32)
    h_ref[...] = h
    u_ref[...] = _rms(h, g_ref[...]).astype(u_ref.dtype)


def _out_proj(x, w, residual, g, *, tm):
    n, k = x.shape
    d = w.shape[1]
    row = lambda width: pl.BlockSpec((tm, width), lambda i: (i, 0))
    return pl.pallas_call(
        _out_proj_kernel,
        grid=(n // tm,),
        in_specs=[row(k), pl.BlockSpec((k, d), lambda i: (0, 0)), row(d), pl.BlockSpec((1, d), lambda i: (0, 0))],
        out_specs=[row(d), row(d)],
        out_shape=[jax.ShapeDtypeStruct((n, d), F32), jax.ShapeDtypeStruct((n, d), BF16)],
        compiler_params=pltpu.CompilerParams(dimension_semantics=("parallel",)),
        name="proj_out",
    )(x, w, residual, g.reshape(1, d))


def _rope_tables(pos0, s):
    half = RET_DIM // 2
    inv_freq = 1.0 / (RET_ROPE_BASE ** np.linspace(0.0, 1.0, half))
    ang = (pos0 + np.arange(s))[:, None] * inv_freq[None, :]
    cos, sin = np.cos(ang), np.sin(ang)
    return (jnp.asarray(np.concatenate([cos, cos], axis=1), F32),
            jnp.asarray(np.concatenate([-sin, sin], axis=1), F32))


def _prep_weights(g_mix, w_in, b_gate, sink, w_br, w_o, g_ffn, w_up, w_conv, b_conv, w_down):
    o = 0
    cuts = {}
    for name, width in (("qa", SWA_Q), ("ka", SWA_KV), ("va", SWA_KV), ("qr", RET_W), ("kr", RET_W),
                        ("vr", RET_W), ("gr", RET_W), ("qm", MEM_W), ("gl", N_BRANCH * D_MODEL)):
        cuts[name] = (o, o + width)
        o += width
    cols = lambda *names: jnp.concatenate([w_in[:, cuts[n][0]:cuts[n][1]] for n in names], axis=1).astype(BF16)
    ones = lambda n, v: jnp.full((n,), v, F32)
    return dict(
        g_mix=g_mix, b_gate=b_gate, sink=sink, g_ffn=g_ffn, w_conv=w_conv, b_conv=b_conv,
        w_plain=cols("qa", "vr", "qm"),
        s_plain=jnp.concatenate([ones(SWA_Q, SWA_HEAD_DIM ** -0.5), ones(RET_W, 1.0),
                                 ones(MEM_W, MEM_HEAD_DIM ** -0.5)]),
        w_kv=cols("ka", "va"),
        w_rot=cols("qr", "kr"),
        s_rot=jnp.concatenate([ones(RET_W, 1.0), ones(RET_W, RET_DIM ** -0.5)]),
        w_g=cols("gr"),
        w_in16=w_in.astype(BF16), gate_col0=cuts["gl"][0],
        w_br=w_br.astype(BF16), w_o=w_o.astype(BF16), w_up=w_up.astype(BF16), w_down=w_down.astype(BF16),
    )


def _run_group(x, pos0, mem_k, mem_v, swa_cache, ret_state, conv_buf, wts, g_final, log_g):
    b, s, d = x.shape
    n = b * s
    tm = min(ROW_TILE, n)
    x2 = x.reshape(n, d)
    plain, u = _norm_mm(x2, wts["g_mix"], wts["w_plain"], wts["s_plain"], tm=tm, tn=COL_TILE, name="proj_plain")
    kv = _mm(u, wts["w_kv"], epilogue="plain", out_dtype=F32, tm=tm, tn=2 * SWA_KV, name="proj_kv")
    cc, ss = _rope_tables(pos0, s)
    if s < tm:
        cc, ss = jnp.tile(cc, (tm // s, 1)), jnp.tile(ss, (tm // s, 1))
    rot = _mm(u, wts["w_rot"], epilogue="rotary", out_dtype=BF16, tm=tm, tn=COL_TILE,
              colvec=wts["s_rot"], tables=(cc, ss), name="proj_rot")
    sgate = _mm(u, wts["w_g"], epilogue="silu", out_dtype=BF16, tm=tm, tn=COL_TILE, name="proj_silu")

    plain3 = plain.reshape(b, s, -1)
    kv3 = kv.reshape(b, s, 2 * SWA_KV)
    if swa_cache is None:
        o_swa = _swa(plain3, 0, kv3, 0, kv3, 1, wts["sink"], sq=s, qt=min(SWA_ROWS, s), kv_off=0)
        new_k = kv3[:, s - WINDOW:, :SWA_KV]
        new_v = kv3[:, s - WINDOW:, SWA_KV:]
    else:
        ck = swa_cache[0].reshape(b, -1, SWA_KV)
        cv = swa_cache[1].reshape(b, -1, SWA_KV)
        n_keep = ck.shape[1]
        k_all = jnp.concatenate([ck, kv3[:, :, :SWA_KV]], axis=1)
        v_all = jnp.concatenate([cv, kv3[:, :, SWA_KV:]], axis=1)
        o_swa = _swa(plain3, 0, k_all, 0, v_all, 0, wts["sink"], sq=s, qt=s, kv_off=n_keep)
        new_k = k_all[:, -n_keep:]
        new_v = v_all[:, -n_keep:]
    state0 = jnp.zeros((b, RET_HEADS, RET_DIM, RET_DIM), F32) if ret_state is None else ret_state
    o_ret, s_new = _retention(rot.reshape(b, s, -1), 0, 1, plain3, 1, sgate.reshape(b, s, -1), state0, log_g,
                              s=s, c=min(RET_ROWS, s))
    o_mem = _mem_attend(plain3, 2, mem_k.reshape(b, N_MEM, MEM_W), mem_v.reshape(b, N_MEM, MEM_W),
                        s=s, tq=min(MEM_ROWS, s))
    merged = _merge(u, o_swa.reshape(n, -1), o_ret.reshape(n, -1), o_mem.reshape(n, -1), wts["w_in16"],
                    wts["gate_col0"], wts["b_gate"], wts["w_br"], tm=tm, tn=MERGE_COLS)
    h1, u2 = _out_proj(merged, wts["w_o"], x2, wts["g_ffn"], tm=min(OUT_PROJ_ROWS, n))
    if s >= ROW_TILE:
        act, co = _ffn_up(u2.reshape(b, s, d), conv_buf, wts["w_up"], wts["w_conv"], wts["b_conv"],
                          tm=ROW_TILE, tf=FF_TILE, nseg=1)
    else:
        act, co = _ffn_up(u2.reshape(1, n, d), conv_buf, wts["w_up"], wts["w_conv"], wts["b_conv"],
                          tm=n, tf=FF_TILE, nseg=b)
    y = _ffn_down(act.reshape(n, D_FF), wts["w_down"], h1, g_final, tm=tm, td=FF_TILE)
    nblk, tfb = co.shape[2], co.shape[4]
    last = co.reshape(b, -1, nblk, 8, tfb)[:, -1, :, 6:8, :]
    new_buf = last.transpose(0, 2, 1, 3).reshape(b, 2, nblk * tfb)
    return (y.reshape(b, s, d), new_k.reshape(b, -1, SWA_KV_HEADS, SWA_HEAD_DIM),
            new_v.reshape(b, -1, SWA_KV_HEADS, SWA_HEAD_DIM), s_new, new_buf)


def _memory_kv(mem, g_mem, w_mem_kv):
    b, m, d = mem.shape
    u = _rmsnorm(mem.reshape(b * m, d), g_mem, min(OUT_PROJ_ROWS, b * m))
    kv = _mm(u, w_mem_kv.astype(BF16), epilogue="plain", out_dtype=F32, tm=min(ROW_TILE, b * m), tn=COL_TILE,
             name="mem_kv")
    return (kv[:, :MEM_W].reshape(b, m, MEM_HEADS, MEM_HEAD_DIM), kv[:, MEM_W:].reshape(b, m, MEM_HEADS, MEM_HEAD_DIM))


def kernel(x_prompt, x_sample, mem_prompt, cache_swa_k, cache_swa_v, state_ret, state_ffn_conv, cache_mem_k, cache_mem_v, g_mix, w_in, b_gate, sink, w_br, w_o, g_mem, w_mem_kv, g_ffn, w_up, w_conv, b_conv, w_down, g_final):
    bp = x_prompt.shape[0]
    depth = w_in.shape[0]
    assert depth == 1, "single-layer problem: the final norm is fused into the layer's FFN"
    log_g = jnp.log1p(-jnp.exp2(-5.0 - jnp.arange(RET_HEADS, dtype=F32)))
    hp, hs = x_prompt, x_sample
    outs_p = [[] for _ in range(6)]
    outs_s = [[] for _ in range(4)]
    for l in range(depth):
        wts = _prep_weights(g_mix[l], w_in[l], b_gate[l], sink[l], w_br[l], w_o[l],
                            g_ffn[l], w_up[l], w_conv[l], b_conv[l], w_down[l])
        mk, mv = _memory_kv(mem_prompt, g_mem[l], w_mem_kv[l])
        zero_buf = jnp.zeros((bp, 2, 2 * D_FF), F32)
        hp, kp, vp, sp, cp = _run_group(hp, 0, mk, mv, None, None, zero_buf, wts, g_final, log_g)
        hs, ksn, vsn, ssn, csn = _run_group(hs, PAST_LEN, cache_mem_k[l], cache_mem_v[l],
                                            (cache_swa_k[l], cache_swa_v[l]), state_ret[l],
                                            state_ffn_conv[l], wts, g_final, log_g)
        for lst, val in zip(outs_p, (kp, vp, sp, cp, mk, mv)):
            lst.append(val)
        for lst, val in zip(outs_s, (ksn, vsn, ssn, csn)):
            lst.append(val)
    return (hp, hs, *[jnp.stack(v) for v in outs_p], *[jnp.stack(v) for v in outs_s])
```

```python
import functools

import jax
import jax.numpy as jnp
import numpy as np
from jax import lax
from jax.experimental import pallas as pl
from jax.experimental.pallas import tpu as pltpu

F32 = jnp.float32
BF16 = jnp.bfloat16

D_MODEL = 2048
CHUNK = 64
WINDOW = 128
SWA_HEADS = 16
SWA_KV_HEADS = 4
SWA_GROUP = SWA_HEADS // SWA_KV_HEADS
SWA_HEAD_DIM = 64
RET_HEADS = 8
RET_DIM = 128
RET_ROPE_BASE = 10000.0
N_MEM = 256
MEM_HEADS = 4
MEM_HEAD_DIM = 256
D_FF = 5632
N_BRANCH = 3
EPS = 1e-6
NEG = -1e30
PAST_LEN = 1024

SWA_Q = SWA_HEADS * SWA_HEAD_DIM
SWA_KV = SWA_KV_HEADS * SWA_HEAD_DIM
RET_W = RET_HEADS * RET_DIM
MEM_W = MEM_HEADS * MEM_HEAD_DIM
LANE = 128
CONV_ROWS = 64
ROW_TILE = 1024
COL_TILE = 1024
MERGE_COLS = 512
OUT_PROJ_ROWS = 512
FF_TILE = 512
SWA_ROWS = 256
RET_ROWS = 256
RET_SEQS = 4
MEM_ROWS = 1024


def _sigmoid(x):
    return 1.0 / (1.0 + jnp.exp(-x))


def _rms(x, g):
    return x * lax.rsqrt(jnp.mean(x * x, axis=-1, keepdims=True) + EPS) * g


def _rmsnorm_kernel(x_ref, g_ref, o_ref):
    o_ref[...] = _rms(x_ref[...], g_ref[...]).astype(o_ref.dtype)


def _rmsnorm(x, g, tm):
    n, d = x.shape
    return pl.pallas_call(
        _rmsnorm_kernel,
        grid=(n // tm,),
        in_specs=[pl.BlockSpec((tm, d), lambda i: (i, 0)),
                  pl.BlockSpec((1, d), lambda i: (0, 0))],
        out_specs=pl.BlockSpec((tm, d), lambda i: (i, 0)),
        out_shape=jax.ShapeDtypeStruct((n, d), BF16),
        name="rmsnorm",
    )(x, g.reshape(1, d))


def _mm_kernel(x_ref, w_ref, *rest, epilogue, tn):
    o_ref = rest[-1]
    acc = jnp.dot(x_ref[...], w_ref[...], preferred_element_type=F32)
    if epilogue == "plain":
        out = acc
    elif epilogue == "silu":
        out = acc * _sigmoid(acc)
    elif epilogue == "rotary":
        cc, ss = rest[1][...], rest[2][...]
        pieces = []
        for j in range(tn // LANE):
            xh = acc[:, j * LANE:(j + 1) * LANE]
            pieces.append(xh * cc + pltpu.roll(xh, LANE // 2, 1) * ss)
        out = jnp.concatenate(pieces, axis=1) * rest[0][...]
    else:
        raise ValueError(epilogue)
    o_ref[...] = out.astype(o_ref.dtype)


def _mm(x, w, *, epilogue, out_dtype, tm, tn, colvec=None, tables=None, name):
    n, k = x.shape
    ncols = w.shape[1]
    in_specs = [pl.BlockSpec((tm, k), lambda i, j: (i, 0)),
                pl.BlockSpec((k, tn), lambda i, j: (0, j))]
    args = [x, w]
    if colvec is not None:
        in_specs.append(pl.BlockSpec((1, tn), lambda i, j: (0, j)))
        args.append(colvec.reshape(1, ncols).astype(F32))
    if tables is not None:
        nblk = tables[0].shape[0] // tm
        for t in tables:
            in_specs.append(pl.BlockSpec((tm, LANE), lambda i, j: (i % nblk, 0)))
            args.append(t)
    return pl.pallas_call(
        functools.partial(_mm_kernel, epilogue=epilogue, tn=tn),
        grid=(n // tm, ncols // tn),
        in_specs=in_specs,
        out_specs=pl.BlockSpec((tm, tn), lambda i, j: (i, j)),
        out_shape=jax.ShapeDtypeStruct((n, ncols), out_dtype),
        compiler_params=pltpu.CompilerParams(dimension_semantics=("parallel", "parallel")),
        name=name,
    )(*args)


def _norm_mm_kernel(x_ref, g_ref, w_ref, cv_ref, o_ref, u_ref):
    @pl.when(pl.program_id(1) == 0)
    def _():
        u_ref[...] = _rms(x_ref[...], g_ref[...]).astype(u_ref.dtype)

    acc = jnp.dot(u_ref[...], w_ref[...], preferred_element_type=F32)
    o_ref[...] = (acc * cv_ref[...]).astype(o_ref.dtype)


def _norm_mm(x, g, w, colvec, *, tm, tn, name):
    n, k = x.shape
    ncols = w.shape[1]
    return pl.pallas_call(
        _norm_mm_kernel,
        grid=(n // tm, ncols // tn),
        in_specs=[pl.BlockSpec((tm, k), lambda i, j: (i, 0)),
                  pl.BlockSpec((1, k), lambda i, j: (0, 0)),
                  pl.BlockSpec((k, tn), lambda i, j: (0, j)),
                  pl.BlockSpec((1, tn), lambda i, j: (0, j))],
        out_specs=[pl.BlockSpec((tm, tn), lambda i, j: (i, j)),
                   pl.BlockSpec((tm, k), lambda i, j: (i, 0))],
        out_shape=[jax.ShapeDtypeStruct((n, ncols), BF16), jax.ShapeDtypeStruct((n, k), BF16)],
        compiler_params=pltpu.CompilerParams(dimension_semantics=("parallel", "arbitrary")),
        name=name,
    )(x, g.reshape(1, k), w, colvec.reshape(1, ncols).astype(F32))


def _swa_kernel(sink_ref, q_ref, kp_ref, kc_ref, vp_ref, vc_ref, o_ref, *scratch, qt, kv_off):
    s_scs, p_scs = scratch[:SWA_KV_HEADS], scratch[SWA_KV_HEADS:]
    i = pl.program_id(1)
    nq = qt // CHUNK
    w = (nq + 2) * CHUNK
    wp = -(-w // LANE) * LANE
    first_ok = jnp.where(i * qt + kv_off >= WINDOW, 0, 2)

    def window(p_ref, c_ref):
        parts = [p_ref[0], c_ref[0]]
        if wp > w:
            parts.append(jnp.zeros((wp - w, SWA_KV), F32))
        return jnp.concatenate(parts, axis=0)

    kwin = window(kp_ref, kc_ref)
    vwin = window(vp_ref, vc_ref)
    lane = lax.broadcasted_iota(jnp.int32, (wp, LANE), 1)
    lane_g = lax.broadcasted_iota(jnp.int32, (wp, SWA_KV), 1) // SWA_HEAD_DIM
    out_g = lax.broadcasted_iota(jnp.int32, (CHUNK, SWA_KV), 1) // SWA_HEAD_DIM

    def block_diag(win, h):
        col, half = divmod(h, 2)
        x = win[:, col * LANE:(col + 1) * LANE]
        xr = pltpu.roll(x, SWA_HEAD_DIM, 1)
        lo = lane < SWA_HEAD_DIM
        r = jnp.where(lo, x, xr) if half == 0 else jnp.where(lo, xr, x)
        r2 = jnp.concatenate([r, r], axis=1).astype(BF16)
        zero = jnp.zeros_like(r2)
        return jnp.concatenate([jnp.where(lane_g == g, r2, zero) for g in range(SWA_GROUP)], axis=0)

    for h in range(SWA_KV_HEADS):
        bk = block_diag(kwin, h)
        bv = block_diag(vwin, h)
        qh = q_ref[0, :, h * SWA_KV:(h + 1) * SWA_KV]
        s_sc, p_sc = s_scs[h], p_scs[h]
        s_sc[...] = lax.dot_general(qh, bk, (((1,), (1,)), ((), ())), preferred_element_type=F32)
        invs = []
        for jq in range(nq):
            rows = slice(jq * CHUNK, (jq + 1) * CHUNK)
            c_lo, c_hi = jq * CHUNK // LANE, ((jq + 3) * CHUNK - 1) // LANE
            width = (c_hi - c_lo + 1) * LANE
            kc = c_lo * (LANE // CHUNK) + lax.broadcasted_iota(jnp.int32, (CHUNK, width), 1) // CHUNK
            visible = (kc >= jq) & (kc <= jq + 2) & (kc >= first_ok)
            inv = jnp.zeros((CHUNK, SWA_KV), F32)
            for g in range(SWA_GROUP):
                l0 = g * wp + c_lo * LANE
                sg = jnp.where(visible, s_sc[rows, l0:l0 + width], NEG)
                sk = sink_ref[h * SWA_GROUP + g]
                m = jnp.maximum(jnp.max(sg, axis=1, keepdims=True), sk)
                p = jnp.exp(sg - m)
                den = jnp.sum(p, axis=1, keepdims=True) + jnp.exp(sk - m)
                p_sc[rows, l0:l0 + width] = p.astype(BF16)
                for c in range(wp // LANE):
                    if not c_lo <= c <= c_hi:
                        p_sc[rows, g * wp + c * LANE:g * wp + (c + 1) * LANE] = jnp.zeros((CHUNK, LANE), BF16)
                inv = jnp.where(out_g == g, 1.0 / den, inv)
            invs.append(inv)
        o = jnp.dot(p_sc[...], bv, preferred_element_type=F32)
        o_ref[0, :, h * SWA_KV:(h + 1) * SWA_KV] = (o * jnp.concatenate(invs, axis=0)).astype(o_ref.dtype)


def _swa(q, q_col, k, k_col, v, v_col, sink, *, sq, qt, kv_off):
    b = q.shape[0]
    cur_off = kv_off // qt
    prev_off = kv_off // WINDOW - 1
    per = qt // WINDOW if qt >= WINDOW else 0
    wp = -(-(qt + WINDOW) // LANE) * LANE

    def prev_map(kcol):
        return lambda bi, i, s: (bi, jnp.maximum(i * per + prev_off, 0), kcol)

    def cur_map(kcol):
        return lambda bi, i, s: (bi, i + cur_off, kcol)

    grid_spec = pltpu.PrefetchScalarGridSpec(
        num_scalar_prefetch=1,
        grid=(b, sq // qt),
        in_specs=[pl.BlockSpec((1, qt, SWA_Q), lambda bi, i, s: (bi, i, q_col)),
                  pl.BlockSpec((1, WINDOW, SWA_KV), prev_map(k_col)),
                  pl.BlockSpec((1, qt, SWA_KV), cur_map(k_col)),
                  pl.BlockSpec((1, WINDOW, SWA_KV), prev_map(v_col)),
                  pl.BlockSpec((1, qt, SWA_KV), cur_map(v_col))],
        out_specs=pl.BlockSpec((1, qt, SWA_Q), lambda bi, i, s: (bi, i, 0)),
        scratch_shapes=([pltpu.VMEM((qt, SWA_GROUP * wp), F32)] * SWA_KV_HEADS
                        + [pltpu.VMEM((qt, SWA_GROUP * wp), BF16)] * SWA_KV_HEADS),
    )
    return pl.pallas_call(
        functools.partial(_swa_kernel, qt=qt, kv_off=kv_off),
        grid_spec=grid_spec,
        out_shape=jax.ShapeDtypeStruct((b, sq, SWA_Q), BF16),
        compiler_params=pltpu.CompilerParams(dimension_semantics=("parallel", "parallel")),
        name="swa",
    )(sink.astype(F32), q, k, k, v, v)


def _ret_kernel(lg_ref, q_ref, k_ref, v_ref, g_ref, s0_ref, o_ref, st_ref, dec_ref, xi_ref, zeta_ref, *, c, bb):
    bi = pl.program_id(0)
    ci = pl.program_id(1)

    @pl.when((bi == 0) & (ci == 0))
    def _():
        n = lax.broadcasted_iota(jnp.int32, (c, c), 0)
        m = lax.broadcasted_iota(jnp.int32, (c, c), 1)
        diff = (n - m).astype(F32)
        nrow = lax.broadcasted_iota(jnp.int32, (c, RET_DIM), 0).astype(F32)
        for h in range(RET_HEADS):
            dec_ref[h] = jnp.where(diff >= 0, jnp.exp(lg_ref[h] * jnp.maximum(diff, 0.0)), 0.0)
            xi_ref[h] = jnp.exp(lg_ref[h] * (nrow + 1.0))
            zeta_ref[h] = jnp.exp(lg_ref[h] * (float(c) - 1.0 - nrow))

    @pl.when(ci == 0)
    def _():
        st_ref[...] = s0_ref[...]

    for bl in range(bb):
        for h in range(RET_HEADS):
            lg = lg_ref[h]
            sl = slice(h * RET_DIM, (h + 1) * RET_DIM)
            qh = q_ref[bl, :, sl]
            kh = k_ref[bl, :, sl]
            vh = v_ref[bl, :, sl]
            state = st_ref[bl, h]
            inner = lax.dot_general(qh, kh, (((1,), (1,)), ((), ())), preferred_element_type=F32) * dec_ref[h]
            o = jnp.dot(inner.astype(BF16), vh, preferred_element_type=F32)
            cross = jnp.dot(qh, state.astype(BF16), preferred_element_type=F32)
            o = o + cross * xi_ref[h]
            kz = (kh.astype(F32) * zeta_ref[h]).astype(BF16)
            upd = lax.dot_general(kz, vh, (((0,), (0,)), ((), ())), preferred_element_type=F32)
            decay_c = jnp.exp(lg * jnp.full((1, RET_DIM), float(c), F32))
            st_ref[bl, h] = decay_c * state + upd
            on = o * lax.rsqrt(jnp.mean(o * o, axis=-1, keepdims=True) + EPS)
            o_ref[bl, :, sl] = (g_ref[bl, :, sl].astype(F32) * on).astype(o_ref.dtype)


def _retention(qk, q_col, k_col, v, v_col, gate, state0, log_g, *, s, c, bb):
    b = qk.shape[0]

    def seq(col):
        return pl.BlockSpec((bb, c, RET_W), lambda bi, ci, lg: (bi, ci, col))

    st_spec = pl.BlockSpec((bb, RET_HEADS, RET_DIM, RET_DIM), lambda bi, ci, lg: (bi, 0, 0, 0))
    grid_spec = pltpu.PrefetchScalarGridSpec(
        num_scalar_prefetch=1,
        grid=(b // bb, s // c),
        in_specs=[seq(q_col), seq(k_col), seq(v_col), seq(0), st_spec],
        out_specs=[seq(0), st_spec],
        scratch_shapes=[pltpu.VMEM((RET_HEADS, c, c), F32), pltpu.VMEM((RET_HEADS, c, RET_DIM), F32),
                        pltpu.VMEM((RET_HEADS, c, RET_DIM), F32)],
    )
    return pl.pallas_call(
        functools.partial(_ret_kernel, c=c, bb=bb),
        grid_spec=grid_spec,
        out_shape=[jax.ShapeDtypeStruct((b, s, RET_W), BF16),
                   jax.ShapeDtypeStruct((b, RET_HEADS, RET_DIM, RET_DIM), F32)],
        compiler_params=pltpu.CompilerParams(dimension_semantics=("arbitrary", "arbitrary")),
        name="retention",
    )(log_g, qk, qk, v, gate, state0)


def _mem_kernel(q_ref, k_ref, v_ref, o_ref):
    for h in range(MEM_HEADS):
        sl = slice(h * MEM_HEAD_DIM, (h + 1) * MEM_HEAD_DIM)
        qh = q_ref[0, :, sl]
        kh = k_ref[0, :, sl].astype(BF16)
        vh = v_ref[0, :, sl].astype(BF16)
        s = lax.dot_general(qh, kh, (((1,), (1,)), ((), ())), preferred_element_type=F32)
        m = jnp.max(s, axis=1, keepdims=True)
        p = jnp.exp(s - m)
        den = jnp.sum(p, axis=1, keepdims=True)
        o = jnp.dot(p.astype(BF16), vh, preferred_element_type=F32)
        o_ref[0, :, sl] = (o * (1.0 / den)).astype(o_ref.dtype)


def _mem_attend(q, q_col, mk, mv, *, s, tq):
    b = q.shape[0]
    kv_spec = pl.BlockSpec((1, N_MEM, MEM_W), lambda bi, i: (bi, 0, 0))
    return pl.pallas_call(
        _mem_kernel,
        grid=(b, s // tq),
        in_specs=[pl.BlockSpec((1, tq, MEM_W), lambda bi, i: (bi, i, q_col)), kv_spec, kv_spec],
        out_specs=pl.BlockSpec((1, tq, MEM_W), lambda bi, i: (bi, i, 0)),
        out_shape=jax.ShapeDtypeStruct((b, s, MEM_W), BF16),
        compiler_params=pltpu.CompilerParams(dimension_semantics=("parallel", "parallel")),
        name="mem_attend",
    )(q, mk, mv)


def _merge_kernel(u_ref, oa_ref, ob_ref, oc_ref, wga_ref, wgb_ref, wgc_ref, ba_ref, bb_ref, bc_ref,
                  wa_ref, wb_ref, wc_ref, o_ref):
    u = u_ref[...]
    acc = None
    for o_r, wg_ref, b_ref, w_ref in ((oa_ref, wga_ref, ba_ref, wa_ref), (ob_ref, wgb_ref, bb_ref, wb_ref),
                                      (oc_ref, wgc_ref, bc_ref, wc_ref)):
        gate = _sigmoid(jnp.dot(u, wg_ref[...], preferred_element_type=F32) + b_ref[...])
        term = gate * jnp.dot(o_r[...], w_ref[...], preferred_element_type=F32)
        acc = term if acc is None else acc + term
    o_ref[...] = acc.astype(o_ref.dtype)


def _merge(u, o_swa, o_ret, o_mem, w_gate, gate_col0, b_gate, w_br, *, tm, tn):
    n, d = u.shape
    kb = o_swa.shape[1]
    nj = D_MODEL // tn
    g0 = gate_col0 // tn
    o_spec = pl.BlockSpec((tm, kb), lambda i, j: (i, 0))

    def wg_spec(br):
        return pl.BlockSpec((d, tn), lambda i, j: (0, g0 + br * nj + j))

    def b_spec(br):
        return pl.BlockSpec((1, tn), lambda i, j: (0, br * nj + j))

    def w_spec(br):
        return pl.BlockSpec((kb, tn), lambda i, j: (br, j))

    bias = b_gate.reshape(1, -1).astype(F32)
    return pl.pallas_call(
        _merge_kernel,
        grid=(n // tm, nj),
        in_specs=[pl.BlockSpec((tm, d), lambda i, j: (i, 0)), o_spec, o_spec, o_spec,
                  wg_spec(0), wg_spec(1), wg_spec(2), b_spec(0), b_spec(1), b_spec(2),
                  w_spec(0), w_spec(1), w_spec(2)],
        out_specs=pl.BlockSpec((tm, tn), lambda i, j: (i, j)),
        out_shape=jax.ShapeDtypeStruct((n, D_MODEL), BF16),
        compiler_params=pltpu.CompilerParams(dimension_semantics=("parallel", "parallel")),
        name="merge",
    )(u, o_swa, o_ret, o_mem, w_gate, w_gate, w_gate, bias, bias, bias, w_br, w_br, w_br)


def _ffn_up_kernel(u_ref, wug_ref, wuv_ref, wc_ref, b_ref, cb_ref, act_ref, co_ref,
                   ag_sc, av_sc, cg_sc, cv_sc, *, tm, tf, nf, nseg):
    i = pl.program_id(1)
    fb = pl.program_id(2)
    seg_t = tm // nseg
    stride = seg_t + 8
    if nseg == 1:
        @pl.when((i == 0) & (fb == 0))
        def _():
            for f in range(nf):
                cg_sc[f, 6:8, :] = cb_ref[0, f]
                cv_sc[f, 6:8, :] = cb_ref[0, nf + f]

    u = u_ref[0]
    for a_sc, wu_ref, blk, carry_sc in ((ag_sc, wug_ref, fb, cg_sc), (av_sc, wuv_ref, nf + fb, cv_sc)):
        a = jnp.dot(u, wu_ref[...], preferred_element_type=F32)
        for s in range(nseg):
            a_sc[s * stride + 8:(s + 1) * stride, :] = a[s * seg_t:(s + 1) * seg_t]
            co_ref[0, s, blk] = a[(s + 1) * seg_t - 8:(s + 1) * seg_t]
        if nseg == 1:
            a_sc[6:8, :] = carry_sc[fb, 6:8, :]
            carry_sc[fb] = a[tm - 8:tm]
        else:
            for s in range(nseg):
                a_sc[s * stride + 6:s * stride + 8, :] = cb_ref[s, blk]

    def conv(a_sc, blk, r0, c0):
        tap = lambda k: a_sc[r0 - k:r0 - k + CONV_ROWS, c0:c0 + LANE]
        wc = lambda k: wc_ref[blk, k:k + 1, c0:c0 + LANE]
        return b_ref[blk, :, c0:c0 + LANE] + (wc(0) * tap(2) + wc(1) * tap(1) + wc(2) * tap(0))

    for c0 in range(0, tf, LANE):
        for s in range(nseg):
            for r in range(0, seg_t, CONV_ROWS):
                cg = conv(ag_sc, fb, s * stride + 8 + r, c0)
                cv = conv(av_sc, nf + fb, s * stride + 8 + r, c0)
                act_ref[0, s * seg_t + r:s * seg_t + r + CONV_ROWS, c0:c0 + LANE] = (
                    cg * _sigmoid(cg) * cv).astype(act_ref.dtype)


def _ffn_up(u, conv_buf, w_up, w_conv, b_conv, *, tm, tf, nseg):
    bt, rows, d = u.shape
    nf = D_FF // tf
    nt = rows // tm
    nb = bt * nseg
    wc_blocks = w_conv.reshape(3, 2 * nf, tf).transpose(1, 0, 2)
    b_blocks = b_conv.reshape(2 * nf, 1, tf)
    cb_blocks = conv_buf.reshape(nb, 2, 2 * nf, tf).transpose(0, 2, 1, 3)
    whole = lambda *shape: pl.BlockSpec(shape, lambda b, i, f: (0,) * len(shape))
    return pl.pallas_call(
        functools.partial(_ffn_up_kernel, tm=tm, tf=tf, nf=nf, nseg=nseg),
        grid=(bt, nt, nf),
        in_specs=[pl.BlockSpec((1, tm, d), lambda b, i, f: (b, i, 0)),
                  pl.BlockSpec((d, tf), lambda b, i, f: (0, f)),
                  pl.BlockSpec((d, tf), lambda b, i, f: (0, nf + f)),
                  whole(2 * nf, 3, tf),
                  whole(2 * nf, 1, tf),
                  pl.BlockSpec((nseg, 2 * nf, 2, tf), lambda b, i, f: (b, 0, 0, 0))],
        out_specs=[pl.BlockSpec((1, tm, tf), lambda b, i, f: (b, i, f)),
                   pl.BlockSpec((1, nseg, 2 * nf, 8, tf), lambda b, i, f: (b * nt + i, 0, 0, 0, 0))],
        out_shape=[jax.ShapeDtypeStruct((bt, rows, D_FF), BF16),
                   jax.ShapeDtypeStruct((bt * nt, nseg, 2 * nf, 8, tf), F32)],
        scratch_shapes=[pltpu.VMEM((tm + 8 * nseg, tf), F32), pltpu.VMEM((tm + 8 * nseg, tf), F32),
                        pltpu.VMEM((nf, 8, tf), F32), pltpu.VMEM((nf, 8, tf), F32)],
        compiler_params=pltpu.CompilerParams(dimension_semantics=("arbitrary", "arbitrary", "arbitrary")),
        name="ffn_up",
    )(u, w_up, w_up, wc_blocks, b_blocks, cb_blocks)


def _ffn_down_kernel(act_ref, wd_ref, h_ref, g_ref, y_ref, *, nd, td):
    j = pl.program_id(1)
    for jj in range(nd):
        @pl.when(j == jj)
        def _():
            y_ref[:, jj * td:(jj + 1) * td] = h_ref[...] + jnp.dot(act_ref[...], wd_ref[...],
                                                                   preferred_element_type=F32)

    @pl.when(j == nd - 1)
    def _():
        y_ref[...] = _rms(y_ref[...], g_ref[...])


def _ffn_down(act, w_down, h, g_final, *, tm, td):
    n, f = act.shape
    d = h.shape[1]
    nd = d // td
    return pl.pallas_call(
        functools.partial(_ffn_down_kernel, nd=nd, td=td),
        grid=(n // tm, nd),
        in_specs=[pl.BlockSpec((tm, f), lambda i, j: (i, 0)),
                  pl.BlockSpec((f, td), lambda i, j: (0, j)),
                  pl.BlockSpec((tm, td), lambda i, j: (i, j)),
                  pl.BlockSpec((1, d), lambda i, j: (0, 0))],
        out_specs=pl.BlockSpec((tm, d), lambda i, j: (i, 0)),
        out_shape=jax.ShapeDtypeStruct((n, d), F32),
        compiler_params=pltpu.CompilerParams(dimension_semantics=("parallel", "arbitrary")),
        name="ffn_down",
    )(act, w_down, h, g_final.reshape(1, d))


def _out_proj_kernel(x_ref, w_ref, res_ref, g_ref, h_ref, u_ref):
    h = res_ref[...] + jnp.dot(x_ref[...], w_ref[...], preferred_element_type=F32)
    h_ref[...] = h
    u_ref[...] = _rms(h, g_ref[...]).astype(u_ref.dtype)


def _out_proj(x, w, residual, g, *, tm):
    n, k = x.shape
    d = w.shape[1]
    row = lambda width: pl.BlockSpec((tm, width), lambda i: (i, 0))
    return pl.pallas_call(
        _out_proj_kernel,
        grid=(n // tm,),
        in_specs=[row(k), pl.BlockSpec((k, d), lambda i: (0, 0)), row(d), pl.BlockSpec((1, d), lambda i: (0, 0))],
        out_specs=[row(d), row(d)],
        out_shape=[jax.ShapeDtypeStruct((n, d), F32), jax.ShapeDtypeStruct((n, d), BF16)],
        compiler_params=pltpu.CompilerParams(dimension_semantics=("parallel",)),
        name="proj_out",
    )(x, w, residual, g.reshape(1, d))


def _rope_tables(pos0, s):
    half = RET_DIM // 2
    inv_freq = 1.0 / (RET_ROPE_BASE ** np.linspace(0.0, 1.0, half))
    ang = (pos0 + np.arange(s))[:, None] * inv_freq[None, :]
    cos, sin = np.cos(ang), np.sin(ang)
    return (jnp.asarray(np.concatenate([cos, cos], axis=1), F32),
            jnp.asarray(np.concatenate([-sin, sin], axis=1), F32))


def _prep_weights(g_mix, w_in, b_gate, sink, w_br, w_o, g_ffn, w_up, w_conv, b_conv, w_down):
    o = 0
    cuts = {}
    for name, width in (("qa", SWA_Q), ("ka", SWA_KV), ("va", SWA_KV), ("qr", RET_W), ("kr", RET_W),
                        ("vr", RET_W), ("gr", RET_W), ("qm", MEM_W), ("gl", N_BRANCH * D_MODEL)):
        cuts[name] = (o, o + width)
        o += width
    cols = lambda *names: jnp.concatenate([w_in[:, cuts[n][0]:cuts[n][1]] for n in names], axis=1).astype(BF16)
    ones = lambda n, v: jnp.full((n,), v, F32)
    return dict(
        g_mix=g_mix, b_gate=b_gate, sink=sink, g_ffn=g_ffn, w_conv=w_conv, b_conv=b_conv,
        w_plain=cols("qa", "vr", "qm"),
        s_plain=jnp.concatenate([ones(SWA_Q, SWA_HEAD_DIM ** -0.5), ones(RET_W, 1.0),
                                 ones(MEM_W, MEM_HEAD_DIM ** -0.5)]),
        w_kv=cols("ka", "va"),
        w_rot=cols("qr", "kr"),
        s_rot=jnp.concatenate([ones(RET_W, 1.0), ones(RET_W, RET_DIM ** -0.5)]),
        w_g=cols("gr"),
        w_in16=w_in.astype(BF16), gate_col0=cuts["gl"][0],
        w_br=w_br.astype(BF16), w_o=w_o.astype(BF16), w_up=w_up.astype(BF16), w_down=w_down.astype(BF16),
    )


def _run_group(x, pos0, mem_k, mem_v, swa_cache, ret_state, conv_buf, wts, g_final, log_g):
    b, s, d = x.shape
    n = b * s
    tm = min(ROW_TILE, n)
    x2 = x.reshape(n, d)
    plain, u = _norm_mm(x2, wts["g_mix"], wts["w_plain"], wts["s_plain"], tm=tm, tn=COL_TILE, name="proj_plain")
    kv = _mm(u, wts["w_kv"], epilogue="plain", out_dtype=F32, tm=tm, tn=2 * SWA_KV, name="proj_kv")
    cc, ss = _rope_tables(pos0, s)
    if s < tm:
        cc, ss = jnp.tile(cc, (tm // s, 1)), jnp.tile(ss, (tm // s, 1))
    rot = _mm(u, wts["w_rot"], epilogue="rotary", out_dtype=BF16, tm=tm, tn=COL_TILE,
              colvec=wts["s_rot"], tables=(cc, ss), name="proj_rot")
    sgate = _mm(u, wts["w_g"], epilogue="silu", out_dtype=BF16, tm=tm, tn=COL_TILE, name="proj_silu")

    plain3 = plain.reshape(b, s, -1)
    kv3 = kv.reshape(b, s, 2 * SWA_KV)
    if swa_cache is None:
        o_swa = _swa(plain3, 0, kv3, 0, kv3, 1, wts["sink"], sq=s, qt=min(SWA_ROWS, s), kv_off=0)
        new_k = kv3[:, s - WINDOW:, :SWA_KV]
        new_v = kv3[:, s - WINDOW:, SWA_KV:]
    else:
        ck = swa_cache[0].reshape(b, -1, SWA_KV)
        cv = swa_cache[1].reshape(b, -1, SWA_KV)
        n_keep = ck.shape[1]
        k_all = jnp.concatenate([ck, kv3[:, :, :SWA_KV]], axis=1)
        v_all = jnp.concatenate([cv, kv3[:, :, SWA_KV:]], axis=1)
        o_swa = _swa(plain3, 0, k_all, 0, v_all, 0, wts["sink"], sq=s, qt=s, kv_off=n_keep)
        new_k = k_all[:, -n_keep:]
        new_v = v_all[:, -n_keep:]
    state0 = jnp.zeros((b, RET_HEADS, RET_DIM, RET_DIM), F32) if ret_state is None else ret_state
    o_ret, s_new = _retention(rot.reshape(b, s, -1), 0, 1, plain3, 1, sgate.reshape(b, s, -1), state0, log_g,
                              s=s, c=min(RET_ROWS, s), bb=min(RET_SEQS, b))
    o_mem = _mem_attend(plain3, 2, mem_k.reshape(b, N_MEM, MEM_W), mem_v.reshape(b, N_MEM, MEM_W),
                        s=s, tq=min(MEM_ROWS, s))
    merged = _merge(u, o_swa.reshape(n, -1), o_ret.reshape(n, -1), o_mem.reshape(n, -1), wts["w_in16"],
                    wts["gate_col0"], wts["b_gate"], wts["w_br"], tm=tm, tn=MERGE_COLS)
    h1, u2 = _out_proj(merged, wts["w_o"], x2, wts["g_ffn"], tm=min(OUT_PROJ_ROWS, n))
    if s >= ROW_TILE:
        act, co = _ffn_up(u2.reshape(b, s, d), conv_buf, wts["w_up"], wts["w_conv"], wts["b_conv"],
                          tm=ROW_TILE, tf=FF_TILE, nseg=1)
    else:
        act, co = _ffn_up(u2.reshape(1, n, d), conv_buf, wts["w_up"], wts["w_conv"], wts["b_conv"],
                          tm=n, tf=FF_TILE, nseg=b)
    y = _ffn_down(act.reshape(n, D_FF), wts["w_down"], h1, g_final, tm=tm, td=FF_TILE)
    nblk, tfb = co.shape[2], co.shape[4]
    last = co.reshape(b, -1, nblk, 8, tfb)[:, -1, :, 6:8, :]
    new_buf = last.transpose(0, 2, 1, 3).reshape(b, 2, nblk * tfb)
    return (y.reshape(b, s, d), new_k.reshape(b, -1, SWA_KV_HEADS, SWA_HEAD_DIM),
            new_v.reshape(b, -1, SWA_KV_HEADS, SWA_HEAD_DIM), s_new, new_buf)


def _memory_kv(mem, g_mem, w_mem_kv):
    b, m, d = mem.shape
    u = _rmsnorm(mem.reshape(b * m, d), g_mem, min(OUT_PROJ_ROWS, b * m))
    kv = _mm(u, w_mem_kv.astype(BF16), epilogue="plain", out_dtype=F32, tm=min(ROW_TILE, b * m), tn=COL_TILE,
             name="mem_kv")
    return (kv[:, :MEM_W].reshape(b, m, MEM_HEADS, MEM_HEAD_DIM), kv[:, MEM_W:].reshape(b, m, MEM_HEADS, MEM_HEAD_DIM))


def kernel(x_prompt, x_sample, mem_prompt, cache_swa_k, cache_swa_v, state_ret, state_ffn_conv, cache_mem_k, cache_mem_v, g_mix, w_in, b_gate, sink, w_br, w_o, g_mem, w_mem_kv, g_ffn, w_up, w_conv, b_conv, w_down, g_final):
    bp = x_prompt.shape[0]
    depth = w_in.shape[0]
    assert depth == 1, "single-layer problem: the final norm is fused into the layer's FFN"
    log_g = jnp.log1p(-jnp.exp2(-5.0 - jnp.arange(RET_HEADS, dtype=F32)))
    hp, hs = x_prompt, x_sample
    outs_p = [[] for _ in range(6)]
    outs_s = [[] for _ in range(4)]
    for l in range(depth):
        wts = _prep_weights(g_mix[l], w_in[l], b_gate[l], sink[l], w_br[l], w_o[l],
                            g_ffn[l], w_up[l], w_conv[l], b_conv[l], w_down[l])
        mk, mv = _memory_kv(mem_prompt, g_mem[l], w_mem_kv[l])
        zero_buf = jnp.zeros((bp, 2, 2 * D_FF), F32)
        hp, kp, vp, sp, cp = _run_group(hp, 0, mk, mv, None, None, zero_buf, wts, g_final, log_g)
        hs, ksn, vsn, ssn, csn = _run_group(hs, PAST_LEN, cache_mem_k[l], cache_mem_v[l],
                                            (cache_swa_k[l], cache_swa_v[l]), state_ret[l],
                                            state_ffn_conv[l], wts, g_final, log_g)
        for lst, val in zip(outs_p, (kp, vp, sp, cp, mk, mv)):
            lst.append(val)
        for lst, val in zip(outs_s, (ksn, vsn, ssn, csn)):
            lst.append(val)
    return (hp, hs, *[jnp.stack(v) for v in outs_p], *[jnp.stack(v) for v in outs_s])
```

```python
import functools

import jax
import jax.numpy as jnp
import numpy as np
from jax import lax
from jax.experimental import pallas as pl
from jax.experimental.pallas import tpu as pltpu

F32 = jnp.float32
BF16 = jnp.bfloat16

D_MODEL = 2048
CHUNK = 64
WINDOW = 128
SWA_HEADS = 16
SWA_KV_HEADS = 4
SWA_GROUP = SWA_HEADS // SWA_KV_HEADS
SWA_HEAD_DIM = 64
RET_HEADS = 8
RET_DIM = 128
RET_ROPE_BASE = 10000.0
N_MEM = 256
MEM_HEADS = 4
MEM_HEAD_DIM = 256
D_FF = 5632
N_BRANCH = 3
EPS = 1e-6
NEG = -1e30
PAST_LEN = 1024

SWA_Q = SWA_HEADS * SWA_HEAD_DIM
SWA_KV = SWA_KV_HEADS * SWA_HEAD_DIM
RET_W = RET_HEADS * RET_DIM
MEM_W = MEM_HEADS * MEM_HEAD_DIM
LANE = 128
CONV_ROWS = 64
ROW_TILE = 1024
COL_TILE = 1024
MERGE_COLS = 512
OUT_PROJ_ROWS = 512
FF_TILE = 512
SWA_ROWS = 256
SWA_SEQS = 4
RET_ROWS = 256
RET_SEQS = 4
MEM_ROWS = 1024


def _sigmoid(x):
    return 1.0 / (1.0 + jnp.exp(-x))


def _rms(x, g):
    return x * lax.rsqrt(jnp.mean(x * x, axis=-1, keepdims=True) + EPS) * g


def _rmsnorm_kernel(x_ref, g_ref, o_ref):
    o_ref[...] = _rms(x_ref[...], g_ref[...]).astype(o_ref.dtype)


def _rmsnorm(x, g, tm):
    n, d = x.shape
    return pl.pallas_call(
        _rmsnorm_kernel,
        grid=(n // tm,),
        in_specs=[pl.BlockSpec((tm, d), lambda i: (i, 0)),
                  pl.BlockSpec((1, d), lambda i: (0, 0))],
        out_specs=pl.BlockSpec((tm, d), lambda i: (i, 0)),
        out_shape=jax.ShapeDtypeStruct((n, d), BF16),
        name="rmsnorm",
    )(x, g.reshape(1, d))


def _mm_kernel(x_ref, w_ref, *rest, epilogue, tn):
    o_ref = rest[-1]
    acc = jnp.dot(x_ref[...], w_ref[...], preferred_element_type=F32)
    if epilogue == "plain":
        out = acc
    elif epilogue == "silu":
        out = acc * _sigmoid(acc)
    elif epilogue == "rotary":
        cc, ss = rest[1][...], rest[2][...]
        pieces = []
        for j in range(tn // LANE):
            xh = acc[:, j * LANE:(j + 1) * LANE]
            pieces.append(xh * cc + pltpu.roll(xh, LANE // 2, 1) * ss)
        out = jnp.concatenate(pieces, axis=1) * rest[0][...]
    else:
        raise ValueError(epilogue)
    o_ref[...] = out.astype(o_ref.dtype)


def _mm(x, w, *, epilogue, out_dtype, tm, tn, colvec=None, tables=None, name):
    n, k = x.shape
    ncols = w.shape[1]
    in_specs = [pl.BlockSpec((tm, k), lambda i, j: (i, 0)),
                pl.BlockSpec((k, tn), lambda i, j: (0, j))]
    args = [x, w]
    if colvec is not None:
        in_specs.append(pl.BlockSpec((1, tn), lambda i, j: (0, j)))
        args.append(colvec.reshape(1, ncols).astype(F32))
    if tables is not None:
        nblk = tables[0].shape[0] // tm
        for t in tables:
            in_specs.append(pl.BlockSpec((tm, LANE), lambda i, j: (i % nblk, 0)))
            args.append(t)
    return pl.pallas_call(
        functools.partial(_mm_kernel, epilogue=epilogue, tn=tn),
        grid=(n // tm, ncols // tn),
        in_specs=in_specs,
        out_specs=pl.BlockSpec((tm, tn), lambda i, j: (i, j)),
        out_shape=jax.ShapeDtypeStruct((n, ncols), out_dtype),
        compiler_params=pltpu.CompilerParams(dimension_semantics=("parallel", "parallel")),
        name=name,
    )(*args)


def _norm_mm_kernel(x_ref, g_ref, w_ref, cv_ref, o_ref, u_ref):
    @pl.when(pl.program_id(1) == 0)
    def _():
        u_ref[...] = _rms(x_ref[...], g_ref[...]).astype(u_ref.dtype)

    acc = jnp.dot(u_ref[...], w_ref[...], preferred_element_type=F32)
    o_ref[...] = (acc * cv_ref[...]).astype(o_ref.dtype)


def _norm_mm(x, g, w, colvec, *, tm, tn, name):
    n, k = x.shape
    ncols = w.shape[1]
    return pl.pallas_call(
        _norm_mm_kernel,
        grid=(n // tm, ncols // tn),
        in_specs=[pl.BlockSpec((tm, k), lambda i, j: (i, 0)),
                  pl.BlockSpec((1, k), lambda i, j: (0, 0)),
                  pl.BlockSpec((k, tn), lambda i, j: (0, j)),
                  pl.BlockSpec((1, tn), lambda i, j: (0, j))],
        out_specs=[pl.BlockSpec((tm, tn), lambda i, j: (i, j)),
                   pl.BlockSpec((tm, k), lambda i, j: (i, 0))],
        out_shape=[jax.ShapeDtypeStruct((n, ncols), BF16), jax.ShapeDtypeStruct((n, k), BF16)],
        compiler_params=pltpu.CompilerParams(dimension_semantics=("parallel", "arbitrary")),
        name=name,
    )(x, g.reshape(1, k), w, colvec.reshape(1, ncols).astype(F32))


def _swa_kernel(sink_ref, q_ref, kp_ref, kc_ref, vp_ref, vc_ref, o_ref, *scratch, qt, kv_off, bb):
    n_sc = bb * SWA_KV_HEADS
    for bl in range(bb):
        _swa_one(sink_ref, q_ref.at[bl], kp_ref.at[bl], kc_ref.at[bl], vp_ref.at[bl], vc_ref.at[bl], o_ref.at[bl],
                 scratch[bl * SWA_KV_HEADS:(bl + 1) * SWA_KV_HEADS],
                 scratch[n_sc + bl * SWA_KV_HEADS:n_sc + (bl + 1) * SWA_KV_HEADS], qt=qt, kv_off=kv_off)


def _swa_one(sink_ref, q_ref, kp_ref, kc_ref, vp_ref, vc_ref, o_ref, s_scs, p_scs, *, qt, kv_off):
    i = pl.program_id(1)
    nq = qt // CHUNK
    w = (nq + 2) * CHUNK
    wp = -(-w // LANE) * LANE
    first_ok = jnp.where(i * qt + kv_off >= WINDOW, 0, 2)

    def window(p_ref, c_ref):
        parts = [p_ref[...], c_ref[...]]
        if wp > w:
            parts.append(jnp.zeros((wp - w, SWA_KV), F32))
        return jnp.concatenate(parts, axis=0)

    kwin = window(kp_ref, kc_ref)
    vwin = window(vp_ref, vc_ref)
    lane = lax.broadcasted_iota(jnp.int32, (wp, LANE), 1)
    lane_g = lax.broadcasted_iota(jnp.int32, (wp, SWA_KV), 1) // SWA_HEAD_DIM
    out_g = lax.broadcasted_iota(jnp.int32, (CHUNK, SWA_KV), 1) // SWA_HEAD_DIM

    def block_diag(win, h):
        col, half = divmod(h, 2)
        x = win[:, col * LANE:(col + 1) * LANE]
        xr = pltpu.roll(x, SWA_HEAD_DIM, 1)
        lo = lane < SWA_HEAD_DIM
        r = jnp.where(lo, x, xr) if half == 0 else jnp.where(lo, xr, x)
        r2 = jnp.concatenate([r, r], axis=1).astype(BF16)
        zero = jnp.zeros_like(r2)
        return jnp.concatenate([jnp.where(lane_g == g, r2, zero) for g in range(SWA_GROUP)], axis=0)

    for h in range(SWA_KV_HEADS):
        bk = block_diag(kwin, h)
        bv = block_diag(vwin, h)
        qh = q_ref[:, h * SWA_KV:(h + 1) * SWA_KV]
        s_sc, p_sc = s_scs[h], p_scs[h]
        s_sc[...] = lax.dot_general(qh, bk, (((1,), (1,)), ((), ())), preferred_element_type=F32)
        invs = []
        for jq in range(nq):
            rows = slice(jq * CHUNK, (jq + 1) * CHUNK)
            c_lo, c_hi = jq * CHUNK // LANE, ((jq + 3) * CHUNK - 1) // LANE
            width = (c_hi - c_lo + 1) * LANE
            kc = c_lo * (LANE // CHUNK) + lax.broadcasted_iota(jnp.int32, (CHUNK, width), 1) // CHUNK
            visible = (kc >= jq) & (kc <= jq + 2) & (kc >= first_ok)
            inv = jnp.zeros((CHUNK, SWA_KV), F32)
            for g in range(SWA_GROUP):
                l0 = g * wp + c_lo * LANE
                sg = jnp.where(visible, s_sc[rows, l0:l0 + width], NEG)
                sk = sink_ref[h * SWA_GROUP + g]
                m = jnp.maximum(jnp.max(sg, axis=1, keepdims=True), sk)
                p = jnp.exp(sg - m)
                den = jnp.sum(p, axis=1, keepdims=True) + jnp.exp(sk - m)
                p_sc[rows, l0:l0 + width] = p.astype(BF16)
                for c in range(wp // LANE):
                    if not c_lo <= c <= c_hi:
                        p_sc[rows, g * wp + c * LANE:g * wp + (c + 1) * LANE] = jnp.zeros((CHUNK, LANE), BF16)
                inv = jnp.where(out_g == g, 1.0 / den, inv)
            invs.append(inv)
        o = jnp.dot(p_sc[...], bv, preferred_element_type=F32)
        o_ref[:, h * SWA_KV:(h + 1) * SWA_KV] = (o * jnp.concatenate(invs, axis=0)).astype(o_ref.dtype)


def _swa(q, q_col, k, k_col, v, v_col, sink, *, sq, qt, kv_off, bb):
    b = q.shape[0]
    cur_off = kv_off // qt
    prev_off = kv_off // WINDOW - 1
    per = qt // WINDOW if qt >= WINDOW else 0
    wp = -(-(qt + WINDOW) // LANE) * LANE

    def prev_map(kcol):
        return lambda bi, i, s: (bi, jnp.maximum(i * per + prev_off, 0), kcol)

    def cur_map(kcol):
        return lambda bi, i, s: (bi, i + cur_off, kcol)

    grid_spec = pltpu.PrefetchScalarGridSpec(
        num_scalar_prefetch=1,
        grid=(b // bb, sq // qt),
        in_specs=[pl.BlockSpec((bb, qt, SWA_Q), lambda bi, i, s: (bi, i, q_col)),
                  pl.BlockSpec((bb, WINDOW, SWA_KV), prev_map(k_col)),
                  pl.BlockSpec((bb, qt, SWA_KV), cur_map(k_col)),
                  pl.BlockSpec((bb, WINDOW, SWA_KV), prev_map(v_col)),
                  pl.BlockSpec((bb, qt, SWA_KV), cur_map(v_col))],
        out_specs=pl.BlockSpec((bb, qt, SWA_Q), lambda bi, i, s: (bi, i, 0)),
        scratch_shapes=([pltpu.VMEM((qt, SWA_GROUP * wp), F32)] * (bb * SWA_KV_HEADS)
                        + [pltpu.VMEM((qt, SWA_GROUP * wp), BF16)] * (bb * SWA_KV_HEADS)),
    )
    return pl.pallas_call(
        functools.partial(_swa_kernel, qt=qt, kv_off=kv_off, bb=bb),
        grid_spec=grid_spec,
        out_shape=jax.ShapeDtypeStruct((b, sq, SWA_Q), BF16),
        compiler_params=pltpu.CompilerParams(dimension_semantics=("parallel", "parallel")),
        name="swa",
    )(sink.astype(F32), q, k, k, v, v)


def _ret_kernel(lg_ref, q_ref, k_ref, v_ref, g_ref, s0_ref, o_ref, st_ref, dec_ref, xi_ref, zeta_ref, *, c, bb):
    bi = pl.program_id(0)
    ci = pl.program_id(1)

    @pl.when((bi == 0) & (ci == 0))
    def _():
        n = lax.broadcasted_iota(jnp.int32, (c, c), 0)
        m = lax.broadcasted_iota(jnp.int32, (c, c), 1)
        diff = (n - m).astype(F32)
        nrow = lax.broadcasted_iota(jnp.int32, (c, RET_DIM), 0).astype(F32)
        for h in range(RET_HEADS):
            dec_ref[h] = jnp.where(diff >= 0, jnp.exp(lg_ref[h] * jnp.maximum(diff, 0.0)), 0.0)
            xi_ref[h] = jnp.exp(lg_ref[h] * (nrow + 1.0))
            zeta_ref[h] = jnp.exp(lg_ref[h] * (float(c) - 1.0 - nrow))

    @pl.when(ci == 0)
    def _():
        st_ref[...] = s0_ref[...]

    for bl in range(bb):
        for h in range(RET_HEADS):
            lg = lg_ref[h]
            sl = slice(h * RET_DIM, (h + 1) * RET_DIM)
            qh = q_ref[bl, :, sl]
            kh = k_ref[bl, :, sl]
            vh = v_ref[bl, :, sl]
            state = st_ref[bl, h]
            inner = lax.dot_general(qh, kh, (((1,), (1,)), ((), ())), preferred_element_type=F32) * dec_ref[h]
            o = jnp.dot(inner.astype(BF16), vh, preferred_element_type=F32)
            cross = jnp.dot(qh, state.astype(BF16), preferred_element_type=F32)
            o = o + cross * xi_ref[h]
            kz = (kh.astype(F32) * zeta_ref[h]).astype(BF16)
            upd = lax.dot_general(kz, vh, (((0,), (0,)), ((), ())), preferred_element_type=F32)
            decay_c = jnp.exp(lg * jnp.full((1, RET_DIM), float(c), F32))
            st_ref[bl, h] = decay_c * state + upd
            on = o * lax.rsqrt(jnp.mean(o * o, axis=-1, keepdims=True) + EPS)
            o_ref[bl, :, sl] = (g_ref[bl, :, sl].astype(F32) * on).astype(o_ref.dtype)


def _retention(qk, q_col, k_col, v, v_col, gate, state0, log_g, *, s, c, bb):
    b = qk.shape[0]

    def seq(col):
        return pl.BlockSpec((bb, c, RET_W), lambda bi, ci, lg: (bi, ci, col))

    st_spec = pl.BlockSpec((bb, RET_HEADS, RET_DIM, RET_DIM), lambda bi, ci, lg: (bi, 0, 0, 0))
    grid_spec = pltpu.PrefetchScalarGridSpec(
        num_scalar_prefetch=1,
        grid=(b // bb, s // c),
        in_specs=[seq(q_col), seq(k_col), seq(v_col), seq(0), st_spec],
        out_specs=[seq(0), st_spec],
        scratch_shapes=[pltpu.VMEM((RET_HEADS, c, c), F32), pltpu.VMEM((RET_HEADS, c, RET_DIM), F32),
                        pltpu.VMEM((RET_HEADS, c, RET_DIM), F32)],
    )
    return pl.pallas_call(
        functools.partial(_ret_kernel, c=c, bb=bb),
        grid_spec=grid_spec,
        out_shape=[jax.ShapeDtypeStruct((b, s, RET_W), BF16),
                   jax.ShapeDtypeStruct((b, RET_HEADS, RET_DIM, RET_DIM), F32)],
        compiler_params=pltpu.CompilerParams(dimension_semantics=("arbitrary", "arbitrary")),
        name="retention",
    )(log_g, qk, qk, v, gate, state0)


def _mem_kernel(q_ref, k_ref, v_ref, o_ref):
    for h in range(MEM_HEADS):
        sl = slice(h * MEM_HEAD_DIM, (h + 1) * MEM_HEAD_DIM)
        qh = q_ref[0, :, sl]
        kh = k_ref[0, :, sl].astype(BF16)
        vh = v_ref[0, :, sl].astype(BF16)
        s = lax.dot_general(qh, kh, (((1,), (1,)), ((), ())), preferred_element_type=F32)
        m = jnp.max(s, axis=1, keepdims=True)
        p = jnp.exp(s - m)
        den = jnp.sum(p, axis=1, keepdims=True)
        o = jnp.dot(p.astype(BF16), vh, preferred_element_type=F32)
        o_ref[0, :, sl] = (o * (1.0 / den)).astype(o_ref.dtype)


def _mem_attend(q, q_col, mk, mv, *, s, tq):
    b = q.shape[0]
    kv_spec = pl.BlockSpec((1, N_MEM, MEM_W), lambda bi, i: (bi, 0, 0))
    return pl.pallas_call(
        _mem_kernel,
        grid=(b, s // tq),
        in_specs=[pl.BlockSpec((1, tq, MEM_W), lambda bi, i: (bi, i, q_col)), kv_spec, kv_spec],
        out_specs=pl.BlockSpec((1, tq, MEM_W), lambda bi, i: (bi, i, 0)),
        out_shape=jax.ShapeDtypeStruct((b, s, MEM_W), BF16),
        compiler_params=pltpu.CompilerParams(dimension_semantics=("parallel", "parallel")),
        name="mem_attend",
    )(q, mk, mv)


def _merge_kernel(u_ref, oa_ref, ob_ref, oc_ref, wga_ref, wgb_ref, wgc_ref, ba_ref, bb_ref, bc_ref,
                  wa_ref, wb_ref, wc_ref, o_ref):
    u = u_ref[...]
    acc = None
    for o_r, wg_ref, b_ref, w_ref in ((oa_ref, wga_ref, ba_ref, wa_ref), (ob_ref, wgb_ref, bb_ref, wb_ref),
                                      (oc_ref, wgc_ref, bc_ref, wc_ref)):
        gate = _sigmoid(jnp.dot(u, wg_ref[...], preferred_element_type=F32) + b_ref[...])
        term = gate * jnp.dot(o_r[...], w_ref[...], preferred_element_type=F32)
        acc = term if acc is None else acc + term
    o_ref[...] = acc.astype(o_ref.dtype)


def _merge(u, o_swa, o_ret, o_mem, w_gate, gate_col0, b_gate, w_br, *, tm, tn):
    n, d = u.shape
    kb = o_swa.shape[1]
    nj = D_MODEL // tn
    g0 = gate_col0 // tn
    o_spec = pl.BlockSpec((tm, kb), lambda i, j: (i, 0))

    def wg_spec(br):
        return pl.BlockSpec((d, tn), lambda i, j: (0, g0 + br * nj + j))

    def b_spec(br):
        return pl.BlockSpec((1, tn), lambda i, j: (0, br * nj + j))

    def w_spec(br):
        return pl.BlockSpec((kb, tn), lambda i, j: (br, j))

    bias = b_gate.reshape(1, -1).astype(F32)
    return pl.pallas_call(
        _merge_kernel,
        grid=(n // tm, nj),
        in_specs=[pl.BlockSpec((tm, d), lambda i, j: (i, 0)), o_spec, o_spec, o_spec,
                  wg_spec(0), wg_spec(1), wg_spec(2), b_spec(0), b_spec(1), b_spec(2),
                  w_spec(0), w_spec(1), w_spec(2)],
        out_specs=pl.BlockSpec((tm, tn), lambda i, j: (i, j)),
        out_shape=jax.ShapeDtypeStruct((n, D_MODEL), BF16),
        compiler_params=pltpu.CompilerParams(dimension_semantics=("parallel", "parallel")),
        name="merge",
    )(u, o_swa, o_ret, o_mem, w_gate, w_gate, w_gate, bias, bias, bias, w_br, w_br, w_br)


def _ffn_up_kernel(u_ref, wug_ref, wuv_ref, wc_ref, b_ref, cb_ref, act_ref, co_ref,
                   ag_sc, av_sc, cg_sc, cv_sc, *, tm, tf, nf, nseg):
    i = pl.program_id(1)
    fb = pl.program_id(2)
    seg_t = tm // nseg
    stride = seg_t + 8
    if nseg == 1:
        @pl.when((i == 0) & (fb == 0))
        def _():
            for f in range(nf):
                cg_sc[f, 6:8, :] = cb_ref[0, f]
                cv_sc[f, 6:8, :] = cb_ref[0, nf + f]

    u = u_ref[0]
    for a_sc, wu_ref, blk, carry_sc in ((ag_sc, wug_ref, fb, cg_sc), (av_sc, wuv_ref, nf + fb, cv_sc)):
        a = jnp.dot(u, wu_ref[...], preferred_element_type=F32)
        for s in range(nseg):
            a_sc[s * stride + 8:(s + 1) * stride, :] = a[s * seg_t:(s + 1) * seg_t]
            co_ref[0, s, blk] = a[(s + 1) * seg_t - 8:(s + 1) * seg_t]
        if nseg == 1:
            a_sc[6:8, :] = carry_sc[fb, 6:8, :]
            carry_sc[fb] = a[tm - 8:tm]
        else:
            for s in range(nseg):
                a_sc[s * stride + 6:s * stride + 8, :] = cb_ref[s, blk]

    def conv(a_sc, blk, r0, c0):
        tap = lambda k: a_sc[r0 - k:r0 - k + CONV_ROWS, c0:c0 + LANE]
        wc = lambda k: wc_ref[blk, k:k + 1, c0:c0 + LANE]
        return b_ref[blk, :, c0:c0 + LANE] + (wc(0) * tap(2) + wc(1) * tap(1) + wc(2) * tap(0))

    for c0 in range(0, tf, LANE):
        for s in range(nseg):
            for r in range(0, seg_t, CONV_ROWS):
                cg = conv(ag_sc, fb, s * stride + 8 + r, c0)
                cv = conv(av_sc, nf + fb, s * stride + 8 + r, c0)
                act_ref[0, s * seg_t + r:s * seg_t + r + CONV_ROWS, c0:c0 + LANE] = (
                    cg * _sigmoid(cg) * cv).astype(act_ref.dtype)


def _ffn_up(u, conv_buf, w_up, w_conv, b_conv, *, tm, tf, nseg):
    bt, rows, d = u.shape
    nf = D_FF // tf
    nt = rows // tm
    nb = bt * nseg
    wc_blocks = w_conv.reshape(3, 2 * nf, tf).transpose(1, 0, 2)
    b_blocks = b_conv.reshape(2 * nf, 1, tf)
    cb_blocks = conv_buf.reshape(nb, 2, 2 * nf, tf).transpose(0, 2, 1, 3)
    whole = lambda *shape: pl.BlockSpec(shape, lambda b, i, f: (0,) * len(shape))
    return pl.pallas_call(
        functools.partial(_ffn_up_kernel, tm=tm, tf=tf, nf=nf, nseg=nseg),
        grid=(bt, nt, nf),
        in_specs=[pl.BlockSpec((1, tm, d), lambda b, i, f: (b, i, 0)),
                  pl.BlockSpec((d, tf), lambda b, i, f: (0, f)),
                  pl.BlockSpec((d, tf), lambda b, i, f: (0, nf + f)),
                  whole(2 * nf, 3, tf),
                  whole(2 * nf, 1, tf),
                  pl.BlockSpec((nseg, 2 * nf, 2, tf), lambda b, i, f: (b, 0, 0, 0))],
        out_specs=[pl.BlockSpec((1, tm, tf), lambda b, i, f: (b, i, f)),
                   pl.BlockSpec((1, nseg, 2 * nf, 8, tf), lambda b, i, f: (b * nt + i, 0, 0, 0, 0))],
        out_shape=[jax.ShapeDtypeStruct((bt, rows, D_FF), BF16),
                   jax.ShapeDtypeStruct((bt * nt, nseg, 2 * nf, 8, tf), F32)],
        scratch_shapes=[pltpu.VMEM((tm + 8 * nseg, tf), F32), pltpu.VMEM((tm + 8 * nseg, tf), F32),
                        pltpu.VMEM((nf, 8, tf), F32), pltpu.VMEM((nf, 8, tf), F32)],
        compiler_params=pltpu.CompilerParams(dimension_semantics=("arbitrary", "arbitrary", "arbitrary")),
        name="ffn_up",
    )(u, w_up, w_up, wc_blocks, b_blocks, cb_blocks)


def _ffn_down_kernel(act_ref, wd_ref, h_ref, g_ref, y_ref, *, nd, td):
    j = pl.program_id(1)
    for jj in range(nd):
        @pl.when(j == jj)
        def _():
            y_ref[:, jj * td:(jj + 1) * td] = h_ref[...] + jnp.dot(act_ref[...], wd_ref[...],
                                                                   preferred_element_type=F32)

    @pl.when(j == nd - 1)
    def _():
        y_ref[...] = _rms(y_ref[...], g_ref[...])


def _ffn_down(act, w_down, h, g_final, *, tm, td):
    n, f = act.shape
    d = h.shape[1]
    nd = d // td
    return pl.pallas_call(
        functools.partial(_ffn_down_kernel, nd=nd, td=td),
        grid=(n // tm, nd),
        in_specs=[pl.BlockSpec((tm, f), lambda i, j: (i, 0)),
                  pl.BlockSpec((f, td), lambda i, j: (0, j)),
                  pl.BlockSpec((tm, td), lambda i, j: (i, j)),
                  pl.BlockSpec((1, d), lambda i, j: (0, 0))],
        out_specs=pl.BlockSpec((tm, d), lambda i, j: (i, 0)),
        out_shape=jax.ShapeDtypeStruct((n, d), F32),
        compiler_params=pltpu.CompilerParams(dimension_semantics=("parallel", "arbitrary")),
        name="ffn_down",
    )(act, w_down, h, g_final.reshape(1, d))


def _out_proj_kernel(x_ref, w_ref, res_ref, g_ref, h_ref, u_ref):
    h = res_ref[...] + jnp.dot(x_ref[...], w_ref[...], preferred_element_type=F32)
    h_ref[...] = h
    u_ref[...] = _rms(h, g_ref[...]).astype(u_ref.dtype)


def _out_proj(x, w, residual, g, *, tm):
    n, k = x.shape
    d = w.shape[1]
    row = lambda width: pl.BlockSpec((tm, width), lambda i: (i, 0))
    return pl.pallas_call(
        _out_proj_kernel,
        grid=(n // tm,),
        in_specs=[row(k), pl.BlockSpec((k, d), lambda i: (0, 0)), row(d), pl.BlockSpec((1, d), lambda i: (0, 0))],
        out_specs=[row(d), row(d)],
        out_shape=[jax.ShapeDtypeStruct((n, d), F32), jax.ShapeDtypeStruct((n, d), BF16)],
        compiler_params=pltpu.CompilerParams(dimension_semantics=("parallel",)),
        name="proj_out",
    )(x, w, residual, g.reshape(1, d))


def _rope_tables(pos0, s):
    half = RET_DIM // 2
    inv_freq = 1.0 / (RET_ROPE_BASE ** np.linspace(0.0, 1.0, half))
    ang = (pos0 + np.arange(s))[:, None] * inv_freq[None, :]
    cos, sin = np.cos(ang), np.sin(ang)
    return (jnp.asarray(np.concatenate([cos, cos], axis=1), F32),
            jnp.asarray(np.concatenate([-sin, sin], axis=1), F32))


def _prep_weights(g_mix, w_in, b_gate, sink, w_br, w_o, g_ffn, w_up, w_conv, b_conv, w_down):
    o = 0
    cuts = {}
    for name, width in (("qa", SWA_Q), ("ka", SWA_KV), ("va", SWA_KV), ("qr", RET_W), ("kr", RET_W),
                        ("vr", RET_W), ("gr", RET_W), ("qm", MEM_W), ("gl", N_BRANCH * D_MODEL)):
        cuts[name] = (o, o + width)
        o += width
    cols = lambda *names: jnp.concatenate([w_in[:, cuts[n][0]:cuts[n][1]] for n in names], axis=1).astype(BF16)
    ones = lambda n, v: jnp.full((n,), v, F32)
    return dict(
        g_mix=g_mix, b_gate=b_gate, sink=sink, g_ffn=g_ffn, w_conv=w_conv, b_conv=b_conv,
        w_plain=cols("qa", "vr", "qm"),
        s_plain=jnp.concatenate([ones(SWA_Q, SWA_HEAD_DIM ** -0.5), ones(RET_W, 1.0),
                                 ones(MEM_W, MEM_HEAD_DIM ** -0.5)]),
        w_kv=cols("ka", "va"),
        w_rot=cols("qr", "kr"),
        s_rot=jnp.concatenate([ones(RET_W, 1.0), ones(RET_W, RET_DIM ** -0.5)]),
        w_g=cols("gr"),
        w_in16=w_in.astype(BF16), gate_col0=cuts["gl"][0],
        w_br=w_br.astype(BF16), w_o=w_o.astype(BF16), w_up=w_up.astype(BF16), w_down=w_down.astype(BF16),
    )


def _run_group(x, pos0, mem_k, mem_v, swa_cache, ret_state, conv_buf, wts, g_final, log_g):
    b, s, d = x.shape
    n = b * s
    tm = min(ROW_TILE, n)
    x2 = x.reshape(n, d)
    plain, u = _norm_mm(x2, wts["g_mix"], wts["w_plain"], wts["s_plain"], tm=tm, tn=COL_TILE, name="proj_plain")
    kv = _mm(u, wts["w_kv"], epilogue="plain", out_dtype=F32, tm=tm, tn=2 * SWA_KV, name="proj_kv")
    cc, ss = _rope_tables(pos0, s)
    if s < tm:
        cc, ss = jnp.tile(cc, (tm // s, 1)), jnp.tile(ss, (tm // s, 1))
    rot = _mm(u, wts["w_rot"], epilogue="rotary", out_dtype=BF16, tm=tm, tn=COL_TILE,
              colvec=wts["s_rot"], tables=(cc, ss), name="proj_rot")
    sgate = _mm(u, wts["w_g"], epilogue="silu", out_dtype=BF16, tm=tm, tn=COL_TILE, name="proj_silu")

    plain3 = plain.reshape(b, s, -1)
    kv3 = kv.reshape(b, s, 2 * SWA_KV)
    if swa_cache is None:
        o_swa = _swa(plain3, 0, kv3, 0, kv3, 1, wts["sink"], sq=s, qt=min(SWA_ROWS, s), kv_off=0,
                     bb=min(SWA_SEQS, b))
        new_k = kv3[:, s - WINDOW:, :SWA_KV]
        new_v = kv3[:, s - WINDOW:, SWA_KV:]
    else:
        ck = swa_cache[0].reshape(b, -1, SWA_KV)
        cv = swa_cache[1].reshape(b, -1, SWA_KV)
        n_keep = ck.shape[1]
        k_all = jnp.concatenate([ck, kv3[:, :, :SWA_KV]], axis=1)
        v_all = jnp.concatenate([cv, kv3[:, :, SWA_KV:]], axis=1)
        o_swa = _swa(plain3, 0, k_all, 0, v_all, 0, wts["sink"], sq=s, qt=s, kv_off=n_keep, bb=min(SWA_SEQS, b))
        new_k = k_all[:, -n_keep:]
        new_v = v_all[:, -n_keep:]
    state0 = jnp.zeros((b, RET_HEADS, RET_DIM, RET_DIM), F32) if ret_state is None else ret_state
    o_ret, s_new = _retention(rot.reshape(b, s, -1), 0, 1, plain3, 1, sgate.reshape(b, s, -1), state0, log_g,
                              s=s, c=min(RET_ROWS, s), bb=min(RET_SEQS, b))
    o_mem = _mem_attend(plain3, 2, mem_k.reshape(b, N_MEM, MEM_W), mem_v.reshape(b, N_MEM, MEM_W),
                        s=s, tq=min(MEM_ROWS, s))
    merged = _merge(u, o_swa.reshape(n, -1), o_ret.reshape(n, -1), o_mem.reshape(n, -1), wts["w_in16"],
                    wts["gate_col0"], wts["b_gate"], wts["w_br"], tm=tm, tn=MERGE_COLS)
    h1, u2 = _out_proj(merged, wts["w_o"], x2, wts["g_ffn"], tm=min(OUT_PROJ_ROWS, n))
    if s >= ROW_TILE:
        act, co = _ffn_up(u2.reshape(b, s, d), conv_buf, wts["w_up"], wts["w_conv"], wts["b_conv"],
                          tm=ROW_TILE, tf=FF_TILE, nseg=1)
    else:
        act, co = _ffn_up(u2.reshape(1, n, d), conv_buf, wts["w_up"], wts["w_conv"], wts["b_conv"],
                          tm=n, tf=FF_TILE, nseg=b)
    y = _ffn_down(act.reshape(n, D_FF), wts["w_down"], h1, g_final, tm=tm, td=FF_TILE)
    nblk, tfb = co.shape[2], co.shape[4]
    last = co.reshape(b, -1, nblk, 8, tfb)[:, -1, :, 6:8, :]
    new_buf = last.transpose(0, 2, 1, 3).reshape(b, 2, nblk * tfb)
    return (y.reshape(b, s, d), new_k.reshape(b, -1, SWA_KV_HEADS, SWA_HEAD_DIM),
            new_v.reshape(b, -1, SWA_KV_HEADS, SWA_HEAD_DIM), s_new, new_buf)


def _memory_kv(mem, g_mem, w_mem_kv):
    b, m, d = mem.shape
    u = _rmsnorm(mem.reshape(b * m, d), g_mem, min(OUT_PROJ_ROWS, b * m))
    kv = _mm(u, w_mem_kv.astype(BF16), epilogue="plain", out_dtype=F32, tm=min(ROW_TILE, b * m), tn=COL_TILE,
             name="mem_kv")
    return (kv[:, :MEM_W].reshape(b, m, MEM_HEADS, MEM_HEAD_DIM), kv[:, MEM_W:].reshape(b, m, MEM_HEADS, MEM_HEAD_DIM))


def kernel(x_prompt, x_sample, mem_prompt, cache_swa_k, cache_swa_v, state_ret, state_ffn_conv, cache_mem_k, cache_mem_v, g_mix, w_in, b_gate, sink, w_br, w_o, g_mem, w_mem_kv, g_ffn, w_up, w_conv, b_conv, w_down, g_final):
    bp = x_prompt.shape[0]
    depth = w_in.shape[0]
    assert depth == 1, "single-layer problem: the final norm is fused into the layer's FFN"
    log_g = jnp.log1p(-jnp.exp2(-5.0 - jnp.arange(RET_HEADS, dtype=F32)))
    hp, hs = x_prompt, x_sample
    outs_p = [[] for _ in range(6)]
    outs_s = [[] for _ in range(4)]
    for l in range(depth):
        wts = _prep_weights(g_mix[l], w_in[l], b_gate[l], sink[l], w_br[l], w_o[l],
                            g_ffn[l], w_up[l], w_conv[l], b_conv[l], w_down[l])
        mk, mv = _memory_kv(mem_prompt, g_mem[l], w_mem_kv[l])
        zero_buf = jnp.zeros((bp, 2, 2 * D_FF), F32)
        hp, kp, vp, sp, cp = _run_group(hp, 0, mk, mv, None, None, zero_buf, wts, g_final, log_g)
        hs, ksn, vsn, ssn, csn = _run_group(hs, PAST_LEN, cache_mem_k[l], cache_mem_v[l],
                                            (cache_swa_k[l], cache_swa_v[l]), state_ret[l],
                                            state_ffn_conv[l], wts, g_final, log_g)
        for lst, val in zip(outs_p, (kp, vp, sp, cp, mk, mv)):
            lst.append(val)
        for lst, val in zip(outs_s, (ksn, vsn, ssn, csn)):
            lst.append(val)
    return (hp, hs, *[jnp.stack(v) for v in outs_p], *[jnp.stack(v) for v in outs_s])
```

```python
import functools

import jax
import jax.numpy as jnp
import numpy as np
from jax import lax
from jax.experimental import pallas as pl
from jax.experimental.pallas import tpu as pltpu

F32 = jnp.float32
BF16 = jnp.bfloat16

D_MODEL = 2048
CHUNK = 64
WINDOW = 128
SWA_HEADS = 16
SWA_KV_HEADS = 4
SWA_GROUP = SWA_HEADS // SWA_KV_HEADS
SWA_HEAD_DIM = 64
RET_HEADS = 8
RET_DIM = 128
RET_ROPE_BASE = 10000.0
N_MEM = 256
MEM_HEADS = 4
MEM_HEAD_DIM = 256
D_FF = 5632
N_BRANCH = 3
EPS = 1e-6
NEG = -1e30
PAST_LEN = 1024

SWA_Q = SWA_HEADS * SWA_HEAD_DIM
SWA_KV = SWA_KV_HEADS * SWA_HEAD_DIM
RET_W = RET_HEADS * RET_DIM
MEM_W = MEM_HEADS * MEM_HEAD_DIM
LANE = 128
CONV_ROWS = 64
ROW_TILE = 1024
COL_TILE = 1024
MERGE_COLS = 512
OUT_PROJ_ROWS = 512
FF_TILE = 512
SWA_ROWS = 256
SWA_SEQS = 4
RET_ROWS = 256
RET_SEQS = 4
MEM_ROWS = 1024
MEM_SEQS = 2


def _sigmoid(x):
    return 1.0 / (1.0 + jnp.exp(-x))


def _rms(x, g):
    return x * lax.rsqrt(jnp.mean(x * x, axis=-1, keepdims=True) + EPS) * g


def _rmsnorm_kernel(x_ref, g_ref, o_ref):
    o_ref[...] = _rms(x_ref[...], g_ref[...]).astype(o_ref.dtype)


def _rmsnorm(x, g, tm):
    n, d = x.shape
    return pl.pallas_call(
        _rmsnorm_kernel,
        grid=(n // tm,),
        in_specs=[pl.BlockSpec((tm, d), lambda i: (i, 0)),
                  pl.BlockSpec((1, d), lambda i: (0, 0))],
        out_specs=pl.BlockSpec((tm, d), lambda i: (i, 0)),
        out_shape=jax.ShapeDtypeStruct((n, d), BF16),
        name="rmsnorm",
    )(x, g.reshape(1, d))


def _mm_kernel(x_ref, w_ref, *rest, epilogue, tn):
    o_ref = rest[-1]
    acc = jnp.dot(x_ref[...], w_ref[...], preferred_element_type=F32)
    if epilogue == "plain":
        out = acc
    elif epilogue == "silu":
        out = acc * _sigmoid(acc)
    elif epilogue == "rotary":
        cc, ss = rest[1][...], rest[2][...]
        pieces = []
        for j in range(tn // LANE):
            xh = acc[:, j * LANE:(j + 1) * LANE]
            pieces.append(xh * cc + pltpu.roll(xh, LANE // 2, 1) * ss)
        out = jnp.concatenate(pieces, axis=1) * rest[0][...]
    else:
        raise ValueError(epilogue)
    o_ref[...] = out.astype(o_ref.dtype)


def _mm(x, w, *, epilogue, out_dtype, tm, tn, colvec=None, tables=None, name):
    n, k = x.shape
    ncols = w.shape[1]
    in_specs = [pl.BlockSpec((tm, k), lambda i, j: (i, 0)),
                pl.BlockSpec((k, tn), lambda i, j: (0, j))]
    args = [x, w]
    if colvec is not None:
        in_specs.append(pl.BlockSpec((1, tn), lambda i, j: (0, j)))
        args.append(colvec.reshape(1, ncols).astype(F32))
    if tables is not None:
        nblk = tables[0].shape[0] // tm
        for t in tables:
            in_specs.append(pl.BlockSpec((tm, LANE), lambda i, j: (i % nblk, 0)))
            args.append(t)
    return pl.pallas_call(
        functools.partial(_mm_kernel, epilogue=epilogue, tn=tn),
        grid=(n // tm, ncols // tn),
        in_specs=in_specs,
        out_specs=pl.BlockSpec((tm, tn), lambda i, j: (i, j)),
        out_shape=jax.ShapeDtypeStruct((n, ncols), out_dtype),
        compiler_params=pltpu.CompilerParams(dimension_semantics=("parallel", "parallel")),
        name=name,
    )(*args)


def _norm_mm_kernel(x_ref, g_ref, w_ref, cv_ref, o_ref, u_ref):
    @pl.when(pl.program_id(1) == 0)
    def _():
        u_ref[...] = _rms(x_ref[...], g_ref[...]).astype(u_ref.dtype)

    acc = jnp.dot(u_ref[...], w_ref[...], preferred_element_type=F32)
    o_ref[...] = (acc * cv_ref[...]).astype(o_ref.dtype)


def _norm_mm(x, g, w, colvec, *, tm, tn, name):
    n, k = x.shape
    ncols = w.shape[1]
    return pl.pallas_call(
        _norm_mm_kernel,
        grid=(n // tm, ncols // tn),
        in_specs=[pl.BlockSpec((tm, k), lambda i, j: (i, 0)),
                  pl.BlockSpec((1, k), lambda i, j: (0, 0)),
                  pl.BlockSpec((k, tn), lambda i, j: (0, j)),
                  pl.BlockSpec((1, tn), lambda i, j: (0, j))],
        out_specs=[pl.BlockSpec((tm, tn), lambda i, j: (i, j)),
                   pl.BlockSpec((tm, k), lambda i, j: (i, 0))],
        out_shape=[jax.ShapeDtypeStruct((n, ncols), BF16), jax.ShapeDtypeStruct((n, k), BF16)],
        compiler_params=pltpu.CompilerParams(dimension_semantics=("parallel", "arbitrary")),
        name=name,
    )(x, g.reshape(1, k), w, colvec.reshape(1, ncols).astype(F32))


def _swa_kernel(sink_ref, q_ref, kp_ref, kc_ref, vp_ref, vc_ref, o_ref, *scratch, qt, kv_off, bb):
    n_sc = bb * SWA_KV_HEADS
    for bl in range(bb):
        _swa_one(sink_ref, q_ref.at[bl], kp_ref.at[bl], kc_ref.at[bl], vp_ref.at[bl], vc_ref.at[bl], o_ref.at[bl],
                 scratch[bl * SWA_KV_HEADS:(bl + 1) * SWA_KV_HEADS],
                 scratch[n_sc + bl * SWA_KV_HEADS:n_sc + (bl + 1) * SWA_KV_HEADS], qt=qt, kv_off=kv_off)


def _swa_one(sink_ref, q_ref, kp_ref, kc_ref, vp_ref, vc_ref, o_ref, s_scs, p_scs, *, qt, kv_off):
    i = pl.program_id(1)
    nq = qt // CHUNK
    w = (nq + 2) * CHUNK
    wp = -(-w // LANE) * LANE
    first_ok = jnp.where(i * qt + kv_off >= WINDOW, 0, 2)

    def window(p_ref, c_ref):
        parts = [p_ref[...], c_ref[...]]
        if wp > w:
            parts.append(jnp.zeros((wp - w, SWA_KV), F32))
        return jnp.concatenate(parts, axis=0)

    kwin = window(kp_ref, kc_ref)
    vwin = window(vp_ref, vc_ref)
    lane = lax.broadcasted_iota(jnp.int32, (wp, LANE), 1)
    lane_g = lax.broadcasted_iota(jnp.int32, (wp, SWA_KV), 1) // SWA_HEAD_DIM
    out_g = lax.broadcasted_iota(jnp.int32, (CHUNK, SWA_KV), 1) // SWA_HEAD_DIM

    def block_diag(win, h):
        col, half = divmod(h, 2)
        x = win[:, col * LANE:(col + 1) * LANE]
        xr = pltpu.roll(x, SWA_HEAD_DIM, 1)
        lo = lane < SWA_HEAD_DIM
        r = jnp.where(lo, x, xr) if half == 0 else jnp.where(lo, xr, x)
        r2 = jnp.concatenate([r, r], axis=1).astype(BF16)
        zero = jnp.zeros_like(r2)
        return jnp.concatenate([jnp.where(lane_g == g, r2, zero) for g in range(SWA_GROUP)], axis=0)

    for h in range(SWA_KV_HEADS):
        bk = block_diag(kwin, h)
        bv = block_diag(vwin, h)
        qh = q_ref[:, h * SWA_KV:(h + 1) * SWA_KV]
        s_sc, p_sc = s_scs[h], p_scs[h]
        s_sc[...] = lax.dot_general(qh, bk, (((1,), (1,)), ((), ())), preferred_element_type=F32)
        invs = []
        for jq in range(nq):
            rows = slice(jq * CHUNK, (jq + 1) * CHUNK)
            c_lo, c_hi = jq * CHUNK // LANE, ((jq + 3) * CHUNK - 1) // LANE
            width = (c_hi - c_lo + 1) * LANE
            kc = c_lo * (LANE // CHUNK) + lax.broadcasted_iota(jnp.int32, (CHUNK, width), 1) // CHUNK
            visible = (kc >= jq) & (kc <= jq + 2) & (kc >= first_ok)
            inv = jnp.zeros((CHUNK, SWA_KV), F32)
            for g in range(SWA_GROUP):
                l0 = g * wp + c_lo * LANE
                sg = jnp.where(visible, s_sc[rows, l0:l0 + width], NEG)
                sk = sink_ref[h * SWA_GROUP + g]
                m = jnp.maximum(jnp.max(sg, axis=1, keepdims=True), sk)
                p = jnp.exp(sg - m)
                den = jnp.sum(p, axis=1, keepdims=True) + jnp.exp(sk - m)
                p_sc[rows, l0:l0 + width] = p.astype(BF16)
                for c in range(wp // LANE):
                    if not c_lo <= c <= c_hi:
                        p_sc[rows, g * wp + c * LANE:g * wp + (c + 1) * LANE] = jnp.zeros((CHUNK, LANE), BF16)
                inv = jnp.where(out_g == g, 1.0 / den, inv)
            invs.append(inv)
        o = jnp.dot(p_sc[...], bv, preferred_element_type=F32)
        o_ref[:, h * SWA_KV:(h + 1) * SWA_KV] = (o * jnp.concatenate(invs, axis=0)).astype(o_ref.dtype)


def _swa(q, q_col, k, k_col, v, v_col, sink, *, sq, qt, kv_off, bb):
    b = q.shape[0]
    cur_off = kv_off // qt
    prev_off = kv_off // WINDOW - 1
    per = qt // WINDOW if qt >= WINDOW else 0
    wp = -(-(qt + WINDOW) // LANE) * LANE

    def prev_map(kcol):
        return lambda bi, i, s: (bi, jnp.maximum(i * per + prev_off, 0), kcol)

    def cur_map(kcol):
        return lambda bi, i, s: (bi, i + cur_off, kcol)

    grid_spec = pltpu.PrefetchScalarGridSpec(
        num_scalar_prefetch=1,
        grid=(b // bb, sq // qt),
        in_specs=[pl.BlockSpec((bb, qt, SWA_Q), lambda bi, i, s: (bi, i, q_col)),
                  pl.BlockSpec((bb, WINDOW, SWA_KV), prev_map(k_col)),
                  pl.BlockSpec((bb, qt, SWA_KV), cur_map(k_col)),
                  pl.BlockSpec((bb, WINDOW, SWA_KV), prev_map(v_col)),
                  pl.BlockSpec((bb, qt, SWA_KV), cur_map(v_col))],
        out_specs=pl.BlockSpec((bb, qt, SWA_Q), lambda bi, i, s: (bi, i, 0)),
        scratch_shapes=([pltpu.VMEM((qt, SWA_GROUP * wp), F32)] * (bb * SWA_KV_HEADS)
                        + [pltpu.VMEM((qt, SWA_GROUP * wp), BF16)] * (bb * SWA_KV_HEADS)),
    )
    return pl.pallas_call(
        functools.partial(_swa_kernel, qt=qt, kv_off=kv_off, bb=bb),
        grid_spec=grid_spec,
        out_shape=jax.ShapeDtypeStruct((b, sq, SWA_Q), BF16),
        compiler_params=pltpu.CompilerParams(dimension_semantics=("parallel", "parallel")),
        name="swa",
    )(sink.astype(F32), q, k, k, v, v)


def _ret_kernel(lg_ref, q_ref, k_ref, v_ref, g_ref, s0_ref, o_ref, st_ref, dec_ref, xi_ref, zeta_ref, *, c, bb):
    bi = pl.program_id(0)
    ci = pl.program_id(1)

    @pl.when((bi == 0) & (ci == 0))
    def _():
        n = lax.broadcasted_iota(jnp.int32, (c, c), 0)
        m = lax.broadcasted_iota(jnp.int32, (c, c), 1)
        diff = (n - m).astype(F32)
        nrow = lax.broadcasted_iota(jnp.int32, (c, RET_DIM), 0).astype(F32)
        for h in range(RET_HEADS):
            dec_ref[h] = jnp.where(diff >= 0, jnp.exp(lg_ref[h] * jnp.maximum(diff, 0.0)), 0.0)
            xi_ref[h] = jnp.exp(lg_ref[h] * (nrow + 1.0))
            zeta_ref[h] = jnp.exp(lg_ref[h] * (float(c) - 1.0 - nrow))

    @pl.when(ci == 0)
    def _():
        st_ref[...] = s0_ref[...]

    for bl in range(bb):
        for h in range(RET_HEADS):
            lg = lg_ref[h]
            sl = slice(h * RET_DIM, (h + 1) * RET_DIM)
            qh = q_ref[bl, :, sl]
            kh = k_ref[bl, :, sl]
            vh = v_ref[bl, :, sl]
            state = st_ref[bl, h]
            inner = lax.dot_general(qh, kh, (((1,), (1,)), ((), ())), preferred_element_type=F32) * dec_ref[h]
            o = jnp.dot(inner.astype(BF16), vh, preferred_element_type=F32)
            cross = jnp.dot(qh, state.astype(BF16), preferred_element_type=F32)
            o = o + cross * xi_ref[h]
            kz = (kh.astype(F32) * zeta_ref[h]).astype(BF16)
            upd = lax.dot_general(kz, vh, (((0,), (0,)), ((), ())), preferred_element_type=F32)
            decay_c = jnp.exp(lg * jnp.full((1, RET_DIM), float(c), F32))
            st_ref[bl, h] = decay_c * state + upd
            on = o * lax.rsqrt(jnp.mean(o * o, axis=-1, keepdims=True) + EPS)
            o_ref[bl, :, sl] = (g_ref[bl, :, sl].astype(F32) * on).astype(o_ref.dtype)


def _retention(qk, q_col, k_col, v, v_col, gate, state0, log_g, *, s, c, bb):
    b = qk.shape[0]

    def seq(col):
        return pl.BlockSpec((bb, c, RET_W), lambda bi, ci, lg: (bi, ci, col))

    st_spec = pl.BlockSpec((bb, RET_HEADS, RET_DIM, RET_DIM), lambda bi, ci, lg: (bi, 0, 0, 0))
    grid_spec = pltpu.PrefetchScalarGridSpec(
        num_scalar_prefetch=1,
        grid=(b // bb, s // c),
        in_specs=[seq(q_col), seq(k_col), seq(v_col), seq(0), st_spec],
        out_specs=[seq(0), st_spec],
        scratch_shapes=[pltpu.VMEM((RET_HEADS, c, c), F32), pltpu.VMEM((RET_HEADS, c, RET_DIM), F32),
                        pltpu.VMEM((RET_HEADS, c, RET_DIM), F32)],
    )
    return pl.pallas_call(
        functools.partial(_ret_kernel, c=c, bb=bb),
        grid_spec=grid_spec,
        out_shape=[jax.ShapeDtypeStruct((b, s, RET_W), BF16),
                   jax.ShapeDtypeStruct((b, RET_HEADS, RET_DIM, RET_DIM), F32)],
        compiler_params=pltpu.CompilerParams(dimension_semantics=("arbitrary", "arbitrary")),
        name="retention",
    )(log_g, qk, qk, v, gate, state0)


def _mem_kernel(q_ref, k_ref, v_ref, o_ref, *, bb):
    for bl in range(bb):
        for h in range(MEM_HEADS):
            sl = slice(h * MEM_HEAD_DIM, (h + 1) * MEM_HEAD_DIM)
            qh = q_ref[bl, :, sl]
            kh = k_ref[bl, :, sl].astype(BF16)
            vh = v_ref[bl, :, sl].astype(BF16)
            s = lax.dot_general(qh, kh, (((1,), (1,)), ((), ())), preferred_element_type=F32)
            m = jnp.max(s, axis=1, keepdims=True)
            p = jnp.exp(s - m)
            den = jnp.sum(p, axis=1, keepdims=True)
            o = jnp.dot(p.astype(BF16), vh, preferred_element_type=F32)
            o_ref[bl, :, sl] = (o * (1.0 / den)).astype(o_ref.dtype)


def _mem_attend(q, q_col, mk, mv, *, s, tq, bb):
    b = q.shape[0]
    kv_spec = pl.BlockSpec((bb, N_MEM, MEM_W), lambda bi, i: (bi, 0, 0))
    return pl.pallas_call(
        functools.partial(_mem_kernel, bb=bb),
        grid=(b // bb, s // tq),
        in_specs=[pl.BlockSpec((bb, tq, MEM_W), lambda bi, i: (bi, i, q_col)), kv_spec, kv_spec],
        out_specs=pl.BlockSpec((bb, tq, MEM_W), lambda bi, i: (bi, i, 0)),
        out_shape=jax.ShapeDtypeStruct((b, s, MEM_W), BF16),
        compiler_params=pltpu.CompilerParams(dimension_semantics=("parallel", "parallel")),
        name="mem_attend",
    )(q, mk, mv)


def _merge_kernel(u_ref, oa_ref, ob_ref, oc_ref, wga_ref, wgb_ref, wgc_ref, ba_ref, bb_ref, bc_ref,
                  wa_ref, wb_ref, wc_ref, o_ref):
    u = u_ref[...]
    acc = None
    for o_r, wg_ref, b_ref, w_ref in ((oa_ref, wga_ref, ba_ref, wa_ref), (ob_ref, wgb_ref, bb_ref, wb_ref),
                                      (oc_ref, wgc_ref, bc_ref, wc_ref)):
        gate = _sigmoid(jnp.dot(u, wg_ref[...], preferred_element_type=F32) + b_ref[...])
        term = gate * jnp.dot(o_r[...], w_ref[...], preferred_element_type=F32)
        acc = term if acc is None else acc + term
    o_ref[...] = acc.astype(o_ref.dtype)


def _merge(u, o_swa, o_ret, o_mem, w_gate, gate_col0, b_gate, w_br, *, tm, tn):
    n, d = u.shape
    kb = o_swa.shape[1]
    nj = D_MODEL // tn
    g0 = gate_col0 // tn
    o_spec = pl.BlockSpec((tm, kb), lambda i, j: (i, 0))

    def wg_spec(br):
        return pl.BlockSpec((d, tn), lambda i, j: (0, g0 + br * nj + j))

    def b_spec(br):
        return pl.BlockSpec((1, tn), lambda i, j: (0, br * nj + j))

    def w_spec(br):
        return pl.BlockSpec((kb, tn), lambda i, j: (br, j))

    bias = b_gate.reshape(1, -1).astype(F32)
    return pl.pallas_call(
        _merge_kernel,
        grid=(n // tm, nj),
        in_specs=[pl.BlockSpec((tm, d), lambda i, j: (i, 0)), o_spec, o_spec, o_spec,
                  wg_spec(0), wg_spec(1), wg_spec(2), b_spec(0), b_spec(1), b_spec(2),
                  w_spec(0), w_spec(1), w_spec(2)],
        out_specs=pl.BlockSpec((tm, tn), lambda i, j: (i, j)),
        out_shape=jax.ShapeDtypeStruct((n, D_MODEL), BF16),
        compiler_params=pltpu.CompilerParams(dimension_semantics=("parallel", "parallel")),
        name="merge",
    )(u, o_swa, o_ret, o_mem, w_gate, w_gate, w_gate, bias, bias, bias, w_br, w_br, w_br)


def _ffn_up_kernel(u_ref, wug_ref, wuv_ref, wc_ref, b_ref, cb_ref, act_ref, co_ref,
                   ag_sc, av_sc, cg_sc, cv_sc, *, tm, tf, nf, nseg):
    i = pl.program_id(1)
    fb = pl.program_id(2)
    seg_t = tm // nseg
    stride = seg_t + 8
    if nseg == 1:
        @pl.when((i == 0) & (fb == 0))
        def _():
            for f in range(nf):
                cg_sc[f, 6:8, :] = cb_ref[0, f]
                cv_sc[f, 6:8, :] = cb_ref[0, nf + f]

    u = u_ref[0]
    for a_sc, wu_ref, blk, carry_sc in ((ag_sc, wug_ref, fb, cg_sc), (av_sc, wuv_ref, nf + fb, cv_sc)):
        a = jnp.dot(u, wu_ref[...], preferred_element_type=F32)
        for s in range(nseg):
            a_sc[s * stride + 8:(s + 1) * stride, :] = a[s * seg_t:(s + 1) * seg_t]
            co_ref[0, s, blk] = a[(s + 1) * seg_t - 8:(s + 1) * seg_t]
        if nseg == 1:
            a_sc[6:8, :] = carry_sc[fb, 6:8, :]
            carry_sc[fb] = a[tm - 8:tm]
        else:
            for s in range(nseg):
                a_sc[s * stride + 6:s * stride + 8, :] = cb_ref[s, blk]

    def conv(a_sc, blk, r0, c0):
        tap = lambda k: a_sc[r0 - k:r0 - k + CONV_ROWS, c0:c0 + LANE]
        wc = lambda k: wc_ref[blk, k:k + 1, c0:c0 + LANE]
        return b_ref[blk, :, c0:c0 + LANE] + (wc(0) * tap(2) + wc(1) * tap(1) + wc(2) * tap(0))

    for c0 in range(0, tf, LANE):
        for s in range(nseg):
            for r in range(0, seg_t, CONV_ROWS):
                cg = conv(ag_sc, fb, s * stride + 8 + r, c0)
                cv = conv(av_sc, nf + fb, s * stride + 8 + r, c0)
                act_ref[0, s * seg_t + r:s * seg_t + r + CONV_ROWS, c0:c0 + LANE] = (
                    cg * _sigmoid(cg) * cv).astype(act_ref.dtype)


def _ffn_up(u, conv_buf, w_up, w_conv, b_conv, *, tm, tf, nseg):
    bt, rows, d = u.shape
    nf = D_FF // tf
    nt = rows // tm
    nb = bt * nseg
    wc_blocks = w_conv.reshape(3, 2 * nf, tf).transpose(1, 0, 2)
    b_blocks = b_conv.reshape(2 * nf, 1, tf)
    cb_blocks = conv_buf.reshape(nb, 2, 2 * nf, tf).transpose(0, 2, 1, 3)
    whole = lambda *shape: pl.BlockSpec(shape, lambda b, i, f: (0,) * len(shape))
    return pl.pallas_call(
        functools.partial(_ffn_up_kernel, tm=tm, tf=tf, nf=nf, nseg=nseg),
        grid=(bt, nt, nf),
        in_specs=[pl.BlockSpec((1, tm, d), lambda b, i, f: (b, i, 0)),
                  pl.BlockSpec((d, tf), lambda b, i, f: (0, f)),
                  pl.BlockSpec((d, tf), lambda b, i, f: (0, nf + f)),
                  whole(2 * nf, 3, tf),
                  whole(2 * nf, 1, tf),
                  pl.BlockSpec((nseg, 2 * nf, 2, tf), lambda b, i, f: (b, 0, 0, 0))],
        out_specs=[pl.BlockSpec((1, tm, tf), lambda b, i, f: (b, i, f)),
                   pl.BlockSpec((1, nseg, 2 * nf, 8, tf), lambda b, i, f: (b * nt + i, 0, 0, 0, 0))],
        out_shape=[jax.ShapeDtypeStruct((bt, rows, D_FF), BF16),
                   jax.ShapeDtypeStruct((bt * nt, nseg, 2 * nf, 8, tf), F32)],
        scratch_shapes=[pltpu.VMEM((tm + 8 * nseg, tf), F32), pltpu.VMEM((tm + 8 * nseg, tf), F32),
                        pltpu.VMEM((nf, 8, tf), F32), pltpu.VMEM((nf, 8, tf), F32)],
        compiler_params=pltpu.CompilerParams(dimension_semantics=("arbitrary", "arbitrary", "arbitrary")),
        name="ffn_up",
    )(u, w_up, w_up, wc_blocks, b_blocks, cb_blocks)


def _ffn_down_kernel(act_ref, wd_ref, h_ref, g_ref, y_ref, *, nd, td):
    j = pl.program_id(1)
    for jj in range(nd):
        @pl.when(j == jj)
        def _():
            y_ref[:, jj * td:(jj + 1) * td] = h_ref[...] + jnp.dot(act_ref[...], wd_ref[...],
                                                                   preferred_element_type=F32)

    @pl.when(j == nd - 1)
    def _():
        y_ref[...] = _rms(y_ref[...], g_ref[...])


def _ffn_down(act, w_down, h, g_final, *, tm, td):
    n, f = act.shape
    d = h.shape[1]
    nd = d // td
    return pl.pallas_call(
        functools.partial(_ffn_down_kernel, nd=nd, td=td),
        grid=(n // tm, nd),
        in_specs=[pl.BlockSpec((tm, f), lambda i, j: (i, 0)),
                  pl.BlockSpec((f, td), lambda i, j: (0, j)),
                  pl.BlockSpec((tm, td), lambda i, j: (i, j)),
                  pl.BlockSpec((1, d), lambda i, j: (0, 0))],
        out_specs=pl.BlockSpec((tm, d), lambda i, j: (i, 0)),
        out_shape=jax.ShapeDtypeStruct((n, d), F32),
        compiler_params=pltpu.CompilerParams(dimension_semantics=("parallel", "arbitrary")),
        name="ffn_down",
    )(act, w_down, h, g_final.reshape(1, d))


def _out_proj_kernel(x_ref, w_ref, res_ref, g_ref, h_ref, u_ref):
    h = res_ref[...] + jnp.dot(x_ref[...], w_ref[...], preferred_element_type=F32)
    h_ref[...] = h
    u_ref[...] = _rms(h, g_ref[...]).astype(u_ref.dtype)


def _out_proj(x, w, residual, g, *, tm):
    n, k = x.shape
    d = w.shape[1]
    row = lambda width: pl.BlockSpec((tm, width), lambda i: (i, 0))
    return pl.pallas_call(
        _out_proj_kernel,
        grid=(n // tm,),
        in_specs=[row(k), pl.BlockSpec((k, d), lambda i: (0, 0)), row(d), pl.BlockSpec((1, d), lambda i: (0, 0))],
        out_specs=[row(d), row(d)],
        out_shape=[jax.ShapeDtypeStruct((n, d), F32), jax.ShapeDtypeStruct((n, d), BF16)],
        compiler_params=pltpu.CompilerParams(dimension_semantics=("parallel",)),
        name="proj_out",
    )(x, w, residual, g.reshape(1, d))


def _rope_tables(pos0, s):
    half = RET_DIM // 2
    inv_freq = 1.0 / (RET_ROPE_BASE ** np.linspace(0.0, 1.0, half))
    ang = (pos0 + np.arange(s))[:, None] * inv_freq[None, :]
    cos, sin = np.cos(ang), np.sin(ang)
    return (jnp.asarray(np.concatenate([cos, cos], axis=1), F32),
            jnp.asarray(np.concatenate([-sin, sin], axis=1), F32))


def _prep_weights(g_mix, w_in, b_gate, sink, w_br, w_o, g_ffn, w_up, w_conv, b_conv, w_down):
    o = 0
    cuts = {}
    for name, width in (("qa", SWA_Q), ("ka", SWA_KV), ("va", SWA_KV), ("qr", RET_W), ("kr", RET_W),
                        ("vr", RET_W), ("gr", RET_W), ("qm", MEM_W), ("gl", N_BRANCH * D_MODEL)):
        cuts[name] = (o, o + width)
        o += width
    cols = lambda *names: jnp.concatenate([w_in[:, cuts[n][0]:cuts[n][1]] for n in names], axis=1).astype(BF16)
    ones = lambda n, v: jnp.full((n,), v, F32)
    return dict(
        g_mix=g_mix, b_gate=b_gate, sink=sink, g_ffn=g_ffn, w_conv=w_conv, b_conv=b_conv,
        w_plain=cols("qa", "vr", "qm"),
        s_plain=jnp.concatenate([ones(SWA_Q, SWA_HEAD_DIM ** -0.5), ones(RET_W, 1.0),
                                 ones(MEM_W, MEM_HEAD_DIM ** -0.5)]),
        w_kv=cols("ka", "va"),
        w_rot=cols("qr", "kr"),
        s_rot=jnp.concatenate([ones(RET_W, 1.0), ones(RET_W, RET_DIM ** -0.5)]),
        w_g=cols("gr"),
        w_in16=w_in.astype(BF16), gate_col0=cuts["gl"][0],
        w_br=w_br.astype(BF16), w_o=w_o.astype(BF16), w_up=w_up.astype(BF16), w_down=w_down.astype(BF16),
    )


def _run_group(x, pos0, mem_k, mem_v, swa_cache, ret_state, conv_buf, wts, g_final, log_g):
    b, s, d = x.shape
    n = b * s
    tm = min(ROW_TILE, n)
    x2 = x.reshape(n, d)
    plain, u = _norm_mm(x2, wts["g_mix"], wts["w_plain"], wts["s_plain"], tm=tm, tn=COL_TILE, name="proj_plain")
    kv = _mm(u, wts["w_kv"], epilogue="plain", out_dtype=F32, tm=tm, tn=2 * SWA_KV, name="proj_kv")
    cc, ss = _rope_tables(pos0, s)
    if s < tm:
        cc, ss = jnp.tile(cc, (tm // s, 1)), jnp.tile(ss, (tm // s, 1))
    rot = _mm(u, wts["w_rot"], epilogue="rotary", out_dtype=BF16, tm=tm, tn=COL_TILE,
              colvec=wts["s_rot"], tables=(cc, ss), name="proj_rot")
    sgate = _mm(u, wts["w_g"], epilogue="silu", out_dtype=BF16, tm=tm, tn=COL_TILE, name="proj_silu")

    plain3 = plain.reshape(b, s, -1)
    kv3 = kv.reshape(b, s, 2 * SWA_KV)
    if swa_cache is None:
        o_swa = _swa(plain3, 0, kv3, 0, kv3, 1, wts["sink"], sq=s, qt=min(SWA_ROWS, s), kv_off=0,
                     bb=min(SWA_SEQS, b))
        new_k = kv3[:, s - WINDOW:, :SWA_KV]
        new_v = kv3[:, s - WINDOW:, SWA_KV:]
    else:
        ck = swa_cache[0].reshape(b, -1, SWA_KV)
        cv = swa_cache[1].reshape(b, -1, SWA_KV)
        n_keep = ck.shape[1]
        k_all = jnp.concatenate([ck, kv3[:, :, :SWA_KV]], axis=1)
        v_all = jnp.concatenate([cv, kv3[:, :, SWA_KV:]], axis=1)
        o_swa = _swa(plain3, 0, k_all, 0, v_all, 0, wts["sink"], sq=s, qt=s, kv_off=n_keep, bb=min(SWA_SEQS, b))
        new_k = k_all[:, -n_keep:]
        new_v = v_all[:, -n_keep:]
    state0 = jnp.zeros((b, RET_HEADS, RET_DIM, RET_DIM), F32) if ret_state is None else ret_state
    o_ret, s_new = _retention(rot.reshape(b, s, -1), 0, 1, plain3, 1, sgate.reshape(b, s, -1), state0, log_g,
                              s=s, c=min(RET_ROWS, s), bb=min(RET_SEQS, b))
    o_mem = _mem_attend(plain3, 2, mem_k.reshape(b, N_MEM, MEM_W), mem_v.reshape(b, N_MEM, MEM_W),
                        s=s, tq=min(MEM_ROWS, s), bb=min(MEM_SEQS, b))
    merged = _merge(u, o_swa.reshape(n, -1), o_ret.reshape(n, -1), o_mem.reshape(n, -1), wts["w_in16"],
                    wts["gate_col0"], wts["b_gate"], wts["w_br"], tm=tm, tn=MERGE_COLS)
    h1, u2 = _out_proj(merged, wts["w_o"], x2, wts["g_ffn"], tm=min(OUT_PROJ_ROWS, n))
    if s >= ROW_TILE:
        act, co = _ffn_up(u2.reshape(b, s, d), conv_buf, wts["w_up"], wts["w_conv"], wts["b_conv"],
                          tm=ROW_TILE, tf=FF_TILE, nseg=1)
    else:
        act, co = _ffn_up(u2.reshape(1, n, d), conv_buf, wts["w_up"], wts["w_conv"], wts["b_conv"],
                          tm=n, tf=FF_TILE, nseg=b)
    y = _ffn_down(act.reshape(n, D_FF), wts["w_down"], h1, g_final, tm=tm, td=FF_TILE)
    nblk, tfb = co.shape[2], co.shape[4]
    last = co.reshape(b, -1, nblk, 8, tfb)[:, -1, :, 6:8, :]
    new_buf = last.transpose(0, 2, 1, 3).reshape(b, 2, nblk * tfb)
    return (y.reshape(b, s, d), new_k.reshape(b, -1, SWA_KV_HEADS, SWA_HEAD_DIM),
            new_v.reshape(b, -1, SWA_KV_HEADS, SWA_HEAD_DIM), s_new, new_buf)


def _memory_kv(mem, g_mem, w_mem_kv):
    b, m, d = mem.shape
    u = _rmsnorm(mem.reshape(b * m, d), g_mem, min(OUT_PROJ_ROWS, b * m))
    kv = _mm(u, w_mem_kv.astype(BF16), epilogue="plain", out_dtype=F32, tm=min(ROW_TILE, b * m), tn=COL_TILE,
             name="mem_kv")
    return (kv[:, :MEM_W].reshape(b, m, MEM_HEADS, MEM_HEAD_DIM), kv[:, MEM_W:].reshape(b, m, MEM_HEADS, MEM_HEAD_DIM))


def kernel(x_prompt, x_sample, mem_prompt, cache_swa_k, cache_swa_v, state_ret, state_ffn_conv, cache_mem_k, cache_mem_v, g_mix, w_in, b_gate, sink, w_br, w_o, g_mem, w_mem_kv, g_ffn, w_up, w_conv, b_conv, w_down, g_final):
    bp = x_prompt.shape[0]
    depth = w_in.shape[0]
    assert depth == 1, "single-layer problem: the final norm is fused into the layer's FFN"
    log_g = jnp.log1p(-jnp.exp2(-5.0 - jnp.arange(RET_HEADS, dtype=F32)))
    hp, hs = x_prompt, x_sample
    outs_p = [[] for _ in range(6)]
    outs_s = [[] for _ in range(4)]
    for l in range(depth):
        wts = _prep_weights(g_mix[l], w_in[l], b_gate[l], sink[l], w_br[l], w_o[l],
                            g_ffn[l], w_up[l], w_conv[l], b_conv[l], w_down[l])
        mk, mv = _memory_kv(mem_prompt, g_mem[l], w_mem_kv[l])
        zero_buf = jnp.zeros((bp, 2, 2 * D_FF), F32)
        hp, kp, vp, sp, cp = _run_group(hp, 0, mk, mv, None, None, zero_buf, wts, g_final, log_g)
        hs, ksn, vsn, ssn, csn = _run_group(hs, PAST_LEN, cache_mem_k[l], cache_mem_v[l],
                                            (cache_swa_k[l], cache_swa_v[l]), state_ret[l],
                                            state_ffn_conv[l], wts, g_final, log_g)
        for lst, val in zip(outs_p, (kp, vp, sp, cp, mk, mv)):
            lst.append(val)
        for lst, val in zip(outs_s, (ksn, vsn, ssn, csn)):
            lst.append(val)
    return (hp, hs, *[jnp.stack(v) for v in outs_p], *[jnp.stack(v) for v in outs_s])
```

```python
import functools

import jax
import jax.numpy as jnp
import numpy as np
from jax import lax
from jax.experimental import pallas as pl
from jax.experimental.pallas import tpu as pltpu

F32 = jnp.float32
BF16 = jnp.bfloat16

D_MODEL = 2048
CHUNK = 64
WINDOW = 128
SWA_HEADS = 16
SWA_KV_HEADS = 4
SWA_GROUP = SWA_HEADS // SWA_KV_HEADS
SWA_HEAD_DIM = 64
RET_HEADS = 8
RET_DIM = 128
RET_ROPE_BASE = 10000.0
N_MEM = 256
MEM_HEADS = 4
MEM_HEAD_DIM = 256
D_FF = 5632
N_BRANCH = 3
EPS = 1e-6
NEG = -1e30
PAST_LEN = 1024

SWA_Q = SWA_HEADS * SWA_HEAD_DIM
SWA_KV = SWA_KV_HEADS * SWA_HEAD_DIM
RET_W = RET_HEADS * RET_DIM
MEM_W = MEM_HEADS * MEM_HEAD_DIM
LANE = 128
CONV_ROWS = 64
ROW_TILE = 1024
COL_TILE = 1024
MERGE_COLS = 512
OUT_PROJ_ROWS = 512
FF_TILE = 512
SWA_ROWS = 256
SWA_SEQS = 4
RET_ROWS = 256
RET_SEQS = 4
MEM_ROWS = 1024
MEM_SEQS = 2


def _sigmoid(x):
    return 1.0 / (1.0 + jnp.exp(-x))


def _rms(x, g):
    return x * lax.rsqrt(jnp.mean(x * x, axis=-1, keepdims=True) + EPS) * g


def _rmsnorm_kernel(x_ref, g_ref, o_ref):
    o_ref[...] = _rms(x_ref[...], g_ref[...]).astype(o_ref.dtype)


def _rmsnorm(x, g, tm):
    n, d = x.shape
    return pl.pallas_call(
        _rmsnorm_kernel,
        grid=(n // tm,),
        in_specs=[pl.BlockSpec((tm, d), lambda i: (i, 0)),
                  pl.BlockSpec((1, d), lambda i: (0, 0))],
        out_specs=pl.BlockSpec((tm, d), lambda i: (i, 0)),
        out_shape=jax.ShapeDtypeStruct((n, d), BF16),
        name="rmsnorm",
    )(x, g.reshape(1, d))


def _mm_kernel(x_ref, w_ref, *rest, epilogue, tn):
    o_ref = rest[-1]
    acc = jnp.dot(x_ref[...], w_ref[...], preferred_element_type=F32)
    if epilogue == "plain":
        out = acc
    elif epilogue == "silu":
        out = acc * _sigmoid(acc)
    elif epilogue == "rotary":
        cc, ss = rest[1][...], rest[2][...]
        pieces = []
        for j in range(tn // LANE):
            xh = acc[:, j * LANE:(j + 1) * LANE]
            pieces.append(xh * cc + pltpu.roll(xh, LANE // 2, 1) * ss)
        out = jnp.concatenate(pieces, axis=1) * rest[0][...]
    else:
        raise ValueError(epilogue)
    o_ref[...] = out.astype(o_ref.dtype)


def _mm(x, w, *, epilogue, out_dtype, tm, tn, colvec=None, tables=None, name):
    n, k = x.shape
    ncols = w.shape[1]
    in_specs = [pl.BlockSpec((tm, k), lambda i, j: (i, 0)),
                pl.BlockSpec((k, tn), lambda i, j: (0, j))]
    args = [x, w]
    if colvec is not None:
        in_specs.append(pl.BlockSpec((1, tn), lambda i, j: (0, j)))
        args.append(colvec.reshape(1, ncols).astype(F32))
    if tables is not None:
        nblk = tables[0].shape[0] // tm
        for t in tables:
            in_specs.append(pl.BlockSpec((tm, LANE), lambda i, j: (i % nblk, 0)))
            args.append(t)
    return pl.pallas_call(
        functools.partial(_mm_kernel, epilogue=epilogue, tn=tn),
        grid=(n // tm, ncols // tn),
        in_specs=in_specs,
        out_specs=pl.BlockSpec((tm, tn), lambda i, j: (i, j)),
        out_shape=jax.ShapeDtypeStruct((n, ncols), out_dtype),
        compiler_params=pltpu.CompilerParams(dimension_semantics=("parallel", "parallel")),
        name=name,
    )(*args)


def _norm_mm_kernel(x_ref, g_ref, w_ref, cv_ref, o_ref, u_ref):
    @pl.when(pl.program_id(1) == 0)
    def _():
        u_ref[...] = _rms(x_ref[...], g_ref[...]).astype(u_ref.dtype)

    acc = jnp.dot(u_ref[...], w_ref[...], preferred_element_type=F32)
    o_ref[...] = (acc * cv_ref[...]).astype(o_ref.dtype)


def _norm_mm(x, g, w, colvec, *, tm, tn, name):
    n, k = x.shape
    ncols = w.shape[1]
    return pl.pallas_call(
        _norm_mm_kernel,
        grid=(n // tm, ncols // tn),
        in_specs=[pl.BlockSpec((tm, k), lambda i, j: (i, 0)),
                  pl.BlockSpec((1, k), lambda i, j: (0, 0)),
                  pl.BlockSpec((k, tn), lambda i, j: (0, j)),
                  pl.BlockSpec((1, tn), lambda i, j: (0, j))],
        out_specs=[pl.BlockSpec((tm, tn), lambda i, j: (i, j)),
                   pl.BlockSpec((tm, k), lambda i, j: (i, 0))],
        out_shape=[jax.ShapeDtypeStruct((n, ncols), BF16), jax.ShapeDtypeStruct((n, k), BF16)],
        compiler_params=pltpu.CompilerParams(dimension_semantics=("parallel", "arbitrary")),
        name=name,
    )(x, g.reshape(1, k), w, colvec.reshape(1, ncols).astype(F32))


def _swa_kernel(sink_ref, q_ref, kp_ref, kc_ref, vp_ref, vc_ref, o_ref, *scratch, qt, kv_off, bb):
    n_sc = bb * SWA_KV_HEADS
    for bl in range(bb):
        _swa_one(sink_ref, q_ref.at[bl], kp_ref.at[bl], kc_ref.at[bl], vp_ref.at[bl], vc_ref.at[bl], o_ref.at[bl],
                 scratch[bl * SWA_KV_HEADS:(bl + 1) * SWA_KV_HEADS],
                 scratch[n_sc + bl * SWA_KV_HEADS:n_sc + (bl + 1) * SWA_KV_HEADS], qt=qt, kv_off=kv_off)


def _swa_one(sink_ref, q_ref, kp_ref, kc_ref, vp_ref, vc_ref, o_ref, s_scs, p_scs, *, qt, kv_off):
    i = pl.program_id(1)
    nq = qt // CHUNK
    w = (nq + 2) * CHUNK
    wp = -(-w // LANE) * LANE
    first_ok = jnp.where(i * qt + kv_off >= WINDOW, 0, 2)

    def window(p_ref, c_ref):
        parts = [p_ref[...], c_ref[...]]
        if wp > w:
            parts.append(jnp.zeros((wp - w, SWA_KV), F32))
        return jnp.concatenate(parts, axis=0)

    kwin = window(kp_ref, kc_ref)
    vwin = window(vp_ref, vc_ref)
    lane = lax.broadcasted_iota(jnp.int32, (wp, LANE), 1)
    lane_g = lax.broadcasted_iota(jnp.int32, (wp, SWA_KV), 1) // SWA_HEAD_DIM
    out_g = lax.broadcasted_iota(jnp.int32, (CHUNK, SWA_KV), 1) // SWA_HEAD_DIM

    def block_diag(win, h):
        col, half = divmod(h, 2)
        x = win[:, col * LANE:(col + 1) * LANE]
        xr = pltpu.roll(x, SWA_HEAD_DIM, 1)
        lo = lane < SWA_HEAD_DIM
        r = jnp.where(lo, x, xr) if half == 0 else jnp.where(lo, xr, x)
        r2 = jnp.concatenate([r, r], axis=1).astype(BF16)
        zero = jnp.zeros_like(r2)
        return jnp.concatenate([jnp.where(lane_g == g, r2, zero) for g in range(SWA_GROUP)], axis=0)

    for h in range(SWA_KV_HEADS):
        bk = block_diag(kwin, h)
        bv = block_diag(vwin, h)
        qh = q_ref[:, h * SWA_KV:(h + 1) * SWA_KV]
        s_sc, p_sc = s_scs[h], p_scs[h]
        s_sc[...] = lax.dot_general(qh, bk, (((1,), (1,)), ((), ())), preferred_element_type=F32)
        invs = []
        for jq in range(nq):
            rows = slice(jq * CHUNK, (jq + 1) * CHUNK)
            c_lo, c_hi = jq * CHUNK // LANE, ((jq + 3) * CHUNK - 1) // LANE
            width = (c_hi - c_lo + 1) * LANE
            kc = c_lo * (LANE // CHUNK) + lax.broadcasted_iota(jnp.int32, (CHUNK, width), 1) // CHUNK
            visible = (kc >= jq) & (kc <= jq + 2) & (kc >= first_ok)
            inv = jnp.zeros((CHUNK, SWA_KV), F32)
            for g in range(SWA_GROUP):
                l0 = g * wp + c_lo * LANE
                sg = jnp.where(visible, s_sc[rows, l0:l0 + width], NEG)
                sk = sink_ref[h * SWA_GROUP + g]
                m = jnp.maximum(jnp.max(sg, axis=1, keepdims=True), sk)
                p = jnp.exp(sg - m)
                den = jnp.sum(p, axis=1, keepdims=True) + jnp.exp(sk - m)
                p_sc[rows, l0:l0 + width] = p.astype(BF16)
                for c in range(wp // LANE):
                    if not c_lo <= c <= c_hi:
                        p_sc[rows, g * wp + c * LANE:g * wp + (c + 1) * LANE] = jnp.zeros((CHUNK, LANE), BF16)
                inv = jnp.where(out_g == g, 1.0 / den, inv)
            invs.append(inv)
        o = jnp.dot(p_sc[...], bv, preferred_element_type=F32)
        o_ref[:, h * SWA_KV:(h + 1) * SWA_KV] = (o * jnp.concatenate(invs, axis=0)).astype(o_ref.dtype)


def _swa(q, q_col, k, k_col, v, v_col, sink, *, sq, qt, kv_off, bb):
    b = q.shape[0]
    cur_off = kv_off // qt
    prev_off = kv_off // WINDOW - 1
    per = qt // WINDOW if qt >= WINDOW else 0
    wp = -(-(qt + WINDOW) // LANE) * LANE

    def prev_map(kcol):
        return lambda bi, i, s: (bi, jnp.maximum(i * per + prev_off, 0), kcol)

    def cur_map(kcol):
        return lambda bi, i, s: (bi, i + cur_off, kcol)

    grid_spec = pltpu.PrefetchScalarGridSpec(
        num_scalar_prefetch=1,
        grid=(b // bb, sq // qt),
        in_specs=[pl.BlockSpec((bb, qt, SWA_Q), lambda bi, i, s: (bi, i, q_col)),
                  pl.BlockSpec((bb, WINDOW, SWA_KV), prev_map(k_col)),
                  pl.BlockSpec((bb, qt, SWA_KV), cur_map(k_col)),
                  pl.BlockSpec((bb, WINDOW, SWA_KV), prev_map(v_col)),
                  pl.BlockSpec((bb, qt, SWA_KV), cur_map(v_col))],
        out_specs=pl.BlockSpec((bb, qt, SWA_Q), lambda bi, i, s: (bi, i, 0)),
        scratch_shapes=([pltpu.VMEM((qt, SWA_GROUP * wp), F32)] * (bb * SWA_KV_HEADS)
                        + [pltpu.VMEM((qt, SWA_GROUP * wp), BF16)] * (bb * SWA_KV_HEADS)),
    )
    return pl.pallas_call(
        functools.partial(_swa_kernel, qt=qt, kv_off=kv_off, bb=bb),
        grid_spec=grid_spec,
        out_shape=jax.ShapeDtypeStruct((b, sq, SWA_Q), BF16),
        compiler_params=pltpu.CompilerParams(dimension_semantics=("parallel", "parallel")),
        name="swa",
    )(sink.astype(F32), q, k, k, v, v)


def _ret_kernel(lg_ref, q_ref, k_ref, v_ref, g_ref, s0_ref, o_ref, st_ref, dec_ref, xi_ref, zeta_ref, *, c, bb):
    bi = pl.program_id(0)
    ci = pl.program_id(1)

    @pl.when((bi == 0) & (ci == 0))
    def _():
        n = lax.broadcasted_iota(jnp.int32, (c, c), 0)
        m = lax.broadcasted_iota(jnp.int32, (c, c), 1)
        diff = (n - m).astype(F32)
        nrow = lax.broadcasted_iota(jnp.int32, (c, RET_DIM), 0).astype(F32)
        for h in range(RET_HEADS):
            dec_ref[h] = jnp.where(diff >= 0, jnp.exp(lg_ref[h] * jnp.maximum(diff, 0.0)), 0.0)
            xi_ref[h] = jnp.exp(lg_ref[h] * (nrow + 1.0))
            zeta_ref[h] = jnp.exp(lg_ref[h] * (float(c) - 1.0 - nrow))

    @pl.when(ci == 0)
    def _():
        st_ref[...] = s0_ref[...]

    for bl in range(bb):
        for h in range(RET_HEADS):
            lg = lg_ref[h]
            sl = slice(h * RET_DIM, (h + 1) * RET_DIM)
            qh = q_ref[bl, :, sl]
            kh = k_ref[bl, :, sl]
            vh = v_ref[bl, :, sl]
            state = st_ref[bl, h]
            inner = lax.dot_general(qh, kh, (((1,), (1,)), ((), ())), preferred_element_type=F32) * dec_ref[h]
            o = jnp.dot(inner.astype(BF16), vh, preferred_element_type=F32)
            cross = jnp.dot(qh, state.astype(BF16), preferred_element_type=F32)
            o = o + cross * xi_ref[h]
            kz = (kh.astype(F32) * zeta_ref[h]).astype(BF16)
            upd = lax.dot_general(kz, vh, (((0,), (0,)), ((), ())), preferred_element_type=F32)
            decay_c = jnp.exp(lg * jnp.full((1, RET_DIM), float(c), F32))
            st_ref[bl, h] = decay_c * state + upd
            on = o * lax.rsqrt(jnp.mean(o * o, axis=-1, keepdims=True) + EPS)
            o_ref[bl, :, sl] = (g_ref[bl, :, sl].astype(F32) * on).astype(o_ref.dtype)


def _retention(qk, q_col, k_col, v, v_col, gate, state0, log_g, *, s, c, bb):
    b = qk.shape[0]

    def seq(col):
        return pl.BlockSpec((bb, c, RET_W), lambda bi, ci, lg: (bi, ci, col))

    st_spec = pl.BlockSpec((bb, RET_HEADS, RET_DIM, RET_DIM), lambda bi, ci, lg: (bi, 0, 0, 0))
    grid_spec = pltpu.PrefetchScalarGridSpec(
        num_scalar_prefetch=1,
        grid=(b // bb, s // c),
        in_specs=[seq(q_col), seq(k_col), seq(v_col), seq(0), st_spec],
        out_specs=[seq(0), st_spec],
        scratch_shapes=[pltpu.VMEM((RET_HEADS, c, c), F32), pltpu.VMEM((RET_HEADS, c, RET_DIM), F32),
                        pltpu.VMEM((RET_HEADS, c, RET_DIM), F32)],
    )
    return pl.pallas_call(
        functools.partial(_ret_kernel, c=c, bb=bb),
        grid_spec=grid_spec,
        out_shape=[jax.ShapeDtypeStruct((b, s, RET_W), BF16),
                   jax.ShapeDtypeStruct((b, RET_HEADS, RET_DIM, RET_DIM), F32)],
        compiler_params=pltpu.CompilerParams(dimension_semantics=("arbitrary", "arbitrary")),
        name="retention",
    )(log_g, qk, qk, v, gate, state0)


def _mem_kernel(q_ref, k_ref, v_ref, o_ref, *, bb):
    for bl in range(bb):
        for h in range(MEM_HEADS):
            sl = slice(h * MEM_HEAD_DIM, (h + 1) * MEM_HEAD_DIM)
            qh = q_ref[bl, :, sl]
            kh = k_ref[bl, :, sl].astype(BF16)
            vh = v_ref[bl, :, sl].astype(BF16)
            s = lax.dot_general(qh, kh, (((1,), (1,)), ((), ())), preferred_element_type=F32)
            m = jnp.max(s, axis=1, keepdims=True)
            p = jnp.exp(s - m)
            den = jnp.sum(p, axis=1, keepdims=True)
            o = jnp.dot(p.astype(BF16), vh, preferred_element_type=F32)
            o_ref[bl, :, sl] = (o * (1.0 / den)).astype(o_ref.dtype)


def _mem_attend(q, q_col, mk, mv, *, s, tq, bb):
    b = q.shape[0]
    kv_spec = pl.BlockSpec((bb, N_MEM, MEM_W), lambda bi, i: (bi, 0, 0))
    return pl.pallas_call(
        functools.partial(_mem_kernel, bb=bb),
        grid=(b // bb, s // tq),
        in_specs=[pl.BlockSpec((bb, tq, MEM_W), lambda bi, i: (bi, i, q_col)), kv_spec, kv_spec],
        out_specs=pl.BlockSpec((bb, tq, MEM_W), lambda bi, i: (bi, i, 0)),
        out_shape=jax.ShapeDtypeStruct((b, s, MEM_W), BF16),
        compiler_params=pltpu.CompilerParams(dimension_semantics=("parallel", "parallel")),
        name="mem_attend",
    )(q, mk, mv)


def _merge_kernel(u_ref, oa_ref, ob_ref, oc_ref, wga_ref, wgb_ref, wgc_ref, ba_ref, bb_ref, bc_ref,
                  wa_ref, wb_ref, wc_ref, o_ref):
    u = u_ref[...]
    acc = None
    for o_r, wg_ref, b_ref, w_ref in ((oa_ref, wga_ref, ba_ref, wa_ref), (ob_ref, wgb_ref, bb_ref, wb_ref),
                                      (oc_ref, wgc_ref, bc_ref, wc_ref)):
        gate = _sigmoid(jnp.dot(u, wg_ref[...], preferred_element_type=F32) + b_ref[...])
        term = gate * jnp.dot(o_r[...], w_ref[...], preferred_element_type=F32)
        acc = term if acc is None else acc + term
    o_ref[...] = acc.astype(o_ref.dtype)


def _merge(u, o_swa, o_ret, o_mem, w_gate, gate_col0, b_gate, w_br, *, tm, tn):
    n, d = u.shape
    kb = o_swa.shape[1]
    nj = D_MODEL // tn
    g0 = gate_col0 // tn
    o_spec = pl.BlockSpec((tm, kb), lambda i, j: (i, 0))

    def wg_spec(br):
        return pl.BlockSpec((d, tn), lambda i, j: (0, g0 + br * nj + j))

    def b_spec(br):
        return pl.BlockSpec((1, tn), lambda i, j: (0, br * nj + j))

    def w_spec(br):
        return pl.BlockSpec((kb, tn), lambda i, j: (br, j))

    bias = b_gate.reshape(1, -1).astype(F32)
    return pl.pallas_call(
        _merge_kernel,
        grid=(n // tm, nj),
        in_specs=[pl.BlockSpec((tm, d), lambda i, j: (i, 0)), o_spec, o_spec, o_spec,
                  wg_spec(0), wg_spec(1), wg_spec(2), b_spec(0), b_spec(1), b_spec(2),
                  w_spec(0), w_spec(1), w_spec(2)],
        out_specs=pl.BlockSpec((tm, tn), lambda i, j: (i, j)),
        out_shape=jax.ShapeDtypeStruct((n, D_MODEL), BF16),
        compiler_params=pltpu.CompilerParams(dimension_semantics=("parallel", "parallel")),
        name="merge",
    )(u, o_swa, o_ret, o_mem, w_gate, w_gate, w_gate, bias, bias, bias, w_br, w_br, w_br)


def _ffn_up_kernel(u_ref, wug_ref, wuv_ref, wc_ref, b_ref, cb_ref, act_ref, co_ref,
                   ag_sc, av_sc, cg_sc, cv_sc, *, tm, tf, nf, nseg):
    i = pl.program_id(1)
    fb = pl.program_id(2)
    seg_t = tm // nseg
    stride = seg_t + 8
    if nseg == 1:
        @pl.when((i == 0) & (fb == 0))
        def _():
            for f in range(nf):
                cg_sc[f, 6:8, :] = cb_ref[0, f]
                cv_sc[f, 6:8, :] = cb_ref[0, nf + f]

    u = u_ref[0]
    for a_sc, wu_ref, blk, carry_sc in ((ag_sc, wug_ref, fb, cg_sc), (av_sc, wuv_ref, nf + fb, cv_sc)):
        a = jnp.dot(u, wu_ref[...], preferred_element_type=F32)
        for s in range(nseg):
            a_sc[s * stride + 8:(s + 1) * stride, :] = a[s * seg_t:(s + 1) * seg_t]
            co_ref[0, s, blk] = a[(s + 1) * seg_t - 8:(s + 1) * seg_t]
        if nseg == 1:
            a_sc[6:8, :] = carry_sc[fb, 6:8, :]
            carry_sc[fb] = a[tm - 8:tm]
        else:
            for s in range(nseg):
                a_sc[s * stride + 6:s * stride + 8, :] = cb_ref[s, blk]

    def conv(a_sc, blk, r0, c0):
        tap = lambda k: a_sc[r0 - k:r0 - k + CONV_ROWS, c0:c0 + LANE]
        wc = lambda k: wc_ref[blk, k:k + 1, c0:c0 + LANE]
        return b_ref[blk, :, c0:c0 + LANE] + (wc(0) * tap(2) + wc(1) * tap(1) + wc(2) * tap(0))

    for c0 in range(0, tf, LANE):
        for s in range(nseg):
            for r in range(0, seg_t, CONV_ROWS):
                cg = conv(ag_sc, fb, s * stride + 8 + r, c0)
                cv = conv(av_sc, nf + fb, s * stride + 8 + r, c0)
                act_ref[0, s * seg_t + r:s * seg_t + r + CONV_ROWS, c0:c0 + LANE] = (
                    cg * _sigmoid(cg) * cv).astype(act_ref.dtype)


def _ffn_up(u, conv_buf, w_up, w_conv, b_conv, *, tm, tf, nseg):
    bt, rows, d = u.shape
    nf = D_FF // tf
    nt = rows // tm
    nb = bt * nseg
    wc_blocks = w_conv.reshape(3, 2 * nf, tf).transpose(1, 0, 2)
    b_blocks = b_conv.reshape(2 * nf, 1, tf)
    cb_blocks = conv_buf.reshape(nb, 2, 2 * nf, tf).transpose(0, 2, 1, 3)
    whole = lambda *shape: pl.BlockSpec(shape, lambda b, i, f: (0,) * len(shape))
    return pl.pallas_call(
        functools.partial(_ffn_up_kernel, tm=tm, tf=tf, nf=nf, nseg=nseg),
        grid=(bt, nt, nf),
        in_specs=[pl.BlockSpec((1, tm, d), lambda b, i, f: (b, i, 0)),
                  pl.BlockSpec((d, tf), lambda b, i, f: (0, f)),
                  pl.BlockSpec((d, tf), lambda b, i, f: (0, nf + f)),
                  whole(2 * nf, 3, tf),
                  whole(2 * nf, 1, tf),
                  pl.BlockSpec((nseg, 2 * nf, 2, tf), lambda b, i, f: (b, 0, 0, 0))],
        out_specs=[pl.BlockSpec((1, tm, tf), lambda b, i, f: (b, i, f)),
                   pl.BlockSpec((1, nseg, 2 * nf, 8, tf), lambda b, i, f: (b * nt + i, 0, 0, 0, 0))],
        out_shape=[jax.ShapeDtypeStruct((bt, rows, D_FF), BF16),
                   jax.ShapeDtypeStruct((bt * nt, nseg, 2 * nf, 8, tf), F32)],
        scratch_shapes=[pltpu.VMEM((tm + 8 * nseg, tf), F32), pltpu.VMEM((tm + 8 * nseg, tf), F32),
                        pltpu.VMEM((nf, 8, tf), F32), pltpu.VMEM((nf, 8, tf), F32)],
        compiler_params=pltpu.CompilerParams(dimension_semantics=("arbitrary", "arbitrary", "arbitrary")),
        name="ffn_up",
    )(u, w_up, w_up, wc_blocks, b_blocks, cb_blocks)


def _ffn_down_kernel(act_ref, wd_ref, h_ref, g_ref, y_ref, *, nd, td):
    j = pl.program_id(1)
    for jj in range(nd):
        @pl.when(j == jj)
        def _():
            y_ref[:, jj * td:(jj + 1) * td] = h_ref[...] + jnp.dot(act_ref[...], wd_ref[...],
                                                                   preferred_element_type=F32)

    @pl.when(j == nd - 1)
    def _():
        y_ref[...] = _rms(y_ref[...], g_ref[...])


def _ffn_down(act, w_down, h, g_final, *, tm, td):
    n, f = act.shape
    d = h.shape[1]
    nd = d // td
    return pl.pallas_call(
        functools.partial(_ffn_down_kernel, nd=nd, td=td),
        grid=(n // tm, nd),
        in_specs=[pl.BlockSpec((tm, f), lambda i, j: (i, 0)),
                  pl.BlockSpec((f, td), lambda i, j: (0, j)),
                  pl.BlockSpec((tm, td), lambda i, j: (i, j)),
                  pl.BlockSpec((1, d), lambda i, j: (0, 0))],
        out_specs=pl.BlockSpec((tm, d), lambda i, j: (i, 0)),
        out_shape=jax.ShapeDtypeStruct((n, d), F32),
        compiler_params=pltpu.CompilerParams(dimension_semantics=("parallel", "arbitrary")),
        name="ffn_down",
    )(act, w_down, h, g_final.reshape(1, d))


def _out_proj_kernel(x_ref, w_ref, res_ref, g_ref, h_ref, u_ref):
    h = res_ref[...] + jnp.dot(x_ref[...], w_ref[...], preferred_element_type=F32)
    h_ref[...] = h
    u_ref[...] = _rms(h, g_ref[...]).astype(u_ref.dtype)


def _out_proj(x, w, residual, g, *, tm):
    n, k = x.shape
    d = w.shape[1]
    row = lambda width: pl.BlockSpec((tm, width), lambda i: (i, 0))
    return pl.pallas_call(
        _out_proj_kernel,
        grid=(n // tm,),
        in_specs=[row(k), pl.BlockSpec((k, d), lambda i: (0, 0)), row(d), pl.BlockSpec((1, d), lambda i: (0, 0))],
        out_specs=[row(d), row(d)],
        out_shape=[jax.ShapeDtypeStruct((n, d), F32), jax.ShapeDtypeStruct((n, d), BF16)],
        compiler_params=pltpu.CompilerParams(dimension_semantics=("parallel",)),
        name="proj_out",
    )(x, w, residual, g.reshape(1, d))


def _rope_tables(pos0, s):
    half = RET_DIM // 2
    inv_freq = 1.0 / (RET_ROPE_BASE ** np.linspace(0.0, 1.0, half))
    ang = (pos0 + np.arange(s))[:, None] * inv_freq[None, :]
    cos, sin = np.cos(ang), np.sin(ang)
    return (jnp.asarray(np.concatenate([cos, cos], axis=1), F32),
            jnp.asarray(np.concatenate([-sin, sin], axis=1), F32))


def _prep_weights(g_mix, w_in, b_gate, sink, w_br, w_o, g_ffn, w_up, w_conv, b_conv, w_down):
    o = 0
    cuts = {}
    for name, width in (("qa", SWA_Q), ("ka", SWA_KV), ("va", SWA_KV), ("qr", RET_W), ("kr", RET_W),
                        ("vr", RET_W), ("gr", RET_W), ("qm", MEM_W), ("gl", N_BRANCH * D_MODEL)):
        cuts[name] = (o, o + width)
        o += width
    cols = lambda *names: jnp.concatenate([w_in[:, cuts[n][0]:cuts[n][1]] for n in names], axis=1).astype(BF16)
    ones = lambda n, v: jnp.full((n,), v, F32)
    return dict(
        g_mix=g_mix, b_gate=b_gate, sink=sink, g_ffn=g_ffn, w_conv=w_conv, b_conv=b_conv,
        w_plain=cols("qa", "vr", "qm"),
        s_plain=jnp.concatenate([ones(SWA_Q, SWA_HEAD_DIM ** -0.5), ones(RET_W, 1.0),
                                 ones(MEM_W, MEM_HEAD_DIM ** -0.5)]),
        w_kv=cols("ka", "va"),
        w_rot=cols("qr", "kr"),
        s_rot=jnp.concatenate([ones(RET_W, 1.0), ones(RET_W, RET_DIM ** -0.5)]),
        w_g=cols("gr"),
        w_in16=w_in.astype(BF16), gate_col0=cuts["gl"][0],
        w_br=w_br.astype(BF16), w_o=w_o.astype(BF16), w_up=w_up.astype(BF16), w_down=w_down.astype(BF16),
    )


def _run_group(x, pos0, mem_k, mem_v, swa_cache, ret_state, conv_buf, wts, g_final, log_g):
    b, s, d = x.shape
    n = b * s
    tm = min(ROW_TILE, n)
    x2 = x.reshape(n, d)
    plain, u = _norm_mm(x2, wts["g_mix"], wts["w_plain"], wts["s_plain"], tm=min(OUT_PROJ_ROWS, n),
                        tn=wts["w_plain"].shape[1], name="proj_plain")
    kv = _mm(u, wts["w_kv"], epilogue="plain", out_dtype=F32, tm=tm, tn=2 * SWA_KV, name="proj_kv")
    cc, ss = _rope_tables(pos0, s)
    if s < tm:
        cc, ss = jnp.tile(cc, (tm // s, 1)), jnp.tile(ss, (tm // s, 1))
    rot = _mm(u, wts["w_rot"], epilogue="rotary", out_dtype=BF16, tm=tm, tn=COL_TILE,
              colvec=wts["s_rot"], tables=(cc, ss), name="proj_rot")
    sgate = _mm(u, wts["w_g"], epilogue="silu", out_dtype=BF16, tm=tm, tn=COL_TILE, name="proj_silu")

    plain3 = plain.reshape(b, s, -1)
    kv3 = kv.reshape(b, s, 2 * SWA_KV)
    if swa_cache is None:
        o_swa = _swa(plain3, 0, kv3, 0, kv3, 1, wts["sink"], sq=s, qt=min(SWA_ROWS, s), kv_off=0,
                     bb=min(SWA_SEQS, b))
        new_k = kv3[:, s - WINDOW:, :SWA_KV]
        new_v = kv3[:, s - WINDOW:, SWA_KV:]
    else:
        ck = swa_cache[0].reshape(b, -1, SWA_KV)
        cv = swa_cache[1].reshape(b, -1, SWA_KV)
        n_keep = ck.shape[1]
        k_all = jnp.concatenate([ck, kv3[:, :, :SWA_KV]], axis=1)
        v_all = jnp.concatenate([cv, kv3[:, :, SWA_KV:]], axis=1)
        o_swa = _swa(plain3, 0, k_all, 0, v_all, 0, wts["sink"], sq=s, qt=s, kv_off=n_keep, bb=min(SWA_SEQS, b))
        new_k = k_all[:, -n_keep:]
        new_v = v_all[:, -n_keep:]
    state0 = jnp.zeros((b, RET_HEADS, RET_DIM, RET_DIM), F32) if ret_state is None else ret_state
    o_ret, s_new = _retention(rot.reshape(b, s, -1), 0, 1, plain3, 1, sgate.reshape(b, s, -1), state0, log_g,
                              s=s, c=min(RET_ROWS, s), bb=min(RET_SEQS, b))
    o_mem = _mem_attend(plain3, 2, mem_k.reshape(b, N_MEM, MEM_W), mem_v.reshape(b, N_MEM, MEM_W),
                        s=s, tq=min(MEM_ROWS, s), bb=min(MEM_SEQS, b))
    merged = _merge(u, o_swa.reshape(n, -1), o_ret.reshape(n, -1), o_mem.reshape(n, -1), wts["w_in16"],
                    wts["gate_col0"], wts["b_gate"], wts["w_br"], tm=tm, tn=MERGE_COLS)
    h1, u2 = _out_proj(merged, wts["w_o"], x2, wts["g_ffn"], tm=min(OUT_PROJ_ROWS, n))
    if s >= ROW_TILE:
        act, co = _ffn_up(u2.reshape(b, s, d), conv_buf, wts["w_up"], wts["w_conv"], wts["b_conv"],
                          tm=ROW_TILE, tf=FF_TILE, nseg=1)
    else:
        act, co = _ffn_up(u2.reshape(1, n, d), conv_buf, wts["w_up"], wts["w_conv"], wts["b_conv"],
                          tm=n, tf=FF_TILE, nseg=b)
    y = _ffn_down(act.reshape(n, D_FF), wts["w_down"], h1, g_final, tm=tm, td=FF_TILE)
    nblk, tfb = co.shape[2], co.shape[4]
    last = co.reshape(b, -1, nblk, 8, tfb)[:, -1, :, 6:8, :]
    new_buf = last.transpose(0, 2, 1, 3).reshape(b, 2, nblk * tfb)
    return (y.reshape(b, s, d), new_k.reshape(b, -1, SWA_KV_HEADS, SWA_HEAD_DIM),
            new_v.reshape(b, -1, SWA_KV_HEADS, SWA_HEAD_DIM), s_new, new_buf)


def _memory_kv(mem, g_mem, w_mem_kv):
    b, m, d = mem.shape
    u = _rmsnorm(mem.reshape(b * m, d), g_mem, min(OUT_PROJ_ROWS, b * m))
    kv = _mm(u, w_mem_kv.astype(BF16), epilogue="plain", out_dtype=F32, tm=min(ROW_TILE, b * m), tn=COL_TILE,
             name="mem_kv")
    return (kv[:, :MEM_W].reshape(b, m, MEM_HEADS, MEM_HEAD_DIM), kv[:, MEM_W:].reshape(b, m, MEM_HEADS, MEM_HEAD_DIM))


def kernel(x_prompt, x_sample, mem_prompt, cache_swa_k, cache_swa_v, state_ret, state_ffn_conv, cache_mem_k, cache_mem_v, g_mix, w_in, b_gate, sink, w_br, w_o, g_mem, w_mem_kv, g_ffn, w_up, w_conv, b_conv, w_down, g_final):
    bp = x_prompt.shape[0]
    depth = w_in.shape[0]
    assert depth == 1, "single-layer problem: the final norm is fused into the layer's FFN"
    log_g = jnp.log1p(-jnp.exp2(-5.0 - jnp.arange(RET_HEADS, dtype=F32)))
    hp, hs = x_prompt, x_sample
    outs_p = [[] for _ in range(6)]
    outs_s = [[] for _ in range(4)]
    for l in range(depth):
        wts = _prep_weights(g_mix[l], w_in[l], b_gate[l], sink[l], w_br[l], w_o[l],
                            g_ffn[l], w_up[l], w_conv[l], b_conv[l], w_down[l])
        mk, mv = _memory_kv(mem_prompt, g_mem[l], w_mem_kv[l])
        zero_buf = jnp.zeros((bp, 2, 2 * D_FF), F32)
        hp, kp, vp, sp, cp = _run_group(hp, 0, mk, mv, None, None, zero_buf, wts, g_final, log_g)
        hs, ksn, vsn, ssn, csn = _run_group(hs, PAST_LEN, cache_mem_k[l], cache_mem_v[l],
                                            (cache_swa_k[l], cache_swa_v[l]), state_ret[l],
                                            state_ffn_conv[l], wts, g_final, log_g)
        for lst, val in zip(outs_p, (kp, vp, sp, cp, mk, mv)):
            lst.append(val)
        for lst, val in zip(outs_s, (ksn, vsn, ssn, csn)):
            lst.append(val)
    return (hp, hs, *[jnp.stack(v) for v in outs_p], *[jnp.stack(v) for v in outs_s])
```

```python
import functools

import jax
import jax.numpy as jnp
import numpy as np
from jax import lax
from jax.experimental import pallas as pl
from jax.experimental.pallas import tpu as pltpu

F32 = jnp.float32
BF16 = jnp.bfloat16

D_MODEL = 2048
CHUNK = 64
WINDOW = 128
SWA_HEADS = 16
SWA_KV_HEADS = 4
SWA_GROUP = SWA_HEADS // SWA_KV_HEADS
SWA_HEAD_DIM = 64
RET_HEADS = 8
RET_DIM = 128
RET_ROPE_BASE = 10000.0
N_MEM = 256
MEM_HEADS = 4
MEM_HEAD_DIM = 256
D_FF = 5632
N_BRANCH = 3
EPS = 1e-6
NEG = -1e30
PAST_LEN = 1024

SWA_Q = SWA_HEADS * SWA_HEAD_DIM
SWA_KV = SWA_KV_HEADS * SWA_HEAD_DIM
RET_W = RET_HEADS * RET_DIM
MEM_W = MEM_HEADS * MEM_HEAD_DIM
LANE = 128
CONV_ROWS = 64
ROW_TILE = 1024
COL_TILE = 1024
MERGE_COLS = 512
OUT_PROJ_ROWS = 512
FF_TILE = 512
SWA_ROWS = 256
SWA_SEQS = 4
RET_ROWS = 256
RET_SEQS = 4
MEM_ROWS = 1024
MEM_SEQS = 4


def _sigmoid(x):
    return 1.0 / (1.0 + jnp.exp(-x))


def _rms(x, g):
    return x * lax.rsqrt(jnp.mean(x * x, axis=-1, keepdims=True) + EPS) * g


def _rmsnorm_kernel(x_ref, g_ref, o_ref):
    o_ref[...] = _rms(x_ref[...], g_ref[...]).astype(o_ref.dtype)


def _rmsnorm(x, g, tm):
    n, d = x.shape
    return pl.pallas_call(
        _rmsnorm_kernel,
        grid=(n // tm,),
        in_specs=[pl.BlockSpec((tm, d), lambda i: (i, 0)),
                  pl.BlockSpec((1, d), lambda i: (0, 0))],
        out_specs=pl.BlockSpec((tm, d), lambda i: (i, 0)),
        out_shape=jax.ShapeDtypeStruct((n, d), BF16),
        name="rmsnorm",
    )(x, g.reshape(1, d))


def _mm_kernel(x_ref, w_ref, *rest, epilogue, tn):
    o_ref = rest[-1]
    acc = jnp.dot(x_ref[...], w_ref[...], preferred_element_type=F32)
    if epilogue == "plain":
        out = acc
    elif epilogue == "silu":
        out = acc * _sigmoid(acc)
    elif epilogue == "rotary":
        cc, ss = rest[1][...], rest[2][...]
        pieces = []
        for j in range(tn // LANE):
            xh = acc[:, j * LANE:(j + 1) * LANE]
            pieces.append(xh * cc + pltpu.roll(xh, LANE // 2, 1) * ss)
        out = jnp.concatenate(pieces, axis=1) * rest[0][...]
    else:
        raise ValueError(epilogue)
    o_ref[...] = out.astype(o_ref.dtype)


def _mm(x, w, *, epilogue, out_dtype, tm, tn, colvec=None, tables=None, name):
    n, k = x.shape
    ncols = w.shape[1]
    in_specs = [pl.BlockSpec((tm, k), lambda i, j: (i, 0)),
                pl.BlockSpec((k, tn), lambda i, j: (0, j))]
    args = [x, w]
    if colvec is not None:
        in_specs.append(pl.BlockSpec((1, tn), lambda i, j: (0, j)))
        args.append(colvec.reshape(1, ncols).astype(F32))
    if tables is not None:
        nblk = tables[0].shape[0] // tm
        for t in tables:
            in_specs.append(pl.BlockSpec((tm, LANE), lambda i, j: (i % nblk, 0)))
            args.append(t)
    return pl.pallas_call(
        functools.partial(_mm_kernel, epilogue=epilogue, tn=tn),
        grid=(n // tm, ncols // tn),
        in_specs=in_specs,
        out_specs=pl.BlockSpec((tm, tn), lambda i, j: (i, j)),
        out_shape=jax.ShapeDtypeStruct((n, ncols), out_dtype),
        compiler_params=pltpu.CompilerParams(dimension_semantics=("parallel", "parallel")),
        name=name,
    )(*args)


def _norm_mm_kernel(x_ref, g_ref, w_ref, cv_ref, o_ref, u_ref):
    @pl.when(pl.program_id(1) == 0)
    def _():
        u_ref[...] = _rms(x_ref[...], g_ref[...]).astype(u_ref.dtype)

    acc = jnp.dot(u_ref[...], w_ref[...], preferred_element_type=F32)
    o_ref[...] = (acc * cv_ref[...]).astype(o_ref.dtype)


def _norm_mm(x, g, w, colvec, *, tm, tn, name):
    n, k = x.shape
    ncols = w.shape[1]
    return pl.pallas_call(
        _norm_mm_kernel,
        grid=(n // tm, ncols // tn),
        in_specs=[pl.BlockSpec((tm, k), lambda i, j: (i, 0)),
                  pl.BlockSpec((1, k), lambda i, j: (0, 0)),
                  pl.BlockSpec((k, tn), lambda i, j: (0, j)),
                  pl.BlockSpec((1, tn), lambda i, j: (0, j))],
        out_specs=[pl.BlockSpec((tm, tn), lambda i, j: (i, j)),
                   pl.BlockSpec((tm, k), lambda i, j: (i, 0))],
        out_shape=[jax.ShapeDtypeStruct((n, ncols), BF16), jax.ShapeDtypeStruct((n, k), BF16)],
        compiler_params=pltpu.CompilerParams(dimension_semantics=("parallel", "arbitrary")),
        name=name,
    )(x, g.reshape(1, k), w, colvec.reshape(1, ncols).astype(F32))


def _swa_kernel(sink_ref, q_ref, kp_ref, kc_ref, vp_ref, vc_ref, o_ref, *scratch, qt, kv_off, bb):
    n_sc = bb * SWA_KV_HEADS
    for bl in range(bb):
        _swa_one(sink_ref, q_ref.at[bl], kp_ref.at[bl], kc_ref.at[bl], vp_ref.at[bl], vc_ref.at[bl], o_ref.at[bl],
                 scratch[bl * SWA_KV_HEADS:(bl + 1) * SWA_KV_HEADS],
                 scratch[n_sc + bl * SWA_KV_HEADS:n_sc + (bl + 1) * SWA_KV_HEADS], qt=qt, kv_off=kv_off)


def _swa_one(sink_ref, q_ref, kp_ref, kc_ref, vp_ref, vc_ref, o_ref, s_scs, p_scs, *, qt, kv_off):
    i = pl.program_id(1)
    nq = qt // CHUNK
    w = (nq + 2) * CHUNK
    wp = -(-w // LANE) * LANE
    first_ok = jnp.where(i * qt + kv_off >= WINDOW, 0, 2)

    def window(p_ref, c_ref):
        parts = [p_ref[...], c_ref[...]]
        if wp > w:
            parts.append(jnp.zeros((wp - w, SWA_KV), F32))
        return jnp.concatenate(parts, axis=0)

    kwin = window(kp_ref, kc_ref)
    vwin = window(vp_ref, vc_ref)
    lane = lax.broadcasted_iota(jnp.int32, (wp, LANE), 1)
    lane_g = lax.broadcasted_iota(jnp.int32, (wp, SWA_KV), 1) // SWA_HEAD_DIM
    out_g = lax.broadcasted_iota(jnp.int32, (CHUNK, SWA_KV), 1) // SWA_HEAD_DIM

    def block_diag(win, h):
        col, half = divmod(h, 2)
        x = win[:, col * LANE:(col + 1) * LANE]
        xr = pltpu.roll(x, SWA_HEAD_DIM, 1)
        lo = lane < SWA_HEAD_DIM
        r = jnp.where(lo, x, xr) if half == 0 else jnp.where(lo, xr, x)
        r2 = jnp.concatenate([r, r], axis=1).astype(BF16)
        zero = jnp.zeros_like(r2)
        return jnp.concatenate([jnp.where(lane_g == g, r2, zero) for g in range(SWA_GROUP)], axis=0)

    for h in range(SWA_KV_HEADS):
        bk = block_diag(kwin, h)
        bv = block_diag(vwin, h)
        qh = q_ref[:, h * SWA_KV:(h + 1) * SWA_KV]
        s_sc, p_sc = s_scs[h], p_scs[h]
        s_sc[...] = lax.dot_general(qh, bk, (((1,), (1,)), ((), ())), preferred_element_type=F32)
        invs = []
        for jq in range(nq):
            rows = slice(jq * CHUNK, (jq + 1) * CHUNK)
            c_lo, c_hi = jq * CHUNK // LANE, ((jq + 3) * CHUNK - 1) // LANE
            width = (c_hi - c_lo + 1) * LANE
            kc = c_lo * (LANE // CHUNK) + lax.broadcasted_iota(jnp.int32, (CHUNK, width), 1) // CHUNK
            visible = (kc >= jq) & (kc <= jq + 2) & (kc >= first_ok)
            inv = jnp.zeros((CHUNK, SWA_KV), F32)
            for g in range(SWA_GROUP):
                l0 = g * wp + c_lo * LANE
                sg = jnp.where(visible, s_sc[rows, l0:l0 + width], NEG)
                sk = sink_ref[h * SWA_GROUP + g]
                m = jnp.maximum(jnp.max(sg, axis=1, keepdims=True), sk)
                p = jnp.exp(sg - m)
                den = jnp.sum(p, axis=1, keepdims=True) + jnp.exp(sk - m)
                p_sc[rows, l0:l0 + width] = p.astype(BF16)
                for c in range(wp // LANE):
                    if not c_lo <= c <= c_hi:
                        p_sc[rows, g * wp + c * LANE:g * wp + (c + 1) * LANE] = jnp.zeros((CHUNK, LANE), BF16)
                inv = jnp.where(out_g == g, 1.0 / den, inv)
            invs.append(inv)
        o = jnp.dot(p_sc[...], bv, preferred_element_type=F32)
        o_ref[:, h * SWA_KV:(h + 1) * SWA_KV] = (o * jnp.concatenate(invs, axis=0)).astype(o_ref.dtype)


def _swa(q, q_col, k, k_col, v, v_col, sink, *, sq, qt, kv_off, bb):
    b = q.shape[0]
    cur_off = kv_off // qt
    prev_off = kv_off // WINDOW - 1
    per = qt // WINDOW if qt >= WINDOW else 0
    wp = -(-(qt + WINDOW) // LANE) * LANE

    def prev_map(kcol):
        return lambda bi, i, s: (bi, jnp.maximum(i * per + prev_off, 0), kcol)

    def cur_map(kcol):
        return lambda bi, i, s: (bi, i + cur_off, kcol)

    grid_spec = pltpu.PrefetchScalarGridSpec(
        num_scalar_prefetch=1,
        grid=(b // bb, sq // qt),
        in_specs=[pl.BlockSpec((bb, qt, SWA_Q), lambda bi, i, s: (bi, i, q_col)),
                  pl.BlockSpec((bb, WINDOW, SWA_KV), prev_map(k_col)),
                  pl.BlockSpec((bb, qt, SWA_KV), cur_map(k_col)),
                  pl.BlockSpec((bb, WINDOW, SWA_KV), prev_map(v_col)),
                  pl.BlockSpec((bb, qt, SWA_KV), cur_map(v_col))],
        out_specs=pl.BlockSpec((bb, qt, SWA_Q), lambda bi, i, s: (bi, i, 0)),
        scratch_shapes=([pltpu.VMEM((qt, SWA_GROUP * wp), F32)] * (bb * SWA_KV_HEADS)
                        + [pltpu.VMEM((qt, SWA_GROUP * wp), BF16)] * (bb * SWA_KV_HEADS)),
    )
    return pl.pallas_call(
        functools.partial(_swa_kernel, qt=qt, kv_off=kv_off, bb=bb),
        grid_spec=grid_spec,
        out_shape=jax.ShapeDtypeStruct((b, sq, SWA_Q), BF16),
        compiler_params=pltpu.CompilerParams(dimension_semantics=("parallel", "parallel")),
        name="swa",
    )(sink.astype(F32), q, k, k, v, v)


def _ret_kernel(lg_ref, q_ref, k_ref, v_ref, g_ref, s0_ref, o_ref, st_ref, dec_ref, xi_ref, zeta_ref, *, c, bb):
    bi = pl.program_id(0)
    ci = pl.program_id(1)

    @pl.when((bi == 0) & (ci == 0))
    def _():
        n = lax.broadcasted_iota(jnp.int32, (c, c), 0)
        m = lax.broadcasted_iota(jnp.int32, (c, c), 1)
        diff = (n - m).astype(F32)
        nrow = lax.broadcasted_iota(jnp.int32, (c, RET_DIM), 0).astype(F32)
        for h in range(RET_HEADS):
            dec_ref[h] = jnp.where(diff >= 0, jnp.exp(lg_ref[h] * jnp.maximum(diff, 0.0)), 0.0)
            xi_ref[h] = jnp.exp(lg_ref[h] * (nrow + 1.0))
            zeta_ref[h] = jnp.exp(lg_ref[h] * (float(c) - 1.0 - nrow))

    @pl.when(ci == 0)
    def _():
        st_ref[...] = s0_ref[...]

    for bl in range(bb):
        for h in range(RET_HEADS):
            lg = lg_ref[h]
            sl = slice(h * RET_DIM, (h + 1) * RET_DIM)
            qh = q_ref[bl, :, sl]
            kh = k_ref[bl, :, sl]
            vh = v_ref[bl, :, sl]
            state = st_ref[bl, h]
            inner = lax.dot_general(qh, kh, (((1,), (1,)), ((), ())), preferred_element_type=F32) * dec_ref[h]
            o = jnp.dot(inner.astype(BF16), vh, preferred_element_type=F32)
            cross = jnp.dot(qh, state.astype(BF16), preferred_element_type=F32)
            o = o + cross * xi_ref[h]
            kz = (kh.astype(F32) * zeta_ref[h]).astype(BF16)
            upd = lax.dot_general(kz, vh, (((0,), (0,)), ((), ())), preferred_element_type=F32)
            decay_c = jnp.exp(lg * jnp.full((1, RET_DIM), float(c), F32))
            st_ref[bl, h] = decay_c * state + upd
            on = o * lax.rsqrt(jnp.mean(o * o, axis=-1, keepdims=True) + EPS)
            o_ref[bl, :, sl] = (g_ref[bl, :, sl].astype(F32) * on).astype(o_ref.dtype)


def _retention(qk, q_col, k_col, v, v_col, gate, state0, log_g, *, s, c, bb):
    b = qk.shape[0]

    def seq(col):
        return pl.BlockSpec((bb, c, RET_W), lambda bi, ci, lg: (bi, ci, col))

    st_spec = pl.BlockSpec((bb, RET_HEADS, RET_DIM, RET_DIM), lambda bi, ci, lg: (bi, 0, 0, 0))
    grid_spec = pltpu.PrefetchScalarGridSpec(
        num_scalar_prefetch=1,
        grid=(b // bb, s // c),
        in_specs=[seq(q_col), seq(k_col), seq(v_col), seq(0), st_spec],
        out_specs=[seq(0), st_spec],
        scratch_shapes=[pltpu.VMEM((RET_HEADS, c, c), F32), pltpu.VMEM((RET_HEADS, c, RET_DIM), F32),
                        pltpu.VMEM((RET_HEADS, c, RET_DIM), F32)],
    )
    return pl.pallas_call(
        functools.partial(_ret_kernel, c=c, bb=bb),
        grid_spec=grid_spec,
        out_shape=[jax.ShapeDtypeStruct((b, s, RET_W), BF16),
                   jax.ShapeDtypeStruct((b, RET_HEADS, RET_DIM, RET_DIM), F32)],
        compiler_params=pltpu.CompilerParams(dimension_semantics=("arbitrary", "arbitrary")),
        name="retention",
    )(log_g, qk, qk, v, gate, state0)


def _mem_kernel(q_ref, k_ref, v_ref, o_ref, *, bb):
    for bl in range(bb):
        for h in range(MEM_HEADS):
            sl = slice(h * MEM_HEAD_DIM, (h + 1) * MEM_HEAD_DIM)
            qh = q_ref[bl, :, sl]
            kh = k_ref[bl, :, sl].astype(BF16)
            vh = v_ref[bl, :, sl].astype(BF16)
            s = lax.dot_general(qh, kh, (((1,), (1,)), ((), ())), preferred_element_type=F32)
            m = jnp.max(s, axis=1, keepdims=True)
            p = jnp.exp(s - m)
            den = jnp.sum(p, axis=1, keepdims=True)
            o = jnp.dot(p.astype(BF16), vh, preferred_element_type=F32)
            o_ref[bl, :, sl] = (o * (1.0 / den)).astype(o_ref.dtype)


def _mem_attend(q, q_col, mk, mv, *, s, tq, bb):
    b = q.shape[0]
    kv_spec = pl.BlockSpec((bb, N_MEM, MEM_W), lambda bi, i: (bi, 0, 0))
    return pl.pallas_call(
        functools.partial(_mem_kernel, bb=bb),
        grid=(b // bb, s // tq),
        in_specs=[pl.BlockSpec((bb, tq, MEM_W), lambda bi, i: (bi, i, q_col)), kv_spec, kv_spec],
        out_specs=pl.BlockSpec((bb, tq, MEM_W), lambda bi, i: (bi, i, 0)),
        out_shape=jax.ShapeDtypeStruct((b, s, MEM_W), BF16),
        compiler_params=pltpu.CompilerParams(dimension_semantics=("parallel", "parallel")),
        name="mem_attend",
    )(q, mk, mv)


def _merge_kernel(u_ref, oa_ref, ob_ref, oc_ref, wga_ref, wgb_ref, wgc_ref, ba_ref, bb_ref, bc_ref,
                  wa_ref, wb_ref, wc_ref, o_ref):
    u = u_ref[...]
    acc = None
    for o_r, wg_ref, b_ref, w_ref in ((oa_ref, wga_ref, ba_ref, wa_ref), (ob_ref, wgb_ref, bb_ref, wb_ref),
                                      (oc_ref, wgc_ref, bc_ref, wc_ref)):
        gate = _sigmoid(jnp.dot(u, wg_ref[...], preferred_element_type=F32) + b_ref[...])
        term = gate * jnp.dot(o_r[...], w_ref[...], preferred_element_type=F32)
        acc = term if acc is None else acc + term
    o_ref[...] = acc.astype(o_ref.dtype)


def _merge(u, o_swa, o_ret, o_mem, w_gate, gate_col0, b_gate, w_br, *, tm, tn):
    n, d = u.shape
    kb = o_swa.shape[1]
    nj = D_MODEL // tn
    g0 = gate_col0 // tn
    o_spec = pl.BlockSpec((tm, kb), lambda i, j: (i, 0))

    def wg_spec(br):
        return pl.BlockSpec((d, tn), lambda i, j: (0, g0 + br * nj + j))

    def b_spec(br):
        return pl.BlockSpec((1, tn), lambda i, j: (0, br * nj + j))

    def w_spec(br):
        return pl.BlockSpec((kb, tn), lambda i, j: (br, j))

    bias = b_gate.reshape(1, -1).astype(F32)
    return pl.pallas_call(
        _merge_kernel,
        grid=(n // tm, nj),
        in_specs=[pl.BlockSpec((tm, d), lambda i, j: (i, 0)), o_spec, o_spec, o_spec,
                  wg_spec(0), wg_spec(1), wg_spec(2), b_spec(0), b_spec(1), b_spec(2),
                  w_spec(0), w_spec(1), w_spec(2)],
        out_specs=pl.BlockSpec((tm, tn), lambda i, j: (i, j)),
        out_shape=jax.ShapeDtypeStruct((n, D_MODEL), BF16),
        compiler_params=pltpu.CompilerParams(dimension_semantics=("parallel", "parallel")),
        name="merge",
    )(u, o_swa, o_ret, o_mem, w_gate, w_gate, w_gate, bias, bias, bias, w_br, w_br, w_br)


def _ffn_up_kernel(u_ref, wug_ref, wuv_ref, wc_ref, b_ref, cb_ref, act_ref, co_ref,
                   ag_sc, av_sc, cg_sc, cv_sc, *, tm, tf, nf, nseg):
    i = pl.program_id(1)
    fb = pl.program_id(2)
    seg_t = tm // nseg
    stride = seg_t + 8
    if nseg == 1:
        @pl.when((i == 0) & (fb == 0))
        def _():
            for f in range(nf):
                cg_sc[f, 6:8, :] = cb_ref[0, f]
                cv_sc[f, 6:8, :] = cb_ref[0, nf + f]

    u = u_ref[0]
    for a_sc, wu_ref, blk, carry_sc in ((ag_sc, wug_ref, fb, cg_sc), (av_sc, wuv_ref, nf + fb, cv_sc)):
        a = jnp.dot(u, wu_ref[...], preferred_element_type=F32)
        for s in range(nseg):
            a_sc[s * stride + 8:(s + 1) * stride, :] = a[s * seg_t:(s + 1) * seg_t]
            co_ref[0, s, blk] = a[(s + 1) * seg_t - 8:(s + 1) * seg_t]
        if nseg == 1:
            a_sc[6:8, :] = carry_sc[fb, 6:8, :]
            carry_sc[fb] = a[tm - 8:tm]
        else:
            for s in range(nseg):
                a_sc[s * stride + 6:s * stride + 8, :] = cb_ref[s, blk]

    def conv(a_sc, blk, r0, c0):
        tap = lambda k: a_sc[r0 - k:r0 - k + CONV_ROWS, c0:c0 + LANE]
        wc = lambda k: wc_ref[blk, k:k + 1, c0:c0 + LANE]
        return b_ref[blk, :, c0:c0 + LANE] + (wc(0) * tap(2) + wc(1) * tap(1) + wc(2) * tap(0))

    for c0 in range(0, tf, LANE):
        for s in range(nseg):
            for r in range(0, seg_t, CONV_ROWS):
                cg = conv(ag_sc, fb, s * stride + 8 + r, c0)
                cv = conv(av_sc, nf + fb, s * stride + 8 + r, c0)
                act_ref[0, s * seg_t + r:s * seg_t + r + CONV_ROWS, c0:c0 + LANE] = (
                    cg * _sigmoid(cg) * cv).astype(act_ref.dtype)


def _ffn_up(u, conv_buf, w_up, w_conv, b_conv, *, tm, tf, nseg):
    bt, rows, d = u.shape
    nf = D_FF // tf
    nt = rows // tm
    nb = bt * nseg
    wc_blocks = w_conv.reshape(3, 2 * nf, tf).transpose(1, 0, 2)
    b_blocks = b_conv.reshape(2 * nf, 1, tf)
    cb_blocks = conv_buf.reshape(nb, 2, 2 * nf, tf).transpose(0, 2, 1, 3)
    whole = lambda *shape: pl.BlockSpec(shape, lambda b, i, f: (0,) * len(shape))
    return pl.pallas_call(
        functools.partial(_ffn_up_kernel, tm=tm, tf=tf, nf=nf, nseg=nseg),
        grid=(bt, nt, nf),
        in_specs=[pl.BlockSpec((1, tm, d), lambda b, i, f: (b, i, 0)),
                  pl.BlockSpec((d, tf), lambda b, i, f: (0, f)),
                  pl.BlockSpec((d, tf), lambda b, i, f: (0, nf + f)),
                  whole(2 * nf, 3, tf),
                  whole(2 * nf, 1, tf),
                  pl.BlockSpec((nseg, 2 * nf, 2, tf), lambda b, i, f: (b, 0, 0, 0))],
        out_specs=[pl.BlockSpec((1, tm, tf), lambda b, i, f: (b, i, f)),
                   pl.BlockSpec((1, nseg, 2 * nf, 8, tf), lambda b, i, f: (b * nt + i, 0, 0, 0, 0))],
        out_shape=[jax.ShapeDtypeStruct((bt, rows, D_FF), BF16),
                   jax.ShapeDtypeStruct((bt * nt, nseg, 2 * nf, 8, tf), F32)],
        scratch_shapes=[pltpu.VMEM((tm + 8 * nseg, tf), F32), pltpu.VMEM((tm + 8 * nseg, tf), F32),
                        pltpu.VMEM((nf, 8, tf), F32), pltpu.VMEM((nf, 8, tf), F32)],
        compiler_params=pltpu.CompilerParams(dimension_semantics=("arbitrary", "arbitrary", "arbitrary")),
        name="ffn_up",
    )(u, w_up, w_up, wc_blocks, b_blocks, cb_blocks)


def _ffn_down_kernel(act_ref, wd_ref, h_ref, g_ref, y_ref, *, nd, td):
    j = pl.program_id(1)
    for jj in range(nd):
        @pl.when(j == jj)
        def _():
            y_ref[:, jj * td:(jj + 1) * td] = h_ref[...] + jnp.dot(act_ref[...], wd_ref[...],
                                                                   preferred_element_type=F32)

    @pl.when(j == nd - 1)
    def _():
        y_ref[...] = _rms(y_ref[...], g_ref[...])


def _ffn_down(act, w_down, h, g_final, *, tm, td):
    n, f = act.shape
    d = h.shape[1]
    nd = d // td
    return pl.pallas_call(
        functools.partial(_ffn_down_kernel, nd=nd, td=td),
        grid=(n // tm, nd),
        in_specs=[pl.BlockSpec((tm, f), lambda i, j: (i, 0)),
                  pl.BlockSpec((f, td), lambda i, j: (0, j)),
                  pl.BlockSpec((tm, td), lambda i, j: (i, j)),
                  pl.BlockSpec((1, d), lambda i, j: (0, 0))],
        out_specs=pl.BlockSpec((tm, d), lambda i, j: (i, 0)),
        out_shape=jax.ShapeDtypeStruct((n, d), F32),
        compiler_params=pltpu.CompilerParams(dimension_semantics=("parallel", "arbitrary")),
        name="ffn_down",
    )(act, w_down, h, g_final.reshape(1, d))


def _out_proj_kernel(x_ref, w_ref, res_ref, g_ref, h_ref, u_ref):
    h = res_ref[...] + jnp.dot(x_ref[...], w_ref[...], preferred_element_type=F32)
    h_ref[...] = h
    u_ref[...] = _rms(h, g_ref[...]).astype(u_ref.dtype)


def _out_proj(x, w, residual, g, *, tm):
    n, k = x.shape
    d = w.shape[1]
    row = lambda width: pl.BlockSpec((tm, width), lambda i: (i, 0))
    return pl.pallas_call(
        _out_proj_kernel,
        grid=(n // tm,),
        in_specs=[row(k), pl.BlockSpec((k, d), lambda i: (0, 0)), row(d), pl.BlockSpec((1, d), lambda i: (0, 0))],
        out_specs=[row(d), row(d)],
        out_shape=[jax.ShapeDtypeStruct((n, d), F32), jax.ShapeDtypeStruct((n, d), BF16)],
        compiler_params=pltpu.CompilerParams(dimension_semantics=("parallel",)),
        name="proj_out",
    )(x, w, residual, g.reshape(1, d))


def _rope_tables(pos0, s):
    half = RET_DIM // 2
    inv_freq = 1.0 / (RET_ROPE_BASE ** np.linspace(0.0, 1.0, half))
    ang = (pos0 + np.arange(s))[:, None] * inv_freq[None, :]
    cos, sin = np.cos(ang), np.sin(ang)
    return (jnp.asarray(np.concatenate([cos, cos], axis=1), F32),
            jnp.asarray(np.concatenate([-sin, sin], axis=1), F32))


def _prep_weights(g_mix, w_in, b_gate, sink, w_br, w_o, g_ffn, w_up, w_conv, b_conv, w_down):
    o = 0
    cuts = {}
    for name, width in (("qa", SWA_Q), ("ka", SWA_KV), ("va", SWA_KV), ("qr", RET_W), ("kr", RET_W),
                        ("vr", RET_W), ("gr", RET_W), ("qm", MEM_W), ("gl", N_BRANCH * D_MODEL)):
        cuts[name] = (o, o + width)
        o += width
    cols = lambda *names: jnp.concatenate([w_in[:, cuts[n][0]:cuts[n][1]] for n in names], axis=1).astype(BF16)
    ones = lambda n, v: jnp.full((n,), v, F32)
    return dict(
        g_mix=g_mix, b_gate=b_gate, sink=sink, g_ffn=g_ffn, w_conv=w_conv, b_conv=b_conv,
        w_plain=cols("qa", "vr", "qm"),
        s_plain=jnp.concatenate([ones(SWA_Q, SWA_HEAD_DIM ** -0.5), ones(RET_W, 1.0),
                                 ones(MEM_W, MEM_HEAD_DIM ** -0.5)]),
        w_kv=cols("ka", "va"),
        w_rot=cols("qr", "kr"),
        s_rot=jnp.concatenate([ones(RET_W, 1.0), ones(RET_W, RET_DIM ** -0.5)]),
        w_g=cols("gr"),
        w_in16=w_in.astype(BF16), gate_col0=cuts["gl"][0],
        w_br=w_br.astype(BF16), w_o=w_o.astype(BF16), w_up=w_up.astype(BF16), w_down=w_down.astype(BF16),
    )


def _run_group(x, pos0, mem_k, mem_v, swa_cache, ret_state, conv_buf, wts, g_final, log_g):
    b, s, d = x.shape
    n = b * s
    tm = min(ROW_TILE, n)
    x2 = x.reshape(n, d)
    plain, u = _norm_mm(x2, wts["g_mix"], wts["w_plain"], wts["s_plain"], tm=min(OUT_PROJ_ROWS, n),
                        tn=wts["w_plain"].shape[1], name="proj_plain")
    kv = _mm(u, wts["w_kv"], epilogue="plain", out_dtype=F32, tm=tm, tn=2 * SWA_KV, name="proj_kv")
    cc, ss = _rope_tables(pos0, s)
    if s < tm:
        cc, ss = jnp.tile(cc, (tm // s, 1)), jnp.tile(ss, (tm // s, 1))
    rot = _mm(u, wts["w_rot"], epilogue="rotary", out_dtype=BF16, tm=tm, tn=COL_TILE,
              colvec=wts["s_rot"], tables=(cc, ss), name="proj_rot")
    sgate = _mm(u, wts["w_g"], epilogue="silu", out_dtype=BF16, tm=tm, tn=COL_TILE, name="proj_silu")

    plain3 = plain.reshape(b, s, -1)
    kv3 = kv.reshape(b, s, 2 * SWA_KV)
    if swa_cache is None:
        o_swa = _swa(plain3, 0, kv3, 0, kv3, 1, wts["sink"], sq=s, qt=min(SWA_ROWS, s), kv_off=0,
                     bb=min(SWA_SEQS, b))
        new_k = kv3[:, s - WINDOW:, :SWA_KV]
        new_v = kv3[:, s - WINDOW:, SWA_KV:]
    else:
        ck = swa_cache[0].reshape(b, -1, SWA_KV)
        cv = swa_cache[1].reshape(b, -1, SWA_KV)
        n_keep = ck.shape[1]
        k_all = jnp.concatenate([ck, kv3[:, :, :SWA_KV]], axis=1)
        v_all = jnp.concatenate([cv, kv3[:, :, SWA_KV:]], axis=1)
        o_swa = _swa(plain3, 0, k_all, 0, v_all, 0, wts["sink"], sq=s, qt=s, kv_off=n_keep, bb=min(SWA_SEQS, b))
        new_k = k_all[:, -n_keep:]
        new_v = v_all[:, -n_keep:]
    state0 = jnp.zeros((b, RET_HEADS, RET_DIM, RET_DIM), F32) if ret_state is None else ret_state
    o_ret, s_new = _retention(rot.reshape(b, s, -1), 0, 1, plain3, 1, sgate.reshape(b, s, -1), state0, log_g,
                              s=s, c=min(RET_ROWS, s), bb=min(RET_SEQS, b))
    o_mem = _mem_attend(plain3, 2, mem_k.reshape(b, N_MEM, MEM_W), mem_v.reshape(b, N_MEM, MEM_W),
                        s=s, tq=min(MEM_ROWS, s), bb=min(MEM_SEQS, b))
    merged = _merge(u, o_swa.reshape(n, -1), o_ret.reshape(n, -1), o_mem.reshape(n, -1), wts["w_in16"],
                    wts["gate_col0"], wts["b_gate"], wts["w_br"], tm=tm, tn=MERGE_COLS)
    h1, u2 = _out_proj(merged, wts["w_o"], x2, wts["g_ffn"], tm=min(OUT_PROJ_ROWS, n))
    if s >= ROW_TILE:
        act, co = _ffn_up(u2.reshape(b, s, d), conv_buf, wts["w_up"], wts["w_conv"], wts["b_conv"],
                          tm=ROW_TILE, tf=FF_TILE, nseg=1)
    else:
        act, co = _ffn_up(u2.reshape(1, n, d), conv_buf, wts["w_up"], wts["w_conv"], wts["b_conv"],
                          tm=n, tf=FF_TILE, nseg=b)
    y = _ffn_down(act.reshape(n, D_FF), wts["w_down"], h1, g_final, tm=tm, td=FF_TILE)
    nblk, tfb = co.shape[2], co.shape[4]
    last = co.reshape(b, -1, nblk, 8, tfb)[:, -1, :, 6:8, :]
    new_buf = last.transpose(0, 2, 1, 3).reshape(b, 2, nblk * tfb)
    return (y.reshape(b, s, d), new_k.reshape(b, -1, SWA_KV_HEADS, SWA_HEAD_DIM),
            new_v.reshape(b, -1, SWA_KV_HEADS, SWA_HEAD_DIM), s_new, new_buf)


def _memory_kv(mem, g_mem, w_mem_kv):
    b, m, d = mem.shape
    u = _rmsnorm(mem.reshape(b * m, d), g_mem, min(OUT_PROJ_ROWS, b * m))
    kv = _mm(u, w_mem_kv.astype(BF16), epilogue="plain", out_dtype=F32, tm=min(ROW_TILE, b * m), tn=COL_TILE,
             name="mem_kv")
    return (kv[:, :MEM_W].reshape(b, m, MEM_HEADS, MEM_HEAD_DIM), kv[:, MEM_W:].reshape(b, m, MEM_HEADS, MEM_HEAD_DIM))


def kernel(x_prompt, x_sample, mem_prompt, cache_swa_k, cache_swa_v, state_ret, state_ffn_conv, cache_mem_k, cache_mem_v, g_mix, w_in, b_gate, sink, w_br, w_o, g_mem, w_mem_kv, g_ffn, w_up, w_conv, b_conv, w_down, g_final):
    bp = x_prompt.shape[0]
    depth = w_in.shape[0]
    assert depth == 1, "single-layer problem: the final norm is fused into the layer's FFN"
    log_g = jnp.log1p(-jnp.exp2(-5.0 - jnp.arange(RET_HEADS, dtype=F32)))
    hp, hs = x_prompt, x_sample
    outs_p = [[] for _ in range(6)]
    outs_s = [[] for _ in range(4)]
    for l in range(depth):
        wts = _prep_weights(g_mix[l], w_in[l], b_gate[l], sink[l], w_br[l], w_o[l],
                            g_ffn[l], w_up[l], w_conv[l], b_conv[l], w_down[l])
        mk, mv = _memory_kv(mem_prompt, g_mem[l], w_mem_kv[l])
        zero_buf = jnp.zeros((bp, 2, 2 * D_FF), F32)
        hp, kp, vp, sp, cp = _run_group(hp, 0, mk, mv, None, None, zero_buf, wts, g_final, log_g)
        hs, ksn, vsn, ssn, csn = _run_group(hs, PAST_LEN, cache_mem_k[l], cache_mem_v[l],
                                            (cache_swa_k[l], cache_swa_v[l]), state_ret[l],
                                            state_ffn_conv[l], wts, g_final, log_g)
        for lst, val in zip(outs_p, (kp, vp, sp, cp, mk, mv)):
            lst.append(val)
        for lst, val in zip(outs_s, (ksn, vsn, ssn, csn)):
            lst.append(val)
    return (hp, hs, *[jnp.stack(v) for v in outs_p], *[jnp.stack(v) for v in outs_s])
```
